```python
import math
import jax, jax.numpy as jnp
from jax import lax
import numpy as np

D_MODEL = 1024
BATCH = 16
SEQ = 2048
DEPTH = 1

HEAD_DIM = 64
MOBA_HEADS = 8
MOBA_WIDTH = MOBA_HEADS * HEAD_DIM
MOBA_BLOCK = 256
MOBA_TOPK = 3
MOBA_QCHUNK = 32
MLA_HEADS = 8
MLA_Q_RANK = 256
MLA_KV_RANK = 128
MLA_NOPE_DIM = 64
MLA_ROPE_DIM = 32
MLA_V_DIM = 64
MLA_QK_DIM = MLA_NOPE_DIM + MLA_ROPE_DIM
MLA_WIDTH = MLA_HEADS * MLA_V_DIM
MIX_WIDTH = MOBA_WIDTH + MLA_WIDTH
IN_WIDTH = 3 * MOBA_WIDTH + MLA_Q_RANK + MLA_KV_RANK + MLA_ROPE_DIM
ATTN_QBLOCK = 128
D_FF = -(-8 * D_MODEL // (3 * 256)) * 256
ROPE_THETA = 10000.0
EPS = 1e-6

kernel_name = "hybrid_moba_mla_parallel_heads"


def rms_norm(x, g):
    xf = x.astype(jnp.float32)
    y = xf * lax.rsqrt(jnp.mean(xf * xf, axis=-1, keepdims=True) + EPS)
    return (y * g.astype(jnp.float32)).astype(x.dtype)


def rope_tables(seq, dim):
    inv = ROPE_THETA ** (-jnp.arange(0, dim, 2, dtype=jnp.float32) / dim)
    ang = jnp.arange(seq, dtype=jnp.float32)[:, None] * inv[None, :]
    return jnp.cos(ang), jnp.sin(ang)


def apply_rope(x, cos, sin):
    half = x.shape[-1] // 2
    xf = x.astype(jnp.float32)
    x1, x2 = xf[..., :half], xf[..., half:]
    c, s = cos[:, None, :], sin[:, None, :]
    return jnp.concatenate([x1 * c - x2 * s, x2 * c + x1 * s], axis=-1).astype(x.dtype)


def moba_attention(q, k, v):
    B, S, H, D = q.shape
    f32 = jnp.float32
    nb = -(-S // MOBA_BLOCK)
    sp = nb * MOBA_BLOCK
    pad = sp - S
    q, k, v = [jnp.pad(t, ((0, 0), (0, pad), (0, 0), (0, 0))).transpose(0, 2, 1, 3)
               for t in (q, k, v)]
    scale = D ** -0.5
    blk = jnp.arange(sp) // MOBA_BLOCK
    qb = q.reshape(B, H, nb, MOBA_BLOCK, D)
    kb = k.reshape(B, H, nb, MOBA_BLOCK, D)
    vb = v.reshape(B, H, nb, MOBA_BLOCK, D)

    s_self = jnp.einsum('bhnqd,bhnkd->bhnqk', qb, kb, preferred_element_type=f32) * scale
    tri = jnp.tril(jnp.ones((MOBA_BLOCK, MOBA_BLOCK), dtype=bool))
    s_self = jnp.where(tri, s_self, -jnp.inf)
    m_self = jnp.max(s_self, axis=-1, keepdims=True)
    p_self = jnp.exp(s_self - m_self)
    l_self = jnp.sum(p_self, axis=-1, keepdims=True)
    o_self = jnp.einsum('bhnqk,bhnkd->bhnqd', p_self, vb.astype(f32)) / l_self
    lse_self = (m_self + jnp.log(l_self))[..., 0].reshape(B, H, sp)
    o_self = o_self.reshape(B, H, sp, D)

    n_sel = min(MOBA_TOPK, nb - 1)
    if n_sel == 0:
        o = o_self
    else:
        k_mean = jnp.mean(kb.astype(f32), axis=3)
        gate = jnp.einsum('bhsd,bhnd->bhsn', q.astype(f32), k_mean)
        past = jnp.arange(nb)[None, :] < blk[:, None]
        gate = jnp.where(past, gate, -jnp.inf)
        _, idx = lax.top_k(gate, n_sel)
        valid = jnp.arange(n_sel)[None, :] < blk[:, None]

        nc = sp // MOBA_QCHUNK

        def to_chunks(t):
            t = t.reshape(B, H, nc, MOBA_QCHUNK, *t.shape[3:])
            return jnp.moveaxis(t, 2, 0)

        bi = jnp.arange(B)[:, None, None, None]
        hi = jnp.arange(H)[None, :, None, None]

        def past_chunk(args):
            qc, ic, vc = args
            kg = kb[bi, hi, ic]
            vg = vb[bi, hi, ic]
            s = jnp.einsum('bhqd,bhqnjd->bhqnj', qc, kg, preferred_element_type=f32) * scale
            s = jnp.where(vc[None, None, :, :, None], s, -jnp.inf)
            m = jnp.max(s, axis=(-2, -1), keepdims=True)
            m = jnp.where(jnp.isfinite(m), m, 0.0)
            p = jnp.exp(s - m)
            l = jnp.sum(p, axis=(-2, -1))
            o = jnp.einsum('bhqnj,bhqnjd->bhqd', p, vg.astype(f32))
            o = o / jnp.where(l > 0, l, 1.0)[..., None]
            lse = m[..., 0, 0] + jnp.log(l)
            return o, lse

        o_past, lse_past = lax.map(
            past_chunk,
            (to_chunks(q), to_chunks(idx), valid.reshape(nc, MOBA_QCHUNK, n_sel)))
        o_past = jnp.moveaxis(o_past, 0, 2).reshape(B, H, sp, D)
        lse_past = jnp.moveaxis(lse_past, 0, 2).reshape(B, H, sp)
        m = jnp.maximum(lse_self, lse_past)
        w_s = jnp.exp(lse_self - m)
        w_p = jnp.exp(lse_past - m)
        o = (w_s[..., None] * o_self + w_p[..., None] * o_past) / (w_s + w_p)[..., None]
    return o[:, :, :S].transpose(0, 2, 1, 3).astype(v.dtype)


def causal_attention(q, k, v):
    B, S, H, Dk = q.shape
    Dv = v.shape[-1]
    scale = Dk ** -0.5
    nqb = S // ATTN_QBLOCK
    qb = q.reshape(B, nqb, ATTN_QBLOCK, H, Dk).transpose(1, 0, 3, 2, 4)
    kt = k.transpose(0, 2, 1, 3)
    vt = v.transpose(0, 2, 1, 3)
    kpos = jnp.arange(S)

    def block(args):
        qblk, start = args
        s = jnp.einsum('bhqd,bhkd->bhqk', qblk, kt, preferred_element_type=jnp.float32) * scale
        qpos = start + jnp.arange(ATTN_QBLOCK)
        s = jnp.where(kpos[None, :] <= qpos[:, None], s, -jnp.inf)
        p = jax.nn.softmax(s, axis=-1)
        return jnp.einsum('bhqk,bhkd->bhqd', p, vt.astype(jnp.float32))

    o = lax.map(block, (qb, jnp.arange(nqb) * ATTN_QBLOCK))
    return o.transpose(1, 0, 3, 2, 4).reshape(B, S, H, Dv).astype(v.dtype)


def mla_attention(c_q, c_kv, k_pe, q_a_g, w_q_up, kv_a_g, w_kv_up, q_norm_g, k_norm_g, cos, sin):
    B, S, _ = c_q.shape
    q = (rms_norm(c_q, q_a_g) @ w_q_up).reshape(B, S, MLA_HEADS, MLA_QK_DIM)
    kv = (rms_norm(c_kv, kv_a_g) @ w_kv_up).reshape(B, S, MLA_HEADS, MLA_NOPE_DIM + MLA_V_DIM)
    k_nope, v = kv[..., :MLA_NOPE_DIM], kv[..., MLA_NOPE_DIM:]
    k_rope = jnp.broadcast_to(k_pe[:, :, None, :], (B, S, MLA_HEADS, MLA_ROPE_DIM))
    k = jnp.concatenate([k_nope, k_rope], axis=-1)
    q = rms_norm(q, q_norm_g)
    k = rms_norm(k, k_norm_g)
    q = jnp.concatenate([q[..., :MLA_NOPE_DIM], apply_rope(q[..., MLA_NOPE_DIM:], cos, sin)], axis=-1)
    k = jnp.concatenate([k[..., :MLA_NOPE_DIM], apply_rope(k[..., MLA_NOPE_DIM:], cos, sin)], axis=-1)
    return causal_attention(q, k, v)


def setup_inputs(seed: int = 0) -> dict:
    key = jax.random.key(seed)
    ks = jax.random.split(key, 16)
    f32 = jnp.float32

    def w(k, shape, fan_in):
        return jax.random.normal(k, (DEPTH,) + shape, f32) * (fan_in ** -0.5)

    def g(k, n):
        return 1.0 + 0.02 * jax.random.normal(k, (DEPTH, n), f32)

    return {
        "x": jax.random.normal(ks[0], (BATCH, SEQ, D_MODEL), f32),
        "attn_norm_g": g(ks[1], D_MODEL),
        "w_in": w(ks[2], (D_MODEL, IN_WIDTH), D_MODEL),
        "moba_q_norm_g": g(ks[3], HEAD_DIM),
        "moba_k_norm_g": g(ks[4], HEAD_DIM),
        "mla_q_a_norm_g": g(ks[5], MLA_Q_RANK),
        "w_q_up": w(ks[6], (MLA_Q_RANK, MLA_HEADS * MLA_QK_DIM), MLA_Q_RANK),
        "mla_kv_a_norm_g": g(ks[7], MLA_KV_RANK),
        "w_kv_up": w(ks[8], (MLA_KV_RANK, MLA_HEADS * (MLA_NOPE_DIM + MLA_V_DIM)), MLA_KV_RANK),
        "mla_q_norm_g": g(ks[9], MLA_QK_DIM),
        "mla_k_norm_g": g(ks[10], MLA_QK_DIM),
        "w_o": w(ks[11], (MIX_WIDTH, D_MODEL), MIX_WIDTH),
        "ffn_norm_g": g(ks[12], D_MODEL),
        "w_gate": w(ks[13], (D_MODEL, D_FF), D_MODEL),
        "w_up": w(ks[14], (D_MODEL, D_FF), D_MODEL),
        "w_down": w(ks[15], (D_FF, D_MODEL), D_FF),
    }


def reference(x, attn_norm_g, w_in, moba_q_norm_g, moba_k_norm_g, mla_q_a_norm_g, w_q_up,
              mla_kv_a_norm_g, w_kv_up, mla_q_norm_g, mla_k_norm_g, w_o, ffn_norm_g,
              w_gate, w_up, w_down):
    B, S, _ = x.shape
    cos_a, sin_a = rope_tables(S, HEAD_DIM)
    cos_b, sin_b = rope_tables(S, MLA_ROPE_DIM)
    split_pts = [MOBA_WIDTH, 2 * MOBA_WIDTH, 3 * MOBA_WIDTH,
                 3 * MOBA_WIDTH + MLA_Q_RANK,
                 3 * MOBA_WIDTH + MLA_Q_RANK + MLA_KV_RANK]
    h = x
    for l in range(DEPTH):
        hn = rms_norm(h, attn_norm_g[l])
        proj = hn @ w_in[l]
        q_a, k_a, v_a, c_q, c_kv, k_pe = jnp.split(proj, split_pts, axis=-1)

        q_a = rms_norm(q_a.reshape(B, S, MOBA_HEADS, HEAD_DIM), moba_q_norm_g[l])
        k_a = rms_norm(k_a.reshape(B, S, MOBA_HEADS, HEAD_DIM), moba_k_norm_g[l])
        q_a = apply_rope(q_a, cos_a, sin_a)
        k_a = apply_rope(k_a, cos_a, sin_a)
        v_a = v_a.reshape(B, S, MOBA_HEADS, HEAD_DIM)
        o_a = moba_attention(q_a, k_a, v_a).reshape(B, S, MOBA_WIDTH)

        o_b = mla_attention(c_q, c_kv, k_pe, mla_q_a_norm_g[l], w_q_up[l], mla_kv_a_norm_g[l],
                            w_kv_up[l], mla_q_norm_g[l], mla_k_norm_g[l], cos_b, sin_b)
        o_b = o_b.reshape(B, S, MLA_WIDTH)

        h = h + jnp.concatenate([o_a, o_b], axis=-1) @ w_o[l]

        gn = rms_norm(h, ffn_norm_g[l])
        h = h + (jax.nn.silu(gn @ w_gate[l]) * (gn @ w_up[l])) @ w_down[l]
    return h.astype(x.dtype)
```

```python
import functools

import jax
import jax.numpy as jnp
from jax import lax
from jax.experimental import pallas as pl
from jax.experimental.pallas import tpu as pltpu

HEAD_DIM = 64
MOBA_HEADS = 8
MOBA_WIDTH = MOBA_HEADS * HEAD_DIM
MOBA_BLOCK = 256
MOBA_TOPK = 3
MLA_HEADS = 8
MLA_Q_RANK = 256
MLA_KV_RANK = 128
MLA_NOPE_DIM = 64
MLA_ROPE_DIM = 32
MLA_V_DIM = 64
MLA_QK_DIM = MLA_NOPE_DIM + MLA_ROPE_DIM
MLA_QK_PAD = 128
MLA_WIDTH = MLA_HEADS * MLA_V_DIM
ROPE_THETA = 10000.0
EPS = 1e-6

NEG_BIG = -1e30
ATTN_BLOCK = 256
HEADS_PER_STEP = 2
V7X_VMEM_LIMIT = 56 * 1024 * 1024

F32 = jnp.float32
BF16 = jnp.bfloat16

_NT = (((1,), (1,)), ((), ()))


def _dot(a, b):
    return jnp.dot(a, b, preferred_element_type=F32)


def _rope_t(x, cos, sin):
    half = x.shape[0] // 2
    x1, x2 = x[:half], x[half:]
    return jnp.concatenate([x1 * cos - x2 * sin, x2 * cos + x1 * sin], axis=0)


def _proj_kernel(x_ref, g_attn_ref, w_in_t_ref, w_qup_t_ref, w_kvup_t_ref,
                 gq_a_ref, gk_a_ref, g_cq_ref, g_ckv_ref, gq_b_ref, gk_b_ref,
                 cos_a_ref, sin_a_ref, cos_b_ref, sin_b_ref,
                 qa_t_ref, ka_ref, va_t_ref, qb_t_ref, kb_ref, vb_t_ref):
    tm = x_ref.shape[0]
    x = x_ref[...]
    hn = x * lax.rsqrt(jnp.mean(x * x, axis=-1, keepdims=True) + EPS) * g_attn_ref[...]
    hn = hn.astype(BF16)
    proj_t = lax.dot_general(w_in_t_ref[...], hn, _NT, preferred_element_type=F32)

    o_k = MOBA_WIDTH
    o_v = 2 * MOBA_WIDTH
    o_cq = 3 * MOBA_WIDTH
    o_ckv = o_cq + MLA_Q_RANK
    o_pe = o_ckv + MLA_KV_RANK

    cos_a, sin_a = cos_a_ref[...], sin_a_ref[...]
    cos_b, sin_b = cos_b_ref[...], sin_b_ref[...]

    gq_a, gk_a = gq_a_ref[...], gk_a_ref[...]
    moba_scale = HEAD_DIM ** -0.5
    ka_parts = []
    for h in range(MOBA_HEADS):
        q = proj_t[h * HEAD_DIM:(h + 1) * HEAD_DIM]
        q = q * lax.rsqrt(jnp.mean(q * q, axis=0, keepdims=True) + EPS) * gq_a
        q = _rope_t(q, cos_a, sin_a) * moba_scale
        qa_t_ref[h * HEAD_DIM:(h + 1) * HEAD_DIM, :] = q.astype(BF16)
        k = proj_t[o_k + h * HEAD_DIM:o_k + (h + 1) * HEAD_DIM]
        k = k * lax.rsqrt(jnp.mean(k * k, axis=0, keepdims=True) + EPS) * gk_a
        ka_parts.append(_rope_t(k, cos_a, sin_a))
    ka_ref[...] = jnp.concatenate(ka_parts, axis=0).T.astype(BF16)
    va_t_ref[...] = proj_t[o_v:o_cq].astype(BF16)

    cq = proj_t[o_cq:o_ckv]
    cq = cq * lax.rsqrt(jnp.mean(cq * cq, axis=0, keepdims=True) + EPS) * g_cq_ref[...]
    qb = _dot(w_qup_t_ref[...], cq.astype(BF16))
    ckv = proj_t[o_ckv:o_pe]
    ckv = ckv * lax.rsqrt(jnp.mean(ckv * ckv, axis=0, keepdims=True) + EPS) * g_ckv_ref[...]
    kv = _dot(w_kvup_t_ref[...], ckv.astype(BF16))
    k_pe = proj_t[o_pe:o_pe + MLA_ROPE_DIM]

    gq_b, gk_b = gq_b_ref[...], gk_b_ref[...]
    mla_scale = MLA_QK_DIM ** -0.5
    pe_ss = jnp.sum(k_pe * k_pe, axis=0, keepdims=True)
    pe_rot = _rope_t(k_pe * gk_b[MLA_NOPE_DIM:], cos_b, sin_b)
    pad = jnp.zeros((MLA_QK_PAD - MLA_QK_DIM, tm), F32)
    kv_w = MLA_NOPE_DIM + MLA_V_DIM
    kb_parts = []
    for h in range(MLA_HEADS):
        q = qb[h * MLA_QK_DIM:(h + 1) * MLA_QK_DIM]
        q = q * lax.rsqrt(jnp.mean(q * q, axis=0, keepdims=True) + EPS) * gq_b
        q = jnp.concatenate(
            [q[:MLA_NOPE_DIM], _rope_t(q[MLA_NOPE_DIM:], cos_b, sin_b)], axis=0) * mla_scale
        qb_t_ref[h * MLA_QK_PAD:(h + 1) * MLA_QK_PAD, :] = (
            jnp.concatenate([q, pad], axis=0).astype(BF16))
        k_nope = kv[h * kv_w:h * kv_w + MLA_NOPE_DIM]
        ss = jnp.sum(k_nope * k_nope, axis=0, keepdims=True) + pe_ss
        r = lax.rsqrt(ss / MLA_QK_DIM + EPS)
        kb_parts += [k_nope * r * gk_b[:MLA_NOPE_DIM], pe_rot * r, pad]
        vb_t_ref[h * MLA_V_DIM:(h + 1) * MLA_V_DIM, :] = (
            kv[h * kv_w + MLA_NOPE_DIM:(h + 1) * kv_w].astype(BF16))
    kb_ref[...] = jnp.concatenate(kb_parts, axis=0).T.astype(BF16)


def _col(v):
    return v.astype(F32).reshape(-1, 1)


def _proj_call(x2, g_attn, w_in_t, w_qup_t, w_kvup_t, gq_a, gk_a, g_cq, g_ckv, gq_b, gk_b,
               cos_a, sin_a, cos_b, sin_b, *, seq, tm):
    t, d = x2.shape
    n_pos = seq // tm
    const = lambda i: (0, 0)
    row = lambda i: (i, 0)
    colb = lambda i: (0, i)
    pos = lambda i: (0, i % n_pos)

    def full(a):
        return pl.BlockSpec(a.shape, const)

    out_shape = [
        jax.ShapeDtypeStruct((MOBA_WIDTH, t), BF16),
        jax.ShapeDtypeStruct((t, MOBA_WIDTH), BF16),
        jax.ShapeDtypeStruct((MOBA_WIDTH, t), BF16),
        jax.ShapeDtypeStruct((MLA_HEADS * MLA_QK_PAD, t), BF16),
        jax.ShapeDtypeStruct((t, MLA_HEADS * MLA_QK_PAD), BF16),
        jax.ShapeDtypeStruct((MLA_WIDTH, t), BF16),
    ]
    out_specs = [
        pl.BlockSpec((MOBA_WIDTH, tm), colb),
        pl.BlockSpec((tm, MOBA_WIDTH), row),
        pl.BlockSpec((MOBA_WIDTH, tm), colb),
        pl.BlockSpec((MLA_HEADS * MLA_QK_PAD, tm), colb),
        pl.BlockSpec((tm, MLA_HEADS * MLA_QK_PAD), row),
        pl.BlockSpec((MLA_WIDTH, tm), colb),
    ]
    in_specs = [
        pl.BlockSpec((tm, d), row), full(g_attn), full(w_in_t), full(w_qup_t), full(w_kvup_t),
        full(gq_a), full(gk_a), full(g_cq), full(g_ckv), full(gq_b), full(gk_b),
        pl.BlockSpec((HEAD_DIM // 2, tm), pos), pl.BlockSpec((HEAD_DIM // 2, tm), pos),
        pl.BlockSpec((MLA_ROPE_DIM // 2, tm), pos), pl.BlockSpec((MLA_ROPE_DIM // 2, tm), pos),
    ]
    return pl.pallas_call(
        _proj_kernel, out_shape=out_shape, grid=(t // tm,),
        in_specs=in_specs, out_specs=out_specs, name="proj_heads",
        compiler_params=pltpu.CompilerParams(
            dimension_semantics=("parallel",), vmem_limit_bytes=V7X_VMEM_LIMIT),
    )(x2, g_attn, w_in_t, w_qup_t, w_kvup_t, gq_a, gk_a, g_cq, g_ckv, gq_b, gk_b,
      cos_a, sin_a, cos_b, sin_b)


def _softmax_block(s_t, m, l, acc, v_t):
    m_new = jnp.maximum(m, jnp.max(s_t, axis=0, keepdims=True))
    alpha = jnp.exp(m - m_new)
    p = jnp.exp(s_t - m_new)
    l = alpha * l + jnp.sum(p, axis=0, keepdims=True)
    acc = alpha * acc + _dot(v_t, p.astype(BF16))
    return m_new, l, acc


def _diag_block(k_d, q_w, v_t):
    s_t = _dot(k_d, q_w)
    key_i = lax.broadcasted_iota(jnp.int32, s_t.shape, 0)
    qry_i = lax.broadcasted_iota(jnp.int32, s_t.shape, 1)
    s_t = jnp.where(key_i <= qry_i, s_t, NEG_BIG)
    m = jnp.max(s_t, axis=0, keepdims=True)
    p = jnp.exp(s_t - m)
    l = jnp.sum(p, axis=0, keepdims=True)
    acc = _dot(v_t, p.astype(BF16))
    return m, l, acc


def _split3(a):
    hi = a.astype(BF16)
    r1 = a - hi.astype(F32)
    mid = r1.astype(BF16)
    lo = (r1 - mid.astype(F32)).astype(BF16)
    return hi, mid, lo


def _moba_kernel(q_t_ref, k_ref, v_t_ref, o_t_ref, bias_ref):
    seq = k_ref.shape[0]
    nb = seq // MOBA_BLOCK
    blk = MOBA_BLOCK
    n_sel = min(MOBA_TOPK, nb - 1)

    k_mean = jnp.concatenate(
        [jnp.mean(k_ref[j * blk:(j + 1) * blk, :].astype(F32), axis=0, keepdims=True)
         for j in range(nb)], axis=0)
    km_hi, km_mid, km_lo = _split3(k_mean)
    row_i = lax.broadcasted_iota(jnp.int32, (nb, blk), 0)
    zeros = jnp.zeros((HEAD_DIM, blk), BF16)

    for hh in range(HEADS_PER_STEP):
        rows = slice(hh * HEAD_DIM, (hh + 1) * HEAD_DIM)
        for i in range(nb):
            cols = slice(i * blk, (i + 1) * blk)
            q_h = q_t_ref[rows, cols]
            q_w = jnp.concatenate([q_h, zeros] if hh == 0 else [zeros, q_h], axis=0)
            m, l, acc = _diag_block(k_ref[cols, :], q_w, v_t_ref[rows, cols])
            if i > 0 and n_sel > 0:
                gate = _dot(km_hi, q_w) + _dot(km_mid, q_w) + _dot(km_lo, q_w)
                gate = jnp.where(row_i < i, gate, -jnp.inf)
                rank = jnp.zeros((nb, blk), jnp.int32)
                for jp in range(nb):
                    g_jp = gate[jp:jp + 1, :]
                    beats = (g_jp > gate) | ((g_jp == gate) & (jp < row_i))
                    rank = rank + beats.astype(jnp.int32)
                sel = (rank < n_sel) & (row_i < i)
                bias_ref[...] = jnp.where(sel, 0.0, NEG_BIG)

                def body(j, carry):
                    m, l, acc = carry
                    ks = pl.ds(pl.multiple_of(j * blk, blk), blk)
                    s_t = _dot(k_ref[ks, :], q_w) + bias_ref[pl.ds(j, 1), :]
                    return _softmax_block(s_t, m, l, acc, v_t_ref[rows, ks])

                m, l, acc = lax.fori_loop(0, i, body, (m, l, acc))
            o_t_ref[rows, cols] = (acc / l).astype(o_t_ref.dtype)


def _mla_kernel(q_t_ref, k_ref, v_t_ref, o_t_ref):
    seq = k_ref.shape[0]
    blk = ATTN_BLOCK
    for hh in range(HEADS_PER_STEP):
        q_rows = slice(hh * MLA_QK_PAD, (hh + 1) * MLA_QK_PAD)
        v_rows = slice(hh * MLA_V_DIM, (hh + 1) * MLA_V_DIM)
        for i in range(seq // blk):
            cols = slice(i * blk, (i + 1) * blk)
            q_w = q_t_ref[q_rows, cols]
            m, l, acc = _diag_block(k_ref[cols, q_rows], q_w, v_t_ref[v_rows, cols])
            if i > 0:
                def body(j, carry):
                    m, l, acc = carry
                    ks = pl.ds(pl.multiple_of(j * blk, blk), blk)
                    s_t = _dot(k_ref[ks, q_rows], q_w)
                    return _softmax_block(s_t, m, l, acc, v_t_ref[v_rows, ks])

                m, l, acc = lax.fori_loop(0, i, body, (m, l, acc))
            o_t_ref[v_rows, cols] = (acc / l).astype(o_t_ref.dtype)


def _attn_call(kernel, q_t, k, v_t, *, seq, qk_rows, name, scratch_shapes=()):
    t = k.shape[0]
    n_batch = t // seq
    n_steps = q_t.shape[0] // (HEADS_PER_STEP * qk_rows)
    k_lanes = k.shape[1] // n_steps
    v_rows = v_t.shape[0] // n_steps
    return pl.pallas_call(
        kernel,
        out_shape=jax.ShapeDtypeStruct(v_t.shape, BF16),
        grid=(n_batch, n_steps),
        in_specs=[
            pl.BlockSpec((HEADS_PER_STEP * qk_rows, seq), lambda b, p: (p, b)),
            pl.BlockSpec((seq, k_lanes), lambda b, p: (b, p)),
            pl.BlockSpec((v_rows, seq), lambda b, p: (p, b)),
        ],
        out_specs=pl.BlockSpec((v_rows, seq), lambda b, p: (p, b)),
        scratch_shapes=list(scratch_shapes), name=name,
        compiler_params=pltpu.CompilerParams(
            dimension_semantics=("parallel", "parallel"), vmem_limit_bytes=V7X_VMEM_LIMIT),
    )(q_t, k, v_t)


def _out_ffn_kernel(x_ref, oa_t_ref, ob_t_ref, w_o_ref, g_ffn_ref, w_gate_ref, w_up_ref,
                    w_down_ref, out_ref):
    o_t = jnp.concatenate([oa_t_ref[...], ob_t_ref[...]], axis=0).astype(F32)
    o = o_t.T.astype(BF16)
    h = x_ref[...] + _dot(o, w_o_ref[...])
    gn = h * lax.rsqrt(jnp.mean(h * h, axis=-1, keepdims=True) + EPS) * g_ffn_ref[...]
    gn = gn.astype(BF16)
    gate = _dot(gn, w_gate_ref[...])
    up = _dot(gn, w_up_ref[...])
    act = (gate * jax.nn.sigmoid(gate) * up).astype(BF16)
    out_ref[...] = h + _dot(act, w_down_ref[...])


def _out_ffn_call(x2, oa_t, ob_t, w_o, g_ffn, w_gate, w_up, w_down, *, tm):
    t, d = x2.shape
    const = lambda i: (0, 0)

    def resident(a):
        return pl.BlockSpec(a.shape, const, pipeline_mode=pl.Buffered(1))

    return pl.pallas_call(
        _out_ffn_kernel,
        out_shape=jax.ShapeDtypeStruct((t, d), x2.dtype),
        grid=(t // tm,),
        in_specs=[
            pl.BlockSpec((tm, d), lambda i: (i, 0)),
            pl.BlockSpec((oa_t.shape[0], tm), lambda i: (0, i)),
            pl.BlockSpec((ob_t.shape[0], tm), lambda i: (0, i)),
            resident(w_o), resident(g_ffn), resident(w_gate), resident(w_up), resident(w_down),
        ],
        out_specs=pl.BlockSpec((tm, d), lambda i: (i, 0)),
        name="out_ffn",
        compiler_params=pltpu.CompilerParams(
            dimension_semantics=("parallel",), vmem_limit_bytes=V7X_VMEM_LIMIT),
    )(x2, oa_t, ob_t, w_o, g_ffn, w_gate, w_up, w_down)


def _rope_tables_t(seq, dim):
    inv = ROPE_THETA ** (-jnp.arange(0, dim, 2, dtype=F32) / dim)
    ang = jnp.arange(seq, dtype=F32)[:, None] * inv[None, :]
    return jnp.cos(ang).T, jnp.sin(ang).T


def kernel(x, attn_norm_g, w_in, moba_q_norm_g, moba_k_norm_g, mla_q_a_norm_g, w_q_up,
           mla_kv_a_norm_g, w_kv_up, mla_q_norm_g, mla_k_norm_g, w_o, ffn_norm_g,
           w_gate, w_up, w_down):
    b, s, d = x.shape
    assert s % MOBA_BLOCK == 0 and s % ATTN_BLOCK == 0
    tm = 256
    cos_a, sin_a = _rope_tables_t(s, HEAD_DIM)
    cos_b, sin_b = _rope_tables_t(s, MLA_ROPE_DIM)
    h = x.reshape(b * s, d)
    for l in range(w_in.shape[0]):
        qa_t, ka, va_t, qb_t, kb, vb_t = _proj_call(
            h, attn_norm_g[l].reshape(1, d), w_in[l].T.astype(BF16),
            w_q_up[l].T.astype(BF16), w_kv_up[l].T.astype(BF16),
            _col(moba_q_norm_g[l]), _col(moba_k_norm_g[l]), _col(mla_q_a_norm_g[l]),
            _col(mla_kv_a_norm_g[l]), _col(mla_q_norm_g[l]), _col(mla_k_norm_g[l]),
            cos_a, sin_a, cos_b, sin_b, seq=s, tm=tm)
        oa_t = _attn_call(_moba_kernel, qa_t, ka, va_t, seq=s, qk_rows=HEAD_DIM, name="moba_attn",
                          scratch_shapes=[pltpu.VMEM((s // MOBA_BLOCK, MOBA_BLOCK), F32)])
        ob_t = _attn_call(_mla_kernel, qb_t, kb, vb_t, seq=s, qk_rows=MLA_QK_PAD, name="mla_attn")
        h = _out_ffn_call(h, oa_t, ob_t, w_o[l].astype(BF16), ffn_norm_g[l].reshape(1, d),
                          w_gate[l].astype(BF16), w_up[l].astype(BF16), w_down[l].astype(BF16),
                          tm=tm)
    return h.reshape(b, s, d).astype(x.dtype)
```

```python
import math

import jax
import jax.numpy as jnp
from jax import lax
from jax.experimental import pallas as pl
from jax.experimental.pallas import tpu as pltpu

HEAD_DIM = 64
MOBA_HEADS = 8
MOBA_WIDTH = MOBA_HEADS * HEAD_DIM
MOBA_BLOCK = 256
MOBA_TOPK = 3
MLA_HEADS = 8
MLA_Q_RANK = 256
MLA_KV_RANK = 128
MLA_NOPE_DIM = 64
MLA_ROPE_DIM = 32
MLA_V_DIM = 64
MLA_QK_DIM = MLA_NOPE_DIM + MLA_ROPE_DIM
MLA_QK_PAD = 128
MLA_WIDTH = MLA_HEADS * MLA_V_DIM
ROPE_THETA = 10000.0
EPS = 1e-6

NEG_BIG = -1e30
ATTN_BLOCK = 256
HEADS_PER_STEP = 2
SCORE_LOOKAHEAD = 2
ONES_ROWS = 16
LOG2_E = math.log2(math.e)
V7X_VMEM_LIMIT = 56 * 1024 * 1024

F32 = jnp.float32
BF16 = jnp.bfloat16

_NT = (((1,), (1,)), ((), ()))


def _dot(a, b):
    return jnp.dot(a, b, preferred_element_type=F32)


def _rope_t(x, cos, sin):
    half = x.shape[0] // 2
    x1, x2 = x[:half], x[half:]
    return jnp.concatenate([x1 * cos - x2 * sin, x2 * cos + x1 * sin], axis=0)


def _proj_kernel(x_ref, g_attn_ref, w_in_t_ref, w_qup_t_ref, w_kvup_t_ref,
                 gq_a_ref, gk_a_ref, g_cq_ref, g_ckv_ref, gq_b_ref, gk_b_ref,
                 cos_a_ref, sin_a_ref, cos_b_ref, sin_b_ref,
                 qa_t_ref, ka_ref, va_t_ref, qb_t_ref, kb_ref, vb_t_ref):
    tm = x_ref.shape[0]
    x = x_ref[...]
    hn = x * lax.rsqrt(jnp.mean(x * x, axis=-1, keepdims=True) + EPS) * g_attn_ref[...]
    hn = hn.astype(BF16)
    proj_t = lax.dot_general(w_in_t_ref[...], hn, _NT, preferred_element_type=F32)

    o_k = MOBA_WIDTH
    o_v = 2 * MOBA_WIDTH
    o_cq = 3 * MOBA_WIDTH
    o_ckv = o_cq + MLA_Q_RANK
    o_pe = o_ckv + MLA_KV_RANK

    cos_a, sin_a = cos_a_ref[...], sin_a_ref[...]
    cos_b, sin_b = cos_b_ref[...], sin_b_ref[...]

    gq_a, gk_a = gq_a_ref[...], gk_a_ref[...]
    moba_scale = HEAD_DIM ** -0.5 * LOG2_E
    ka_parts = []
    for h in range(MOBA_HEADS):
        q = proj_t[h * HEAD_DIM:(h + 1) * HEAD_DIM]
        q = q * lax.rsqrt(jnp.mean(q * q, axis=0, keepdims=True) + EPS) * gq_a
        q = _rope_t(q, cos_a, sin_a) * moba_scale
        qa_t_ref[h * HEAD_DIM:(h + 1) * HEAD_DIM, :] = q.astype(BF16)
        k = proj_t[o_k + h * HEAD_DIM:o_k + (h + 1) * HEAD_DIM]
        k = k * lax.rsqrt(jnp.mean(k * k, axis=0, keepdims=True) + EPS) * gk_a
        ka_parts.append(_rope_t(k, cos_a, sin_a))
    ka_ref[...] = jnp.concatenate(ka_parts, axis=0).T.astype(BF16)
    va_t_ref[...] = proj_t[o_v:o_cq].astype(BF16)

    cq = proj_t[o_cq:o_ckv]
    cq = cq * lax.rsqrt(jnp.mean(cq * cq, axis=0, keepdims=True) + EPS) * g_cq_ref[...]
    qb = _dot(w_qup_t_ref[...], cq.astype(BF16))
    ckv = proj_t[o_ckv:o_pe]
    ckv = ckv * lax.rsqrt(jnp.mean(ckv * ckv, axis=0, keepdims=True) + EPS) * g_ckv_ref[...]
    kv = _dot(w_kvup_t_ref[...], ckv.astype(BF16))
    k_pe = proj_t[o_pe:o_pe + MLA_ROPE_DIM]

    gq_b, gk_b = gq_b_ref[...], gk_b_ref[...]
    mla_scale = MLA_QK_DIM ** -0.5 * LOG2_E
    pe_ss = jnp.sum(k_pe * k_pe, axis=0, keepdims=True)
    pe_rot = _rope_t(k_pe * gk_b[MLA_NOPE_DIM:], cos_b, sin_b)
    pad = jnp.zeros((MLA_QK_PAD - MLA_QK_DIM, tm), F32)
    kv_w = MLA_NOPE_DIM + MLA_V_DIM
    kb_parts = []
    for h in range(MLA_HEADS):
        q = qb[h * MLA_QK_DIM:(h + 1) * MLA_QK_DIM]
        q = q * lax.rsqrt(jnp.mean(q * q, axis=0, keepdims=True) + EPS) * gq_b
        q = jnp.concatenate(
            [q[:MLA_NOPE_DIM], _rope_t(q[MLA_NOPE_DIM:], cos_b, sin_b)], axis=0) * mla_scale
        qb_t_ref[h * MLA_QK_PAD:(h + 1) * MLA_QK_PAD, :] = (
            jnp.concatenate([q, pad], axis=0).astype(BF16))
        k_nope = kv[h * kv_w:h * kv_w + MLA_NOPE_DIM]
        ss = jnp.sum(k_nope * k_nope, axis=0, keepdims=True) + pe_ss
        r = lax.rsqrt(ss / MLA_QK_DIM + EPS)
        kb_parts += [k_nope * r * gk_b[:MLA_NOPE_DIM], pe_rot * r, pad]
        vb_t_ref[h * MLA_V_DIM:(h + 1) * MLA_V_DIM, :] = (
            kv[h * kv_w + MLA_NOPE_DIM:(h + 1) * kv_w].astype(BF16))
    kb_ref[...] = jnp.concatenate(kb_parts, axis=0).T.astype(BF16)


def _col(v):
    return v.astype(F32).reshape(-1, 1)


def _proj_call(x2, g_attn, w_in_t, w_qup_t, w_kvup_t, gq_a, gk_a, g_cq, g_ckv, gq_b, gk_b,
               cos_a, sin_a, cos_b, sin_b, *, seq, tm):
    t, d = x2.shape
    n_pos = seq // tm
    const = lambda i: (0, 0)
    row = lambda i: (i, 0)
    colb = lambda i: (0, i)
    pos = lambda i: (0, i % n_pos)

    def full(a):
        return pl.BlockSpec(a.shape, const)

    out_shape = [
        jax.ShapeDtypeStruct((MOBA_WIDTH, t), BF16),
        jax.ShapeDtypeStruct((t, MOBA_WIDTH), BF16),
        jax.ShapeDtypeStruct((MOBA_WIDTH, t), BF16),
        jax.ShapeDtypeStruct((MLA_HEADS * MLA_QK_PAD, t), BF16),
        jax.ShapeDtypeStruct((t, MLA_HEADS * MLA_QK_PAD), BF16),
        jax.ShapeDtypeStruct((MLA_WIDTH, t), BF16),
    ]
    out_specs = [
        pl.BlockSpec((MOBA_WIDTH, tm), colb),
        pl.BlockSpec((tm, MOBA_WIDTH), row),
        pl.BlockSpec((MOBA_WIDTH, tm), colb),
        pl.BlockSpec((MLA_HEADS * MLA_QK_PAD, tm), colb),
        pl.BlockSpec((tm, MLA_HEADS * MLA_QK_PAD), row),
        pl.BlockSpec((MLA_WIDTH, tm), colb),
    ]
    in_specs = [
        pl.BlockSpec((tm, d), row), full(g_attn), full(w_in_t), full(w_qup_t), full(w_kvup_t),
        full(gq_a), full(gk_a), full(g_cq), full(g_ckv), full(gq_b), full(gk_b),
        pl.BlockSpec((HEAD_DIM // 2, tm), pos), pl.BlockSpec((HEAD_DIM // 2, tm), pos),
        pl.BlockSpec((MLA_ROPE_DIM // 2, tm), pos), pl.BlockSpec((MLA_ROPE_DIM // 2, tm), pos),
    ]
    return pl.pallas_call(
        _proj_kernel, out_shape=out_shape, grid=(t // tm,),
        in_specs=in_specs, out_specs=out_specs, name="proj_heads",
        compiler_params=pltpu.CompilerParams(
            dimension_semantics=("parallel",), vmem_limit_bytes=V7X_VMEM_LIMIT),
    )(x2, g_attn, w_in_t, w_qup_t, w_kvup_t, gq_a, gk_a, g_cq, g_ckv, gq_b, gk_b,
      cos_a, sin_a, cos_b, sin_b)


def _attend(s_past, s_diag, v_aug):
    key_i = lax.broadcasted_iota(jnp.int32, s_diag.shape, 0)
    qry_i = lax.broadcasted_iota(jnp.int32, s_diag.shape, 1)
    s_diag = jnp.where(key_i <= qry_i, s_diag, NEG_BIG)
    m = jnp.max(s_diag, axis=0, keepdims=True)
    if s_past is not None:
        m = jnp.maximum(m, jnp.max(s_past, axis=0, keepdims=True))
        p = jnp.concatenate([jnp.exp2(s_past - m).astype(BF16),
                             jnp.exp2(s_diag - m).astype(BF16)], axis=0)
    else:
        p = jnp.exp2(s_diag - m).astype(BF16)
    r = _dot(v_aug, p)
    dv = v_aug.shape[0] - ONES_ROWS
    return r[:dv] / r[dv:dv + 1]


def _split3(a):
    hi = a.astype(BF16)
    r1 = a - hi.astype(F32)
    mid = r1.astype(BF16)
    lo = (r1 - mid.astype(F32)).astype(BF16)
    return hi, mid, lo


def _moba_kernel(q_t_ref, k_ref, v_t_ref, o_t_ref):
    seq = k_ref.shape[0]
    nb = seq // MOBA_BLOCK
    blk = MOBA_BLOCK
    n_sel = min(MOBA_TOPK, nb - 1)

    k_mean = jnp.concatenate(
        [jnp.mean(k_ref[j * blk:(j + 1) * blk, :].astype(F32), axis=0, keepdims=True)
         for j in range(nb)], axis=0)
    km_hi, km_mid, km_lo = _split3(k_mean)
    row_i = lax.broadcasted_iota(jnp.int32, (nb, blk), 0)
    zeros = jnp.zeros((HEAD_DIM, blk), BF16)
    ones = jnp.ones((ONES_ROWS, seq), BF16)

    v_aug = [jnp.concatenate([v_t_ref[hh * HEAD_DIM:(hh + 1) * HEAD_DIM, :], ones], axis=0)
             for hh in range(HEADS_PER_STEP)]

    def scores(hh, i):
        q_h = q_t_ref[hh * HEAD_DIM:(hh + 1) * HEAD_DIM, i * blk:(i + 1) * blk]
        q_w = jnp.concatenate([q_h, zeros] if hh == 0 else [zeros, q_h], axis=0)
        s = _dot(k_ref[0:(i + 1) * blk, :], q_w)
        if i == 0 or n_sel == 0:
            return None, s
        gate = _dot(km_hi, q_w) + _dot(km_mid, q_w) + _dot(km_lo, q_w)
        gate = jnp.where(row_i < i, gate, -jnp.inf)
        rank = jnp.zeros((nb, blk), jnp.int32)
        for jp in range(i):
            g_jp = gate[jp:jp + 1, :]
            beats = (g_jp > gate) | ((g_jp == gate) & (jp < row_i))
            rank = rank + beats.astype(jnp.int32)
        bias = jnp.where((rank < n_sel) & (row_i < i), 0.0, NEG_BIG)
        s_past = jnp.concatenate(
            [s[j * blk:(j + 1) * blk] + bias[j:j + 1, :] for j in range(i)], axis=0)
        return s_past, s[i * blk:]

    def finish(hh, i, s_past, s_diag):
        o = _attend(s_past, s_diag, v_aug[hh][:, :(i + 1) * blk])
        o_t_ref[hh * HEAD_DIM:(hh + 1) * HEAD_DIM, i * blk:(i + 1) * blk] = o.astype(o_t_ref.dtype)

    _pipelined_units(nb, scores, finish)


def _pipelined_units(n_qblk, scores, finish):
    units = [(hh, i) for hh in range(HEADS_PER_STEP)
             for i in (range(n_qblk) if hh % 2 == 0 else reversed(range(n_qblk)))]
    pending = [scores(*unit) for unit in units[:SCORE_LOOKAHEAD]]
    for u, unit in enumerate(units):
        if u + SCORE_LOOKAHEAD < len(units):
            pending.append(scores(*units[u + SCORE_LOOKAHEAD]))
        finish(*unit, *pending.pop(0))


def _mla_kernel(q_t_ref, k_ref, v_t_ref, o_t_ref):
    seq = k_ref.shape[0]
    blk = ATTN_BLOCK
    ones = jnp.ones((ONES_ROWS, seq), BF16)
    v_aug = [jnp.concatenate([v_t_ref[hh * MLA_V_DIM:(hh + 1) * MLA_V_DIM, :], ones], axis=0)
             for hh in range(HEADS_PER_STEP)]

    def scores(hh, i):
        q_rows = slice(hh * MLA_QK_PAD, (hh + 1) * MLA_QK_PAD)
        s = _dot(k_ref[0:(i + 1) * blk, q_rows], q_t_ref[q_rows, i * blk:(i + 1) * blk])
        return (s[:i * blk] if i > 0 else None), s[i * blk:]

    def finish(hh, i, s_past, s_diag):
        o = _attend(s_past, s_diag, v_aug[hh][:, :(i + 1) * blk])
        o_t_ref[hh * MLA_V_DIM:(hh + 1) * MLA_V_DIM, i * blk:(i + 1) * blk] = o.astype(o_t_ref.dtype)

    _pipelined_units(seq // blk, scores, finish)


def _attn_call(kernel, q_t, k, v_t, *, seq, qk_rows, name):
    t = k.shape[0]
    n_batch = t // seq
    n_steps = q_t.shape[0] // (HEADS_PER_STEP * qk_rows)
    k_lanes = k.shape[1] // n_steps
    v_rows = v_t.shape[0] // n_steps
    return pl.pallas_call(
        kernel,
        out_shape=jax.ShapeDtypeStruct(v_t.shape, BF16),
        grid=(n_batch, n_steps),
        in_specs=[
            pl.BlockSpec((HEADS_PER_STEP * qk_rows, seq), lambda b, p: (p, b)),
            pl.BlockSpec((seq, k_lanes), lambda b, p: (b, p)),
            pl.BlockSpec((v_rows, seq), lambda b, p: (p, b)),
        ],
        out_specs=pl.BlockSpec((v_rows, seq), lambda b, p: (p, b)),
        name=name,
        compiler_params=pltpu.CompilerParams(
            dimension_semantics=("parallel", "parallel"), vmem_limit_bytes=V7X_VMEM_LIMIT),
    )(q_t, k, v_t)


def _out_ffn_kernel(x_ref, oa_t_ref, ob_t_ref, w_o_ref, g_ffn_ref, w_gate_ref, w_up_ref,
                    w_down_ref, out_ref):
    o_t = jnp.concatenate([oa_t_ref[...], ob_t_ref[...]], axis=0).astype(F32)
    o = o_t.T.astype(BF16)
    h = x_ref[...] + _dot(o, w_o_ref[...])
    gn = h * lax.rsqrt(jnp.mean(h * h, axis=-1, keepdims=True) + EPS) * g_ffn_ref[...]
    gn = gn.astype(BF16)
    gate = _dot(gn, w_gate_ref[...])
    up = _dot(gn, w_up_ref[...])
    act = (gate * jax.nn.sigmoid(gate) * up).astype(BF16)
    out_ref[...] = h + _dot(act, w_down_ref[...])


def _out_ffn_call(x2, oa_t, ob_t, w_o, g_ffn, w_gate, w_up, w_down, *, tm):
    t, d = x2.shape
    const = lambda i: (0, 0)

    def resident(a):
        return pl.BlockSpec(a.shape, const, pipeline_mode=pl.Buffered(1))

    return pl.pallas_call(
        _out_ffn_kernel,
        out_shape=jax.ShapeDtypeStruct((t, d), x2.dtype),
        grid=(t // tm,),
        in_specs=[
            pl.BlockSpec((tm, d), lambda i: (i, 0)),
            pl.BlockSpec((oa_t.shape[0], tm), lambda i: (0, i)),
            pl.BlockSpec((ob_t.shape[0], tm), lambda i: (0, i)),
            resident(w_o), resident(g_ffn), resident(w_gate), resident(w_up), resident(w_down),
        ],
        out_specs=pl.BlockSpec((tm, d), lambda i: (i, 0)),
        name="out_ffn",
        compiler_params=pltpu.CompilerParams(
            dimension_semantics=("parallel",), vmem_limit_bytes=V7X_VMEM_LIMIT),
    )(x2, oa_t, ob_t, w_o, g_ffn, w_gate, w_up, w_down)


def _rope_tables_t(seq, dim):
    inv = ROPE_THETA ** (-jnp.arange(0, dim, 2, dtype=F32) / dim)
    ang = jnp.arange(seq, dtype=F32)[:, None] * inv[None, :]
    return jnp.cos(ang).T, jnp.sin(ang).T


def kernel(x, attn_norm_g, w_in, moba_q_norm_g, moba_k_norm_g, mla_q_a_norm_g, w_q_up,
           mla_kv_a_norm_g, w_kv_up, mla_q_norm_g, mla_k_norm_g, w_o, ffn_norm_g,
           w_gate, w_up, w_down):
    b, s, d = x.shape
    assert s % MOBA_BLOCK == 0 and s % ATTN_BLOCK == 0
    tm = 256
    cos_a, sin_a = _rope_tables_t(s, HEAD_DIM)
    cos_b, sin_b = _rope_tables_t(s, MLA_ROPE_DIM)
    h = x.reshape(b * s, d)
    for l in range(w_in.shape[0]):
        qa_t, ka, va_t, qb_t, kb, vb_t = _proj_call(
            h, attn_norm_g[l].reshape(1, d), w_in[l].T.astype(BF16),
            w_q_up[l].T.astype(BF16), w_kv_up[l].T.astype(BF16),
            _col(moba_q_norm_g[l]), _col(moba_k_norm_g[l]), _col(mla_q_a_norm_g[l]),
            _col(mla_kv_a_norm_g[l]), _col(mla_q_norm_g[l]), _col(mla_k_norm_g[l]),
            cos_a, sin_a, cos_b, sin_b, seq=s, tm=tm)
        oa_t = _attn_call(_moba_kernel, qa_t, ka, va_t, seq=s, qk_rows=HEAD_DIM, name="moba_attn")
        ob_t = _attn_call(_mla_kernel, qb_t, kb, vb_t, seq=s, qk_rows=MLA_QK_PAD, name="mla_attn")
        h = _out_ffn_call(h, oa_t, ob_t, w_o[l].astype(BF16), ffn_norm_g[l].reshape(1, d),
                          w_gate[l].astype(BF16), w_up[l].astype(BF16), w_down[l].astype(BF16),
                          tm=tm)
    return h.reshape(b, s, d).astype(x.dtype)
```

```python
import functools
import math

import jax
import jax.numpy as jnp
from jax import lax
from jax.experimental import pallas as pl
from jax.experimental.pallas import tpu as pltpu

HEAD_DIM = 64
MOBA_HEADS = 8
MOBA_WIDTH = MOBA_HEADS * HEAD_DIM
MOBA_BLOCK = 256
MOBA_TOPK = 3
MLA_HEADS = 8
MLA_Q_RANK = 256
MLA_KV_RANK = 128
MLA_NOPE_DIM = 64
MLA_ROPE_DIM = 32
MLA_V_DIM = 64
MLA_QK_DIM = MLA_NOPE_DIM + MLA_ROPE_DIM
MLA_QK_PAD = 128
MLA_WIDTH = MLA_HEADS * MLA_V_DIM
ROPE_THETA = 10000.0
EPS = 1e-6

NEG_BIG = -1e30
ATTN_BLOCK = 256
HEADS_PER_STEP = 2
SCORE_LOOKAHEAD = 3
ONES_ROWS = 16
LOG2_E = math.log2(math.e)
MAX_SAFE_EXPONENT = 48.0
BF16_NORM_MARGIN = 1.02
V7X_VMEM_LIMIT = 56 * 1024 * 1024

F32 = jnp.float32
BF16 = jnp.bfloat16

_NT = (((1,), (1,)), ((), ()))


def _dot(a, b):
    return jnp.dot(a, b, preferred_element_type=F32)


def _rope_t(x, cos, sin):
    half = x.shape[0] // 2
    x1, x2 = x[:half], x[half:]
    return jnp.concatenate([x1 * cos - x2 * sin, x2 * cos + x1 * sin], axis=0)


def _proj_kernel(x_ref, g_attn_ref, w_in_t_ref, w_qup_t_ref, w_kvup_t_ref,
                 gq_a_ref, gk_a_ref, g_cq_ref, g_ckv_ref, gq_b_ref, gk_b_ref,
                 cos_a_ref, sin_a_ref, cos_b_ref, sin_b_ref,
                 qa_t_ref, ka_ref, va_t_ref, qb_t_ref, kb_ref, vb_t_ref):
    tm = x_ref.shape[0]
    x = x_ref[...]
    hn = x * lax.rsqrt(jnp.mean(x * x, axis=-1, keepdims=True) + EPS) * g_attn_ref[...]
    hn = hn.astype(BF16)
    proj_t = lax.dot_general(w_in_t_ref[...], hn, _NT, preferred_element_type=F32)

    o_k = MOBA_WIDTH
    o_v = 2 * MOBA_WIDTH
    o_cq = 3 * MOBA_WIDTH
    o_ckv = o_cq + MLA_Q_RANK
    o_pe = o_ckv + MLA_KV_RANK

    cos_a, sin_a = cos_a_ref[...], sin_a_ref[...]
    cos_b, sin_b = cos_b_ref[...], sin_b_ref[...]

    gq_a, gk_a = gq_a_ref[...], gk_a_ref[...]
    moba_scale = HEAD_DIM ** -0.5 * LOG2_E
    ka_parts = []
    for h in range(MOBA_HEADS):
        q = proj_t[h * HEAD_DIM:(h + 1) * HEAD_DIM]
        q = q * lax.rsqrt(jnp.mean(q * q, axis=0, keepdims=True) + EPS) * gq_a
        q = _rope_t(q, cos_a, sin_a) * moba_scale
        qa_t_ref[h * HEAD_DIM:(h + 1) * HEAD_DIM, :] = q.astype(BF16)
        k = proj_t[o_k + h * HEAD_DIM:o_k + (h + 1) * HEAD_DIM]
        k = k * lax.rsqrt(jnp.mean(k * k, axis=0, keepdims=True) + EPS) * gk_a
        ka_parts.append(_rope_t(k, cos_a, sin_a))
    ka_ref[...] = jnp.concatenate(ka_parts, axis=0).T.astype(BF16)
    va_t_ref[...] = proj_t[o_v:o_cq].astype(BF16)

    cq = proj_t[o_cq:o_ckv]
    cq = cq * lax.rsqrt(jnp.mean(cq * cq, axis=0, keepdims=True) + EPS) * g_cq_ref[...]
    qb = _dot(w_qup_t_ref[...], cq.astype(BF16))
    ckv = proj_t[o_ckv:o_pe]
    ckv = ckv * lax.rsqrt(jnp.mean(ckv * ckv, axis=0, keepdims=True) + EPS) * g_ckv_ref[...]
    kv = _dot(w_kvup_t_ref[...], ckv.astype(BF16))
    k_pe = proj_t[o_pe:o_pe + MLA_ROPE_DIM]

    gq_b, gk_b = gq_b_ref[...], gk_b_ref[...]
    mla_scale = MLA_QK_DIM ** -0.5 * LOG2_E
    pe_ss = jnp.sum(k_pe * k_pe, axis=0, keepdims=True)
    pe_rot = _rope_t(k_pe * gk_b[MLA_NOPE_DIM:], cos_b, sin_b)
    pad = jnp.zeros((MLA_QK_PAD - MLA_QK_DIM, tm), F32)
    kv_w = MLA_NOPE_DIM + MLA_V_DIM
    kb_parts = []
    for h in range(MLA_HEADS):
        q = qb[h * MLA_QK_DIM:(h + 1) * MLA_QK_DIM]
        q = q * lax.rsqrt(jnp.mean(q * q, axis=0, keepdims=True) + EPS) * gq_b
        q = jnp.concatenate(
            [q[:MLA_NOPE_DIM], _rope_t(q[MLA_NOPE_DIM:], cos_b, sin_b)], axis=0) * mla_scale
        qb_t_ref[h * MLA_QK_PAD:(h + 1) * MLA_QK_PAD, :] = (
            jnp.concatenate([q, pad], axis=0).astype(BF16))
        k_nope = kv[h * kv_w:h * kv_w + MLA_NOPE_DIM]
        ss = jnp.sum(k_nope * k_nope, axis=0, keepdims=True) + pe_ss
        r = lax.rsqrt(ss / MLA_QK_DIM + EPS)
        kb_parts += [k_nope * r * gk_b[:MLA_NOPE_DIM], pe_rot * r, pad]
        vb_t_ref[h * MLA_V_DIM:(h + 1) * MLA_V_DIM, :] = (
            kv[h * kv_w + MLA_NOPE_DIM:(h + 1) * kv_w].astype(BF16))
    kb_ref[...] = jnp.concatenate(kb_parts, axis=0).T.astype(BF16)


def _col(v):
    return v.astype(F32).reshape(-1, 1)


def _proj_call(x2, g_attn, w_in_t, w_qup_t, w_kvup_t, gq_a, gk_a, g_cq, g_ckv, gq_b, gk_b,
               cos_a, sin_a, cos_b, sin_b, *, seq, tm):
    t, d = x2.shape
    n_pos = seq // tm
    const = lambda i: (0, 0)
    row = lambda i: (i, 0)
    colb = lambda i: (0, i)
    pos = lambda i: (0, i % n_pos)

    def full(a):
        return pl.BlockSpec(a.shape, const)

    out_shape = [
        jax.ShapeDtypeStruct((MOBA_WIDTH, t), BF16),
        jax.ShapeDtypeStruct((t, MOBA_WIDTH), BF16),
        jax.ShapeDtypeStruct((MOBA_WIDTH, t), BF16),
        jax.ShapeDtypeStruct((MLA_HEADS * MLA_QK_PAD, t), BF16),
        jax.ShapeDtypeStruct((t, MLA_HEADS * MLA_QK_PAD), BF16),
        jax.ShapeDtypeStruct((MLA_WIDTH, t), BF16),
    ]
    out_specs = [
        pl.BlockSpec((MOBA_WIDTH, tm), colb),
        pl.BlockSpec((tm, MOBA_WIDTH), row),
        pl.BlockSpec((MOBA_WIDTH, tm), colb),
        pl.BlockSpec((MLA_HEADS * MLA_QK_PAD, tm), colb),
        pl.BlockSpec((tm, MLA_HEADS * MLA_QK_PAD), row),
        pl.BlockSpec((MLA_WIDTH, tm), colb),
    ]
    in_specs = [
        pl.BlockSpec((tm, d), row), full(g_attn), full(w_in_t), full(w_qup_t), full(w_kvup_t),
        full(gq_a), full(gk_a), full(g_cq), full(g_ckv), full(gq_b), full(gk_b),
        pl.BlockSpec((HEAD_DIM // 2, tm), pos), pl.BlockSpec((HEAD_DIM // 2, tm), pos),
        pl.BlockSpec((MLA_ROPE_DIM // 2, tm), pos), pl.BlockSpec((MLA_ROPE_DIM // 2, tm), pos),
    ]
    return pl.pallas_call(
        _proj_kernel, out_shape=out_shape, grid=(t // tm,),
        in_specs=in_specs, out_specs=out_specs, name="proj_heads",
        compiler_params=pltpu.CompilerParams(
            dimension_semantics=("parallel",), vmem_limit_bytes=V7X_VMEM_LIMIT),
    )(x2, g_attn, w_in_t, w_qup_t, w_kvup_t, gq_a, gk_a, g_cq, g_ckv, gq_b, gk_b,
      cos_a, sin_a, cos_b, sin_b)


def _attend(s_past, s_diag, v_aug, bounded):
    key_i = lax.broadcasted_iota(jnp.int32, s_diag.shape, 0)
    qry_i = lax.broadcasted_iota(jnp.int32, s_diag.shape, 1)
    s_diag = jnp.where(key_i <= qry_i, s_diag, NEG_BIG)
    if bounded:
        prob = lambda s: jnp.exp2(s).astype(BF16)
    else:
        m = jnp.max(s_diag, axis=0, keepdims=True)
        if s_past is not None:
            m = jnp.maximum(m, jnp.max(s_past, axis=0, keepdims=True))
        prob = lambda s: jnp.exp2(s - m).astype(BF16)
    p = prob(s_diag)
    if s_past is not None:
        p = jnp.concatenate([prob(s_past), p], axis=0)
    r = _dot(v_aug, p)
    dv = v_aug.shape[0] - ONES_ROWS
    return r[:dv] / r[dv:dv + 1]


def _split3(a):
    hi = a.astype(BF16)
    r1 = a - hi.astype(F32)
    mid = r1.astype(BF16)
    lo = (r1 - mid.astype(F32)).astype(BF16)
    return hi, mid, lo


def _moba_kernel(q_t_ref, k_ref, v_t_ref, o_t_ref, k_ext_ref, *, bounded):
    seq = k_ref.shape[0]
    nb = seq // MOBA_BLOCK
    blk = MOBA_BLOCK
    n_sel = min(MOBA_TOPK, nb - 1)

    k_lanes = k_ref.shape[1]
    assert nb <= k_lanes
    k_ext_ref[:, 0:k_lanes] = k_ref[...]
    blk_of_key = lax.broadcasted_iota(jnp.int32, (seq, k_lanes), 0) // blk
    lane_i = lax.broadcasted_iota(jnp.int32, (seq, k_lanes), 1)
    k_ext_ref[:, k_lanes:] = jnp.where(blk_of_key == lane_i, 1.0, 0.0).astype(BF16)

    k_mean = jnp.concatenate(
        [jnp.mean(k_ref[j * blk:(j + 1) * blk, :].astype(F32), axis=0, keepdims=True)
         for j in range(nb)], axis=0)
    km_hi, km_mid, km_lo = _split3(k_mean)
    row_i = lax.broadcasted_iota(jnp.int32, (nb, seq), 0)
    blk_of_qry = lax.broadcasted_iota(jnp.int32, (nb, seq), 1) // blk
    zeros = jnp.zeros((HEAD_DIM, seq), BF16)
    bias_pad = jnp.zeros((k_lanes - nb, seq), F32)
    ones = jnp.ones((ONES_ROWS, seq), BF16)

    v_aug = [jnp.concatenate([v_t_ref[hh * HEAD_DIM:(hh + 1) * HEAD_DIM, :], ones], axis=0)
             for hh in range(HEADS_PER_STEP)]

    q_w = []
    for hh in range(HEADS_PER_STEP):
        q_h = q_t_ref[hh * HEAD_DIM:(hh + 1) * HEAD_DIM, :]
        q_pad = jnp.concatenate([q_h, zeros] if hh == 0 else [zeros, q_h], axis=0)
        gate = _dot(km_hi, q_pad) + _dot(km_mid, q_pad) + _dot(km_lo, q_pad)
        past = row_i < blk_of_qry
        gate = jnp.where(past, gate, -jnp.inf)
        rank = jnp.zeros((nb, seq), jnp.int32)
        for jp in range(nb - 1):
            g_jp = gate[jp:jp + 1, :]
            beats = (g_jp > gate) | ((g_jp == gate) & (jp < row_i))
            rank = rank + beats.astype(jnp.int32)
        bias = jnp.where(past & (rank >= n_sel), NEG_BIG, 0.0)
        q_w.append(jnp.concatenate(
            [q_pad, jnp.concatenate([bias, bias_pad], axis=0).astype(BF16)], axis=0))

    def scores(hh, i):
        s = _dot(k_ext_ref[0:(i + 1) * blk, :], q_w[hh][:, i * blk:(i + 1) * blk])
        return (s[:i * blk] if i > 0 else None), s[i * blk:]

    def finish(hh, i, s_past, s_diag):
        o = _attend(s_past, s_diag, v_aug[hh][:, :(i + 1) * blk], bounded)
        o_t_ref[hh * HEAD_DIM:(hh + 1) * HEAD_DIM, i * blk:(i + 1) * blk] = o.astype(o_t_ref.dtype)

    _pipelined_units(nb, scores, finish)


def _pipelined_units(n_qblk, scores, finish):
    units = [(hh, i) for hh in range(HEADS_PER_STEP)
             for i in (range(n_qblk) if hh % 2 == 0 else reversed(range(n_qblk)))]
    pending = [scores(*unit) for unit in units[:SCORE_LOOKAHEAD]]
    for u, unit in enumerate(units):
        if u + SCORE_LOOKAHEAD < len(units):
            pending.append(scores(*units[u + SCORE_LOOKAHEAD]))
        finish(*unit, *pending.pop(0))


def _mla_kernel(q_t_ref, k_ref, v_t_ref, o_t_ref, *, bounded):
    seq = k_ref.shape[0]
    blk = ATTN_BLOCK
    ones = jnp.ones((ONES_ROWS, seq), BF16)
    v_aug = [jnp.concatenate([v_t_ref[hh * MLA_V_DIM:(hh + 1) * MLA_V_DIM, :], ones], axis=0)
             for hh in range(HEADS_PER_STEP)]

    def scores(hh, i):
        q_rows = slice(hh * MLA_QK_PAD, (hh + 1) * MLA_QK_PAD)
        s = _dot(k_ref[0:(i + 1) * blk, q_rows], q_t_ref[q_rows, i * blk:(i + 1) * blk])
        return (s[:i * blk] if i > 0 else None), s[i * blk:]

    def finish(hh, i, s_past, s_diag):
        o = _attend(s_past, s_diag, v_aug[hh][:, :(i + 1) * blk], bounded)
        o_t_ref[hh * MLA_V_DIM:(hh + 1) * MLA_V_DIM, i * blk:(i + 1) * blk] = o.astype(o_t_ref.dtype)

    _pipelined_units(seq // blk, scores, finish)


def _attn_call(kernel, q_t, k, v_t, score_bound, *, seq, qk_rows, name, scratch_shapes=()):
    t = k.shape[0]
    n_batch = t // seq
    n_steps = q_t.shape[0] // (HEADS_PER_STEP * qk_rows)
    k_lanes = k.shape[1] // n_steps
    v_rows = v_t.shape[0] // n_steps

    def call(bounded, *operands):
        return pl.pallas_call(
            functools.partial(kernel, bounded=bounded),
            out_shape=jax.ShapeDtypeStruct(v_t.shape, BF16),
            grid=(n_batch, n_steps),
            in_specs=[
                pl.BlockSpec((HEADS_PER_STEP * qk_rows, seq), lambda b, p: (p, b)),
                pl.BlockSpec((seq, k_lanes), lambda b, p: (b, p)),
                pl.BlockSpec((v_rows, seq), lambda b, p: (p, b)),
            ],
            out_specs=pl.BlockSpec((v_rows, seq), lambda b, p: (p, b)),
            scratch_shapes=list(scratch_shapes),
            name=name + ("_bounded" if bounded else "_shifted"),
            compiler_params=pltpu.CompilerParams(
                dimension_semantics=("parallel", "parallel"), vmem_limit_bytes=V7X_VMEM_LIMIT),
        )(*operands)

    return lax.cond(score_bound <= MAX_SAFE_EXPONENT,
                    functools.partial(call, True), functools.partial(call, False),
                    q_t, k, v_t)


def _score_bound(g_q, g_k, dim):
    return (jnp.max(jnp.abs(g_q.astype(F32))) * jnp.max(jnp.abs(g_k.astype(F32)))
            * (dim ** 0.5 * LOG2_E * BF16_NORM_MARGIN))


def _out_ffn_kernel(x_ref, oa_t_ref, ob_t_ref, w_o_ref, g_ffn_ref, w_gate_ref, w_up_ref,
                    w_down_ref, out_ref):
    o_t = jnp.concatenate([oa_t_ref[...], ob_t_ref[...]], axis=0).astype(F32)
    o = o_t.T.astype(BF16)
    h = x_ref[...] + _dot(o, w_o_ref[...])
    gn = h * lax.rsqrt(jnp.mean(h * h, axis=-1, keepdims=True) + EPS) * g_ffn_ref[...]
    gn = gn.astype(BF16)
    gate = _dot(gn, w_gate_ref[...])
    up = _dot(gn, w_up_ref[...])
    act = (gate * jax.nn.sigmoid(gate) * up).astype(BF16)
    out_ref[...] = h + _dot(act, w_down_ref[...])


def _out_ffn_call(x2, oa_t, ob_t, w_o, g_ffn, w_gate, w_up, w_down, *, tm):
    t, d = x2.shape
    const = lambda i: (0, 0)

    def resident(a):
        return pl.BlockSpec(a.shape, const, pipeline_mode=pl.Buffered(1))

    return pl.pallas_call(
        _out_ffn_kernel,
        out_shape=jax.ShapeDtypeStruct((t, d), x2.dtype),
        grid=(t // tm,),
        in_specs=[
            pl.BlockSpec((tm, d), lambda i: (i, 0)),
            pl.BlockSpec((oa_t.shape[0], tm), lambda i: (0, i)),
            pl.BlockSpec((ob_t.shape[0], tm), lambda i: (0, i)),
            resident(w_o), resident(g_ffn), resident(w_gate), resident(w_up), resident(w_down),
        ],
        out_specs=pl.BlockSpec((tm, d), lambda i: (i, 0)),
        name="out_ffn",
        compiler_params=pltpu.CompilerParams(
            dimension_semantics=("parallel",), vmem_limit_bytes=V7X_VMEM_LIMIT),
    )(x2, oa_t, ob_t, w_o, g_ffn, w_gate, w_up, w_down)


def _rope_tables_t(seq, dim):
    inv = ROPE_THETA ** (-jnp.arange(0, dim, 2, dtype=F32) / dim)
    ang = jnp.arange(seq, dtype=F32)[:, None] * inv[None, :]
    return jnp.cos(ang).T, jnp.sin(ang).T


def kernel(x, attn_norm_g, w_in, moba_q_norm_g, moba_k_norm_g, mla_q_a_norm_g, w_q_up,
           mla_kv_a_norm_g, w_kv_up, mla_q_norm_g, mla_k_norm_g, w_o, ffn_norm_g,
           w_gate, w_up, w_down):
    b, s, d = x.shape
    assert s % MOBA_BLOCK == 0 and s % ATTN_BLOCK == 0
    tm = 256
    cos_a, sin_a = _rope_tables_t(s, HEAD_DIM)
    cos_b, sin_b = _rope_tables_t(s, MLA_ROPE_DIM)
    h = x.reshape(b * s, d)
    for l in range(w_in.shape[0]):
        qa_t, ka, va_t, qb_t, kb, vb_t = _proj_call(
            h, attn_norm_g[l].reshape(1, d), w_in[l].T.astype(BF16),
            w_q_up[l].T.astype(BF16), w_kv_up[l].T.astype(BF16),
            _col(moba_q_norm_g[l]), _col(moba_k_norm_g[l]), _col(mla_q_a_norm_g[l]),
            _col(mla_kv_a_norm_g[l]), _col(mla_q_norm_g[l]), _col(mla_k_norm_g[l]),
            cos_a, sin_a, cos_b, sin_b, seq=s, tm=tm)
        oa_t = _attn_call(_moba_kernel, qa_t, ka, va_t,
                          _score_bound(moba_q_norm_g[l], moba_k_norm_g[l], HEAD_DIM),
                          seq=s, qk_rows=HEAD_DIM, name="moba_attn",
                          scratch_shapes=[pltpu.VMEM((s, 2 * HEADS_PER_STEP * HEAD_DIM), BF16)])
        ob_t = _attn_call(_mla_kernel, qb_t, kb, vb_t,
                          _score_bound(mla_q_norm_g[l], mla_k_norm_g[l], MLA_QK_DIM),
                          seq=s, qk_rows=MLA_QK_PAD, name="mla_attn")
        h = _out_ffn_call(h, oa_t, ob_t, w_o[l].astype(BF16), ffn_norm_g[l].reshape(1, d),
                          w_gate[l].astype(BF16), w_up[l].astype(BF16), w_down[l].astype(BF16),
                          tm=tm)
    return h.reshape(b, s, d).astype(x.dtype)
```

```python
import functools
import math

import jax
import jax.numpy as jnp
from jax import lax
from jax.experimental import pallas as pl
from jax.experimental.pallas import tpu as pltpu

HEAD_DIM = 64
MOBA_HEADS = 8
MOBA_WIDTH = MOBA_HEADS * HEAD_DIM
MOBA_BLOCK = 256
MOBA_TOPK = 3
MLA_HEADS = 8
MLA_Q_RANK = 256
MLA_KV_RANK = 128
MLA_NOPE_DIM = 64
MLA_ROPE_DIM = 32
MLA_V_DIM = 64
MLA_QK_DIM = MLA_NOPE_DIM + MLA_ROPE_DIM
MLA_QK_PAD = 128
MLA_WIDTH = MLA_HEADS * MLA_V_DIM
ROPE_THETA = 10000.0
EPS = 1e-6

NEG_BIG = -1e30
ATTN_BLOCK = 256
PROJ_TILE = 256
FFN_TILE = 512
HEADS_PER_STEP = 2
SCORE_LOOKAHEAD = 3
ONES_ROWS = 16
LOG2_E = math.log2(math.e)
MAX_SAFE_EXPONENT = 48.0
BF16_NORM_MARGIN = 1.02
V7X_VMEM_LIMIT = 56 * 1024 * 1024

F32 = jnp.float32
BF16 = jnp.bfloat16

_NT = (((1,), (1,)), ((), ()))


def _dot(a, b):
    return jnp.dot(a, b, preferred_element_type=F32)


def _rope_t(x, cos, sin):
    half = x.shape[0] // 2
    x1, x2 = x[:half], x[half:]
    return jnp.concatenate([x1 * cos - x2 * sin, x2 * cos + x1 * sin], axis=0)


def _proj_kernel(x_ref, g_attn_ref, w_in_t_ref, w_qup_t_ref, w_kvup_t_ref,
                 gq_a_ref, gk_a_ref, g_cq_ref, g_ckv_ref, gq_b_ref, gk_b_ref,
                 cos_a_ref, sin_a_ref, cos_b_ref, sin_b_ref,
                 qa_t_ref, ka_ref, va_t_ref, qb_t_ref, kb_ref, vb_t_ref,
                 proj_even_ref, proj_odd_ref):
    t = pl.program_id(0)

    @pl.when(t == 0)
    def _():
        proj_odd_ref[...] = jnp.zeros_like(proj_odd_ref)

    def step(proj_new_ref, proj_done_ref):
        x = x_ref[...]
        hn = x * lax.rsqrt(jnp.mean(x * x, axis=-1, keepdims=True) + EPS) * g_attn_ref[...]
        proj_new_ref[...] = lax.dot_general(w_in_t_ref[...], hn.astype(BF16), _NT,
                                            preferred_element_type=F32)
        _finish_heads(proj_done_ref[...], w_qup_t_ref, w_kvup_t_ref,
                      gq_a_ref, gk_a_ref, g_cq_ref, g_ckv_ref, gq_b_ref, gk_b_ref,
                      cos_a_ref, sin_a_ref, cos_b_ref, sin_b_ref,
                      qa_t_ref, ka_ref, va_t_ref, qb_t_ref, kb_ref, vb_t_ref)

    @pl.when(t % 2 == 0)
    def _():
        step(proj_even_ref, proj_odd_ref)

    @pl.when(t % 2 == 1)
    def _():
        step(proj_odd_ref, proj_even_ref)


def _finish_heads(proj_t, w_qup_t_ref, w_kvup_t_ref,
                  gq_a_ref, gk_a_ref, g_cq_ref, g_ckv_ref, gq_b_ref, gk_b_ref,
                  cos_a_ref, sin_a_ref, cos_b_ref, sin_b_ref,
                  qa_t_ref, ka_ref, va_t_ref, qb_t_ref, kb_ref, vb_t_ref):
    tm = proj_t.shape[1]
    o_k = MOBA_WIDTH
    o_v = 2 * MOBA_WIDTH
    o_cq = 3 * MOBA_WIDTH
    o_ckv = o_cq + MLA_Q_RANK
    o_pe = o_ckv + MLA_KV_RANK

    cos_a, sin_a = cos_a_ref[...], sin_a_ref[...]
    cos_b, sin_b = cos_b_ref[...], sin_b_ref[...]

    gq_a, gk_a = gq_a_ref[...], gk_a_ref[...]
    ka_parts = []
    for h in range(MOBA_HEADS):
        q = proj_t[h * HEAD_DIM:(h + 1) * HEAD_DIM]
        q = q * lax.rsqrt(jnp.mean(q * q, axis=0, keepdims=True) + EPS) * gq_a
        qa_t_ref[h * HEAD_DIM:(h + 1) * HEAD_DIM, :] = _rope_t(q, cos_a, sin_a).astype(BF16)
        k = proj_t[o_k + h * HEAD_DIM:o_k + (h + 1) * HEAD_DIM]
        k = k * lax.rsqrt(jnp.mean(k * k, axis=0, keepdims=True) + EPS) * gk_a
        ka_parts.append(_rope_t(k, cos_a, sin_a))
    ka_ref[...] = jnp.concatenate(ka_parts, axis=0).T.astype(BF16)
    va_t_ref[...] = proj_t[o_v:o_cq].astype(BF16)

    cq = proj_t[o_cq:o_ckv]
    cq = cq * lax.rsqrt(jnp.mean(cq * cq, axis=0, keepdims=True) + EPS) * g_cq_ref[...]
    qb = _dot(w_qup_t_ref[...], cq.astype(BF16))
    ckv = proj_t[o_ckv:o_pe]
    ckv = ckv * lax.rsqrt(jnp.mean(ckv * ckv, axis=0, keepdims=True) + EPS) * g_ckv_ref[...]
    kv = _dot(w_kvup_t_ref[...], ckv.astype(BF16))
    k_pe = proj_t[o_pe:o_pe + MLA_ROPE_DIM]

    gq_b, gk_b = gq_b_ref[...], gk_b_ref[...]
    pe_ss = jnp.sum(k_pe * k_pe, axis=0, keepdims=True)
    pe_rot = _rope_t(k_pe * gk_b[MLA_NOPE_DIM:], cos_b, sin_b)
    pad = jnp.zeros((MLA_QK_PAD - MLA_QK_DIM, tm), F32)
    kv_w = MLA_NOPE_DIM + MLA_V_DIM
    kb_parts = []
    for h in range(MLA_HEADS):
        q = qb[h * MLA_QK_DIM:(h + 1) * MLA_QK_DIM]
        q = q * lax.rsqrt(jnp.mean(q * q, axis=0, keepdims=True) + EPS) * gq_b
        q = jnp.concatenate(
            [q[:MLA_NOPE_DIM], _rope_t(q[MLA_NOPE_DIM:], cos_b, sin_b)], axis=0)
        qb_t_ref[h * MLA_QK_PAD:(h + 1) * MLA_QK_PAD, :] = (
            jnp.concatenate([q, pad], axis=0).astype(BF16))
        k_nope = kv[h * kv_w:h * kv_w + MLA_NOPE_DIM]
        ss = jnp.sum(k_nope * k_nope, axis=0, keepdims=True) + pe_ss
        r = lax.rsqrt(ss / MLA_QK_DIM + EPS)
        kb_parts += [k_nope * r * gk_b[:MLA_NOPE_DIM], pe_rot * r, pad]
        vb_t_ref[h * MLA_V_DIM:(h + 1) * MLA_V_DIM, :] = (
            kv[h * kv_w + MLA_NOPE_DIM:(h + 1) * kv_w].astype(BF16))
    kb_ref[...] = jnp.concatenate(kb_parts, axis=0).T.astype(BF16)


def _col(v):
    return v.astype(F32).reshape(-1, 1)


def _proj_call(x2, g_attn, w_in_t, w_qup_t, w_kvup_t, gq_a, gk_a, g_cq, g_ckv, gq_b, gk_b,
               cos_a, sin_a, cos_b, sin_b, *, seq, tm):
    t, d = x2.shape
    n_pos = seq // tm
    n_tiles = t // tm
    const = lambda i: (0, 0)
    x_row = lambda i: (jnp.minimum(i, n_tiles - 1), 0)
    row = lambda i: (jnp.maximum(i - 1, 0), 0)
    colb = lambda i: (0, jnp.maximum(i - 1, 0))
    pos = lambda i: (0, jnp.maximum(i - 1, 0) % n_pos)

    def full(a):
        return pl.BlockSpec(a.shape, const)

    out_shape = [
        jax.ShapeDtypeStruct((MOBA_WIDTH, t), BF16),
        jax.ShapeDtypeStruct((t, MOBA_WIDTH), BF16),
        jax.ShapeDtypeStruct((MOBA_WIDTH, t), BF16),
        jax.ShapeDtypeStruct((MLA_HEADS * MLA_QK_PAD, t), BF16),
        jax.ShapeDtypeStruct((t, MLA_HEADS * MLA_QK_PAD), BF16),
        jax.ShapeDtypeStruct((MLA_WIDTH, t), BF16),
    ]
    out_specs = [
        pl.BlockSpec((MOBA_WIDTH, tm), colb),
        pl.BlockSpec((tm, MOBA_WIDTH), row),
        pl.BlockSpec((MOBA_WIDTH, tm), colb),
        pl.BlockSpec((MLA_HEADS * MLA_QK_PAD, tm), colb),
        pl.BlockSpec((tm, MLA_HEADS * MLA_QK_PAD), row),
        pl.BlockSpec((MLA_WIDTH, tm), colb),
    ]
    in_specs = [
        pl.BlockSpec((tm, d), x_row), full(g_attn), full(w_in_t), full(w_qup_t), full(w_kvup_t),
        full(gq_a), full(gk_a), full(g_cq), full(g_ckv), full(gq_b), full(gk_b),
        pl.BlockSpec((HEAD_DIM // 2, tm), pos), pl.BlockSpec((HEAD_DIM // 2, tm), pos),
        pl.BlockSpec((MLA_ROPE_DIM // 2, tm), pos), pl.BlockSpec((MLA_ROPE_DIM // 2, tm), pos),
    ]
    return pl.pallas_call(
        _proj_kernel, out_shape=out_shape, grid=(n_tiles + 1,),
        in_specs=in_specs, out_specs=out_specs, name="proj_heads",
        scratch_shapes=[pltpu.VMEM((w_in_t.shape[0], tm), F32)] * 2,
        compiler_params=pltpu.CompilerParams(
            dimension_semantics=("arbitrary",), vmem_limit_bytes=V7X_VMEM_LIMIT),
    )(x2, g_attn, w_in_t, w_qup_t, w_kvup_t, gq_a, gk_a, g_cq, g_ckv, gq_b, gk_b,
      cos_a, sin_a, cos_b, sin_b)


def _attend(s_past, s_diag, v_aug, bounded):
    key_i = lax.broadcasted_iota(jnp.int32, s_diag.shape, 0)
    qry_i = lax.broadcasted_iota(jnp.int32, s_diag.shape, 1)
    s_diag = jnp.where(key_i <= qry_i, s_diag, NEG_BIG)
    if bounded:
        prob = lambda s: jnp.exp2(s).astype(BF16)
    else:
        m = jnp.max(s_diag, axis=0, keepdims=True)
        if s_past is not None:
            m = jnp.maximum(m, jnp.max(s_past, axis=0, keepdims=True))
        prob = lambda s: jnp.exp2(s - m).astype(BF16)
    p = prob(s_diag)
    if s_past is not None:
        p = jnp.concatenate([prob(s_past), p], axis=0)
    r = _dot(v_aug, p)
    dv = v_aug.shape[0] - ONES_ROWS
    return r[:dv] / r[dv:dv + 1]


def _split3(a):
    hi = a.astype(BF16)
    r1 = a - hi.astype(F32)
    mid = r1.astype(BF16)
    lo = (r1 - mid.astype(F32)).astype(BF16)
    return hi, mid, lo


def _moba_kernel(q_t_ref, k_ref, v_t_ref, o_t_ref, k_ext_ref, *, bounded):
    seq = k_ref.shape[0]
    nb = seq // MOBA_BLOCK
    blk = MOBA_BLOCK
    n_sel = min(MOBA_TOPK, nb - 1)

    k_lanes = k_ref.shape[1]
    assert nb <= k_lanes
    k_ext_ref[:, 0:k_lanes] = k_ref[...]
    blk_of_key = lax.broadcasted_iota(jnp.int32, (seq, k_lanes), 0) // blk
    lane_i = lax.broadcasted_iota(jnp.int32, (seq, k_lanes), 1)
    k_ext_ref[:, k_lanes:] = jnp.where(blk_of_key == lane_i, 1.0, 0.0).astype(BF16)

    k_mean = jnp.concatenate(
        [jnp.mean(k_ref[j * blk:(j + 1) * blk, :].astype(F32), axis=0, keepdims=True)
         for j in range(nb)], axis=0)
    km_hi, km_mid, km_lo = _split3(k_mean)
    row_i = lax.broadcasted_iota(jnp.int32, (nb, seq), 0)
    blk_of_qry = lax.broadcasted_iota(jnp.int32, (nb, seq), 1) // blk
    zeros = jnp.zeros((HEAD_DIM, seq), BF16)
    bias_pad = jnp.zeros((k_lanes - nb, seq), F32)
    ones = jnp.ones((ONES_ROWS, seq), BF16)

    v_aug = [jnp.concatenate([v_t_ref[hh * HEAD_DIM:(hh + 1) * HEAD_DIM, :], ones], axis=0)
             for hh in range(HEADS_PER_STEP)]

    q_w = []
    for hh in range(HEADS_PER_STEP):
        q_h = q_t_ref[hh * HEAD_DIM:(hh + 1) * HEAD_DIM, :]
        q_pad = jnp.concatenate([q_h, zeros] if hh == 0 else [zeros, q_h], axis=0)
        gate = _dot(km_hi, q_pad) + _dot(km_mid, q_pad) + _dot(km_lo, q_pad)
        past = row_i < blk_of_qry
        gate = jnp.where(past, gate, -jnp.inf)
        rank = jnp.zeros((nb, seq), jnp.int32)
        for jp in range(nb - 1):
            g_jp = gate[jp:jp + 1, :]
            beats = (g_jp > gate) | ((g_jp == gate) & (jp < row_i))
            rank = rank + beats.astype(jnp.int32)
        bias = jnp.where(past & (rank >= n_sel), NEG_BIG, 0.0)
        q_w.append(jnp.concatenate(
            [q_pad, jnp.concatenate([bias, bias_pad], axis=0).astype(BF16)], axis=0))

    def scores(hh, i):
        s = _dot(k_ext_ref[0:(i + 1) * blk, :], q_w[hh][:, i * blk:(i + 1) * blk])
        return (s[:i * blk] if i > 0 else None), s[i * blk:]

    def finish(hh, i, s_past, s_diag):
        o = _attend(s_past, s_diag, v_aug[hh][:, :(i + 1) * blk], bounded)
        o_t_ref[hh * HEAD_DIM:(hh + 1) * HEAD_DIM, i * blk:(i + 1) * blk] = o.astype(o_t_ref.dtype)

    _pipelined_units(nb, scores, finish)


def _pipelined_units(n_qblk, scores, finish):
    units = [(hh, i) for hh in range(HEADS_PER_STEP)
             for i in (range(n_qblk) if hh % 2 == 0 else reversed(range(n_qblk)))]
    pending = [scores(*unit) for unit in units[:SCORE_LOOKAHEAD]]
    for u, unit in enumerate(units):
        if u + SCORE_LOOKAHEAD < len(units):
            pending.append(scores(*units[u + SCORE_LOOKAHEAD]))
        finish(*unit, *pending.pop(0))


def _mla_kernel(q_t_ref, k_ref, v_t_ref, o_t_ref, *, bounded):
    seq = k_ref.shape[0]
    blk = ATTN_BLOCK
    ones = jnp.ones((ONES_ROWS, seq), BF16)
    v_aug = [jnp.concatenate([v_t_ref[hh * MLA_V_DIM:(hh + 1) * MLA_V_DIM, :], ones], axis=0)
             for hh in range(HEADS_PER_STEP)]

    def scores(hh, i):
        q_rows = slice(hh * MLA_QK_PAD, (hh + 1) * MLA_QK_PAD)
        s = _dot(k_ref[0:(i + 1) * blk, q_rows], q_t_ref[q_rows, i * blk:(i + 1) * blk])
        return (s[:i * blk] if i > 0 else None), s[i * blk:]

    def finish(hh, i, s_past, s_diag):
        o = _attend(s_past, s_diag, v_aug[hh][:, :(i + 1) * blk], bounded)
        o_t_ref[hh * MLA_V_DIM:(hh + 1) * MLA_V_DIM, i * blk:(i + 1) * blk] = o.astype(o_t_ref.dtype)

    _pipelined_units(seq // blk, scores, finish)


def _attn_call(kernel, q_t, k, v_t, score_bound, *, seq, qk_rows, name, scratch_shapes=()):
    t = k.shape[0]
    n_batch = t // seq
    n_steps = q_t.shape[0] // (HEADS_PER_STEP * qk_rows)
    k_lanes = k.shape[1] // n_steps
    v_rows = v_t.shape[0] // n_steps

    def call(bounded, *operands):
        return pl.pallas_call(
            functools.partial(kernel, bounded=bounded),
            out_shape=jax.ShapeDtypeStruct(v_t.shape, BF16),
            grid=(n_batch, n_steps),
            in_specs=[
                pl.BlockSpec((HEADS_PER_STEP * qk_rows, seq), lambda b, p: (p, b)),
                pl.BlockSpec((seq, k_lanes), lambda b, p: (b, p)),
                pl.BlockSpec((v_rows, seq), lambda b, p: (p, b)),
            ],
            out_specs=pl.BlockSpec((v_rows, seq), lambda b, p: (p, b)),
            scratch_shapes=list(scratch_shapes),
            name=name + ("_bounded" if bounded else "_shifted"),
            compiler_params=pltpu.CompilerParams(
                dimension_semantics=("parallel", "parallel"), vmem_limit_bytes=V7X_VMEM_LIMIT),
        )(*operands)

    return lax.cond(score_bound <= MAX_SAFE_EXPONENT,
                    functools.partial(call, True), functools.partial(call, False),
                    q_t, k, v_t)


def _score_bound(g_q, g_k, dim):
    return (jnp.max(jnp.abs(g_q.astype(F32))) * jnp.max(jnp.abs(g_k.astype(F32)))
            * (dim ** 0.5 * LOG2_E * BF16_NORM_MARGIN))


def _out_ffn_kernel(x_ref, oa_t_ref, ob_t_ref, w_o_ref, g_ffn_ref, w_gate_ref, w_up_ref,
                    w_down_ref, out_ref):
    o_t = jnp.concatenate([oa_t_ref[...], ob_t_ref[...]], axis=0).astype(F32)
    o = o_t.T.astype(BF16)
    h = x_ref[...] + _dot(o, w_o_ref[...])
    gn = h * lax.rsqrt(jnp.mean(h * h, axis=-1, keepdims=True) + EPS) * g_ffn_ref[...]
    gn = gn.astype(BF16)
    gate = _dot(gn, w_gate_ref[...])
    up = _dot(gn, w_up_ref[...])
    act = (gate * jax.nn.sigmoid(gate) * up).astype(BF16)
    out_ref[...] = h + _dot(act, w_down_ref[...])


def _out_ffn_call(x2, oa_t, ob_t, w_o, g_ffn, w_gate, w_up, w_down, *, tm):
    t, d = x2.shape
    const = lambda i: (0, 0)

    def resident(a):
        return pl.BlockSpec(a.shape, const, pipeline_mode=pl.Buffered(1))

    return pl.pallas_call(
        _out_ffn_kernel,
        out_shape=jax.ShapeDtypeStruct((t, d), x2.dtype),
        grid=(t // tm,),
        in_specs=[
            pl.BlockSpec((tm, d), lambda i: (i, 0)),
            pl.BlockSpec((oa_t.shape[0], tm), lambda i: (0, i)),
            pl.BlockSpec((ob_t.shape[0], tm), lambda i: (0, i)),
            resident(w_o), resident(g_ffn), resident(w_gate), resident(w_up), resident(w_down),
        ],
        out_specs=pl.BlockSpec((tm, d), lambda i: (i, 0)),
        name="out_ffn",
        compiler_params=pltpu.CompilerParams(
            dimension_semantics=("parallel",), vmem_limit_bytes=V7X_VMEM_LIMIT),
    )(x2, oa_t, ob_t, w_o, g_ffn, w_gate, w_up, w_down)


def _rope_tables_t(seq, dim):
    inv = ROPE_THETA ** (-jnp.arange(0, dim, 2, dtype=F32) / dim)
    ang = jnp.arange(seq, dtype=F32)[:, None] * inv[None, :]
    return jnp.cos(ang).T, jnp.sin(ang).T


def kernel(x, attn_norm_g, w_in, moba_q_norm_g, moba_k_norm_g, mla_q_a_norm_g, w_q_up,
           mla_kv_a_norm_g, w_kv_up, mla_q_norm_g, mla_k_norm_g, w_o, ffn_norm_g,
           w_gate, w_up, w_down):
    b, s, d = x.shape
    assert s % MOBA_BLOCK == 0 and s % ATTN_BLOCK == 0
    assert s % PROJ_TILE == 0 and (b * s) % FFN_TILE == 0
    cos_a, sin_a = _rope_tables_t(s, HEAD_DIM)
    cos_b, sin_b = _rope_tables_t(s, MLA_ROPE_DIM)
    h = x.reshape(b * s, d)
    for l in range(w_in.shape[0]):
        qa_t, ka, va_t, qb_t, kb, vb_t = _proj_call(
            h, attn_norm_g[l].reshape(1, d), w_in[l].T.astype(BF16),
            w_q_up[l].T.astype(BF16), w_kv_up[l].T.astype(BF16),
            _col(moba_q_norm_g[l]) * (HEAD_DIM ** -0.5 * LOG2_E), _col(moba_k_norm_g[l]),
            _col(mla_q_a_norm_g[l]), _col(mla_kv_a_norm_g[l]),
            _col(mla_q_norm_g[l]) * (MLA_QK_DIM ** -0.5 * LOG2_E), _col(mla_k_norm_g[l]),
            cos_a, sin_a, cos_b, sin_b, seq=s, tm=PROJ_TILE)
        oa_t = _attn_call(_moba_kernel, qa_t, ka, va_t,
                          _score_bound(moba_q_norm_g[l], moba_k_norm_g[l], HEAD_DIM),
                          seq=s, qk_rows=HEAD_DIM, name="moba_attn",
                          scratch_shapes=[pltpu.VMEM((s, 2 * HEADS_PER_STEP * HEAD_DIM), BF16)])
        ob_t = _attn_call(_mla_kernel, qb_t, kb, vb_t,
                          _score_bound(mla_q_norm_g[l], mla_k_norm_g[l], MLA_QK_DIM),
                          seq=s, qk_rows=MLA_QK_PAD, name="mla_attn")
        h = _out_ffn_call(h, oa_t, ob_t, w_o[l].astype(BF16), ffn_norm_g[l].reshape(1, d),
                          w_gate[l].astype(BF16), w_up[l].astype(BF16), w_down[l].astype(BF16),
                          tm=FFN_TILE)
    return h.reshape(b, s, d).astype(x.dtype)
```

```python
import functools
import math

import jax
import jax.numpy as jnp
from jax import lax
from jax.experimental import pallas as pl
from jax.experimental.pallas import tpu as pltpu

HEAD_DIM = 64
MOBA_HEADS = 8
MOBA_WIDTH = MOBA_HEADS * HEAD_DIM
MOBA_BLOCK = 256
MOBA_TOPK = 3
MLA_HEADS = 8
MLA_Q_RANK = 256
MLA_KV_RANK = 128
MLA_NOPE_DIM = 64
MLA_ROPE_DIM = 32
MLA_V_DIM = 64
MLA_QK_DIM = MLA_NOPE_DIM + MLA_ROPE_DIM
MLA_QK_PAD = 128
MLA_WIDTH = MLA_HEADS * MLA_V_DIM
ROPE_THETA = 10000.0
EPS = 1e-6

NEG_BIG = -1e30
ATTN_BLOCK = 256
PROJ_STAGES = 3
PROJ_TILE = 256
FFN_TILE = 512
ATTN_HEADS_PER_STEP = 4
SCORE_LOOKAHEAD = 3
ONES_ROWS = 16
LOG2_E = math.log2(math.e)
MAX_SAFE_EXPONENT = 48.0
BF16_NORM_MARGIN = 1.02
V7X_VMEM_LIMIT = 56 * 1024 * 1024

F32 = jnp.float32
BF16 = jnp.bfloat16

_NT = (((1,), (1,)), ((), ()))


def _dot(a, b):
    return jnp.dot(a, b, preferred_element_type=F32)


def _rope_t(x, cos, sin):
    half = x.shape[0] // 2
    x1, x2 = x[:half], x[half:]
    return jnp.concatenate([x1 * cos - x2 * sin, x2 * cos + x1 * sin], axis=0)


def _proj_kernel(x_ref, g_attn_ref, w_in_t_ref, w_qup_t_ref, w_kvup_t_ref,
                 gq_a_ref, gk_a_ref, g_cq_ref, g_ckv_ref, gq_b_ref, gk_b_ref,
                 cos_a_ref, sin_a_ref, cos_b_ref, sin_b_ref,
                 qa_t_ref, ka_ref, va_t_ref, qb_t_ref, kb_ref, vb_t_ref,
                 hn_even_ref, hn_odd_ref, proj_even_ref, proj_odd_ref):
    t = pl.program_id(0)

    @pl.when(t == 0)
    def _():
        hn_odd_ref[...] = jnp.zeros_like(hn_odd_ref)
        proj_odd_ref[...] = jnp.zeros_like(proj_odd_ref)

    def step(hn_new_ref, hn_done_ref, proj_new_ref, proj_done_ref):
        x = x_ref[...]
        hn = x * lax.rsqrt(jnp.mean(x * x, axis=-1, keepdims=True) + EPS) * g_attn_ref[...]
        hn_new_ref[...] = hn.T.astype(BF16)
        proj_new_ref[...] = _dot(w_in_t_ref[...], hn_done_ref[...])
        _finish_heads(proj_done_ref[...], w_qup_t_ref, w_kvup_t_ref,
                      gq_a_ref, gk_a_ref, g_cq_ref, g_ckv_ref, gq_b_ref, gk_b_ref,
                      cos_a_ref, sin_a_ref, cos_b_ref, sin_b_ref,
                      qa_t_ref, ka_ref, va_t_ref, qb_t_ref, kb_ref, vb_t_ref)

    @pl.when(t % 2 == 0)
    def _():
        step(hn_even_ref, hn_odd_ref, proj_even_ref, proj_odd_ref)

    @pl.when(t % 2 == 1)
    def _():
        step(hn_odd_ref, hn_even_ref, proj_odd_ref, proj_even_ref)


def _finish_heads(proj_t, w_qup_t_ref, w_kvup_t_ref,
                  gq_a_ref, gk_a_ref, g_cq_ref, g_ckv_ref, gq_b_ref, gk_b_ref,
                  cos_a_ref, sin_a_ref, cos_b_ref, sin_b_ref,
                  qa_t_ref, ka_ref, va_t_ref, qb_t_ref, kb_ref, vb_t_ref):
    tm = proj_t.shape[1]
    o_k = MOBA_WIDTH
    o_v = 2 * MOBA_WIDTH
    o_cq = 3 * MOBA_WIDTH
    o_ckv = o_cq + MLA_Q_RANK
    o_pe = o_ckv + MLA_KV_RANK

    cos_a, sin_a = cos_a_ref[...], sin_a_ref[...]
    cos_b, sin_b = cos_b_ref[...], sin_b_ref[...]

    gq_a, gk_a = gq_a_ref[...], gk_a_ref[...]
    ka_parts = []
    for h in range(MOBA_HEADS):
        q = proj_t[h * HEAD_DIM:(h + 1) * HEAD_DIM]
        q = q * lax.rsqrt(jnp.mean(q * q, axis=0, keepdims=True) + EPS) * gq_a
        qa_t_ref[h * HEAD_DIM:(h + 1) * HEAD_DIM, :] = _rope_t(q, cos_a, sin_a).astype(BF16)
        k = proj_t[o_k + h * HEAD_DIM:o_k + (h + 1) * HEAD_DIM]
        k = k * lax.rsqrt(jnp.mean(k * k, axis=0, keepdims=True) + EPS) * gk_a
        ka_parts.append(_rope_t(k, cos_a, sin_a))
    ka_ref[...] = jnp.concatenate(ka_parts, axis=0).T.astype(BF16)
    va_t_ref[...] = proj_t[o_v:o_cq].astype(BF16)

    cq = proj_t[o_cq:o_ckv]
    cq = cq * lax.rsqrt(jnp.mean(cq * cq, axis=0, keepdims=True) + EPS) * g_cq_ref[...]
    qb = _dot(w_qup_t_ref[...], cq.astype(BF16))
    ckv = proj_t[o_ckv:o_pe]
    ckv = ckv * lax.rsqrt(jnp.mean(ckv * ckv, axis=0, keepdims=True) + EPS) * g_ckv_ref[...]
    kv = _dot(w_kvup_t_ref[...], ckv.astype(BF16))
    k_pe = proj_t[o_pe:o_pe + MLA_ROPE_DIM]

    gq_b, gk_b = gq_b_ref[...], gk_b_ref[...]
    pe_ss = jnp.sum(k_pe * k_pe, axis=0, keepdims=True)
    pe_rot = _rope_t(k_pe * gk_b[MLA_NOPE_DIM:], cos_b, sin_b)
    pad = jnp.zeros((MLA_QK_PAD - MLA_QK_DIM, tm), F32)
    kv_w = MLA_NOPE_DIM + MLA_V_DIM
    kb_parts = []
    for h in range(MLA_HEADS):
        q = qb[h * MLA_QK_DIM:(h + 1) * MLA_QK_DIM]
        q = q * lax.rsqrt(jnp.mean(q * q, axis=0, keepdims=True) + EPS) * gq_b
        q = jnp.concatenate(
            [q[:MLA_NOPE_DIM], _rope_t(q[MLA_NOPE_DIM:], cos_b, sin_b)], axis=0)
        qb_t_ref[h * MLA_QK_PAD:(h + 1) * MLA_QK_PAD, :] = (
            jnp.concatenate([q, pad], axis=0).astype(BF16))
        k_nope = kv[h * kv_w:h * kv_w + MLA_NOPE_DIM]
        ss = jnp.sum(k_nope * k_nope, axis=0, keepdims=True) + pe_ss
        r = lax.rsqrt(ss / MLA_QK_DIM + EPS)
        kb_parts += [k_nope * r * gk_b[:MLA_NOPE_DIM], pe_rot * r, pad]
        vb_t_ref[h * MLA_V_DIM:(h + 1) * MLA_V_DIM, :] = (
            kv[h * kv_w + MLA_NOPE_DIM:(h + 1) * kv_w].astype(BF16))
    kb_ref[...] = jnp.concatenate(kb_parts, axis=0).T.astype(BF16)


def _col(v):
    return v.astype(F32).reshape(-1, 1)


def _proj_call(x2, g_attn, w_in_t, w_qup_t, w_kvup_t, gq_a, gk_a, g_cq, g_ckv, gq_b, gk_b,
               cos_a, sin_a, cos_b, sin_b, *, seq, tm):
    t, d = x2.shape
    n_pos = seq // tm
    n_tiles = t // tm
    const = lambda i: (0, 0)
    x_row = lambda i: (jnp.minimum(i, n_tiles - 1), 0)
    row = lambda i: (jnp.maximum(i - PROJ_STAGES + 1, 0), 0)
    colb = lambda i: (0, jnp.maximum(i - PROJ_STAGES + 1, 0))
    pos = lambda i: (0, jnp.maximum(i - PROJ_STAGES + 1, 0) % n_pos)

    def full(a):
        return pl.BlockSpec(a.shape, const)

    out_shape = [
        jax.ShapeDtypeStruct((MOBA_WIDTH, t), BF16),
        jax.ShapeDtypeStruct((t, MOBA_WIDTH), BF16),
        jax.ShapeDtypeStruct((MOBA_WIDTH, t), BF16),
        jax.ShapeDtypeStruct((MLA_HEADS * MLA_QK_PAD, t), BF16),
        jax.ShapeDtypeStruct((t, MLA_HEADS * MLA_QK_PAD), BF16),
        jax.ShapeDtypeStruct((MLA_WIDTH, t), BF16),
    ]
    out_specs = [
        pl.BlockSpec((MOBA_WIDTH, tm), colb),
        pl.BlockSpec((tm, MOBA_WIDTH), row),
        pl.BlockSpec((MOBA_WIDTH, tm), colb),
        pl.BlockSpec((MLA_HEADS * MLA_QK_PAD, tm), colb),
        pl.BlockSpec((tm, MLA_HEADS * MLA_QK_PAD), row),
        pl.BlockSpec((MLA_WIDTH, tm), colb),
    ]
    in_specs = [
        pl.BlockSpec((tm, d), x_row), full(g_attn), full(w_in_t), full(w_qup_t), full(w_kvup_t),
        full(gq_a), full(gk_a), full(g_cq), full(g_ckv), full(gq_b), full(gk_b),
        pl.BlockSpec((HEAD_DIM // 2, tm), pos), pl.BlockSpec((HEAD_DIM // 2, tm), pos),
        pl.BlockSpec((MLA_ROPE_DIM // 2, tm), pos), pl.BlockSpec((MLA_ROPE_DIM // 2, tm), pos),
    ]
    return pl.pallas_call(
        _proj_kernel, out_shape=out_shape, grid=(n_tiles + PROJ_STAGES - 1,),
        in_specs=in_specs, out_specs=out_specs, name="proj_heads",
        scratch_shapes=[pltpu.VMEM((d, tm), BF16)] * 2
        + [pltpu.VMEM((w_in_t.shape[0], tm), F32)] * 2,
        compiler_params=pltpu.CompilerParams(
            dimension_semantics=("arbitrary",), vmem_limit_bytes=V7X_VMEM_LIMIT),
    )(x2, g_attn, w_in_t, w_qup_t, w_kvup_t, gq_a, gk_a, g_cq, g_ckv, gq_b, gk_b,
      cos_a, sin_a, cos_b, sin_b)


def _attend(s_past, s_diag, v_aug, bounded):
    key_i = lax.broadcasted_iota(jnp.int32, s_diag.shape, 0)
    qry_i = lax.broadcasted_iota(jnp.int32, s_diag.shape, 1)
    s_diag = jnp.where(key_i <= qry_i, s_diag, NEG_BIG)
    if bounded:
        prob = lambda s: jnp.exp2(s).astype(BF16)
    else:
        m = jnp.max(s_diag, axis=0, keepdims=True)
        if s_past is not None:
            m = jnp.maximum(m, jnp.max(s_past, axis=0, keepdims=True))
        prob = lambda s: jnp.exp2(s - m).astype(BF16)
    p = prob(s_diag)
    if s_past is not None:
        p = jnp.concatenate([prob(s_past), p], axis=0)
    r = _dot(v_aug, p)
    dv = v_aug.shape[0] - ONES_ROWS
    return r[:dv] / r[dv:dv + 1]


def _split3(a):
    hi = a.astype(BF16)
    r1 = a - hi.astype(F32)
    mid = r1.astype(BF16)
    lo = (r1 - mid.astype(F32)).astype(BF16)
    return hi, mid, lo


def _moba_kernel(q_t_ref, k_ref, v_t_ref, o_t_ref, k_ext_ref, *, bounded):
    seq = k_ref.shape[0]
    nb = seq // MOBA_BLOCK
    blk = MOBA_BLOCK
    n_sel = min(MOBA_TOPK, nb - 1)
    n_heads = q_t_ref.shape[0] // HEAD_DIM
    pair_lanes = 2 * HEAD_DIM
    assert nb <= pair_lanes and n_heads % 2 == 0

    row_i = lax.broadcasted_iota(jnp.int32, (nb, seq), 0)
    blk_of_qry = lax.broadcasted_iota(jnp.int32, (nb, seq), 1) // blk
    past = row_i < blk_of_qry
    blk_of_key = lax.broadcasted_iota(jnp.int32, (seq, pair_lanes), 0) // blk
    lane_i = lax.broadcasted_iota(jnp.int32, (seq, pair_lanes), 1)
    block_onehot = jnp.where(blk_of_key == lane_i, 1.0, 0.0).astype(BF16)
    zeros = jnp.zeros((HEAD_DIM, seq), BF16)
    bias_pad = jnp.zeros((pair_lanes - nb, seq), F32)
    ones = jnp.ones((ONES_ROWS, seq), BF16)

    v_aug, q_w = [], []
    for hh in range(n_heads):
        pair, half = divmod(hh, 2)
        lanes = slice(pair * pair_lanes, (pair + 1) * pair_lanes)
        if half == 0:
            k_ext_ref[pair, :, 0:pair_lanes] = k_ref[:, lanes]
            k_ext_ref[pair, :, pair_lanes:] = block_onehot
            k_mean = jnp.concatenate(
                [jnp.mean(k_ref[j * blk:(j + 1) * blk, lanes].astype(F32), axis=0, keepdims=True)
                 for j in range(nb)], axis=0)
            km_hi, km_mid, km_lo = _split3(k_mean)
        rows = slice(hh * HEAD_DIM, (hh + 1) * HEAD_DIM)
        v_aug.append(jnp.concatenate([v_t_ref[rows, :], ones], axis=0))
        q_pad = jnp.concatenate([q_t_ref[rows, :], zeros] if half == 0
                                else [zeros, q_t_ref[rows, :]], axis=0)
        gate = _dot(km_hi, q_pad) + _dot(km_mid, q_pad) + _dot(km_lo, q_pad)
        gate = jnp.where(past, gate, -jnp.inf)
        rank = jnp.zeros((nb, seq), jnp.int32)
        for jp in range(nb - 1):
            g_jp = gate[jp:jp + 1, :]
            beats = (g_jp > gate) | ((g_jp == gate) & (jp < row_i))
            rank = rank + beats.astype(jnp.int32)
        bias = jnp.where(past & (rank >= n_sel), NEG_BIG, 0.0)
        q_w.append(jnp.concatenate(
            [q_pad, jnp.concatenate([bias, bias_pad], axis=0).astype(BF16)], axis=0))

    def scores(hh, i):
        s = _dot(k_ext_ref[hh // 2, 0:(i + 1) * blk, :], q_w[hh][:, i * blk:(i + 1) * blk])
        return (s[:i * blk] if i > 0 else None), s[i * blk:]

    def finish(hh, i, s_past, s_diag):
        o = _attend(s_past, s_diag, v_aug[hh][:, :(i + 1) * blk], bounded)
        o_t_ref[hh * HEAD_DIM:(hh + 1) * HEAD_DIM, i * blk:(i + 1) * blk] = o.astype(o_t_ref.dtype)

    _pipelined_units(n_heads, nb, scores, finish)


def _pipelined_units(n_heads, n_qblk, scores, finish):
    units = [(hh, i) for hh in range(n_heads)
             for i in (range(n_qblk) if hh % 2 == 0 else reversed(range(n_qblk)))]
    pending = [scores(*unit) for unit in units[:SCORE_LOOKAHEAD]]
    for u, unit in enumerate(units):
        if u + SCORE_LOOKAHEAD < len(units):
            pending.append(scores(*units[u + SCORE_LOOKAHEAD]))
        finish(*unit, *pending.pop(0))


def _mla_kernel(q_t_ref, k_ref, v_t_ref, o_t_ref, *, bounded):
    seq = k_ref.shape[0]
    blk = ATTN_BLOCK
    n_heads = q_t_ref.shape[0] // MLA_QK_PAD
    ones = jnp.ones((ONES_ROWS, seq), BF16)
    v_aug = [jnp.concatenate([v_t_ref[hh * MLA_V_DIM:(hh + 1) * MLA_V_DIM, :], ones], axis=0)
             for hh in range(n_heads)]

    def scores(hh, i):
        q_rows = slice(hh * MLA_QK_PAD, (hh + 1) * MLA_QK_PAD)
        s = _dot(k_ref[0:(i + 1) * blk, q_rows], q_t_ref[q_rows, i * blk:(i + 1) * blk])
        return (s[:i * blk] if i > 0 else None), s[i * blk:]

    def finish(hh, i, s_past, s_diag):
        o = _attend(s_past, s_diag, v_aug[hh][:, :(i + 1) * blk], bounded)
        o_t_ref[hh * MLA_V_DIM:(hh + 1) * MLA_V_DIM, i * blk:(i + 1) * blk] = o.astype(o_t_ref.dtype)

    _pipelined_units(n_heads, seq // blk, scores, finish)


def _attn_call(kernel, q_t, k, v_t, score_bound, *, seq, qk_rows, name, scratch_shapes=()):
    heads_per_step = ATTN_HEADS_PER_STEP
    t = k.shape[0]
    n_batch = t // seq
    n_steps = q_t.shape[0] // (heads_per_step * qk_rows)
    k_lanes = k.shape[1] // n_steps
    v_rows = v_t.shape[0] // n_steps

    def call(bounded, *operands):
        return pl.pallas_call(
            functools.partial(kernel, bounded=bounded),
            out_shape=jax.ShapeDtypeStruct(v_t.shape, BF16),
            grid=(n_batch, n_steps),
            in_specs=[
                pl.BlockSpec((heads_per_step * qk_rows, seq), lambda b, p: (p, b)),
                pl.BlockSpec((seq, k_lanes), lambda b, p: (b, p)),
                pl.BlockSpec((v_rows, seq), lambda b, p: (p, b)),
            ],
            out_specs=pl.BlockSpec((v_rows, seq), lambda b, p: (p, b)),
            scratch_shapes=list(scratch_shapes),
            name=name + ("_bounded" if bounded else "_shifted"),
            compiler_params=pltpu.CompilerParams(
                dimension_semantics=("parallel", "parallel"), vmem_limit_bytes=V7X_VMEM_LIMIT),
        )(*operands)

    return lax.cond(score_bound <= MAX_SAFE_EXPONENT,
                    functools.partial(call, True), functools.partial(call, False),
                    q_t, k, v_t)


def _score_bound(g_q, g_k, dim):
    return (jnp.max(jnp.abs(g_q.astype(F32))) * jnp.max(jnp.abs(g_k.astype(F32)))
            * (dim ** 0.5 * LOG2_E * BF16_NORM_MARGIN))


def _out_ffn_kernel(x_ref, oa_t_ref, ob_t_ref, w_o_ref, g_ffn_ref, w_gate_ref, w_up_ref,
                    w_down_ref, out_ref):
    o_t = jnp.concatenate([oa_t_ref[...], ob_t_ref[...]], axis=0).astype(F32)
    o = o_t.T.astype(BF16)
    h = x_ref[...] + _dot(o, w_o_ref[...])
    gn = h * lax.rsqrt(jnp.mean(h * h, axis=-1, keepdims=True) + EPS) * g_ffn_ref[...]
    gn = gn.astype(BF16)
    gate = _dot(gn, w_gate_ref[...])
    up = _dot(gn, w_up_ref[...])
    act = (gate * jax.nn.sigmoid(gate) * up).astype(BF16)
    out_ref[...] = h + _dot(act, w_down_ref[...])


def _out_ffn_call(x2, oa_t, ob_t, w_o, g_ffn, w_gate, w_up, w_down, *, tm):
    t, d = x2.shape
    const = lambda i: (0, 0)

    def resident(a):
        return pl.BlockSpec(a.shape, const, pipeline_mode=pl.Buffered(1))

    return pl.pallas_call(
        _out_ffn_kernel,
        out_shape=jax.ShapeDtypeStruct((t, d), x2.dtype),
        grid=(t // tm,),
        in_specs=[
            pl.BlockSpec((tm, d), lambda i: (i, 0)),
            pl.BlockSpec((oa_t.shape[0], tm), lambda i: (0, i)),
            pl.BlockSpec((ob_t.shape[0], tm), lambda i: (0, i)),
            resident(w_o), resident(g_ffn), resident(w_gate), resident(w_up), resident(w_down),
        ],
        out_specs=pl.BlockSpec((tm, d), lambda i: (i, 0)),
        name="out_ffn",
        compiler_params=pltpu.CompilerParams(
            dimension_semantics=("parallel",), vmem_limit_bytes=V7X_VMEM_LIMIT),
    )(x2, oa_t, ob_t, w_o, g_ffn, w_gate, w_up, w_down)


def _rope_tables_t(seq, dim):
    inv = ROPE_THETA ** (-jnp.arange(0, dim, 2, dtype=F32) / dim)
    ang = jnp.arange(seq, dtype=F32)[:, None] * inv[None, :]
    return jnp.cos(ang).T, jnp.sin(ang).T


def kernel(x, attn_norm_g, w_in, moba_q_norm_g, moba_k_norm_g, mla_q_a_norm_g, w_q_up,
           mla_kv_a_norm_g, w_kv_up, mla_q_norm_g, mla_k_norm_g, w_o, ffn_norm_g,
           w_gate, w_up, w_down):
    b, s, d = x.shape
    assert s % MOBA_BLOCK == 0 and s % ATTN_BLOCK == 0
    assert s % PROJ_TILE == 0 and (b * s) % FFN_TILE == 0
    cos_a, sin_a = _rope_tables_t(s, HEAD_DIM)
    cos_b, sin_b = _rope_tables_t(s, MLA_ROPE_DIM)
    h = x.reshape(b * s, d)
    for l in range(w_in.shape[0]):
        qa_t, ka, va_t, qb_t, kb, vb_t = _proj_call(
            h, attn_norm_g[l].reshape(1, d), w_in[l].T.astype(BF16),
            w_q_up[l].T.astype(BF16), w_kv_up[l].T.astype(BF16),
            _col(moba_q_norm_g[l]) * (HEAD_DIM ** -0.5 * LOG2_E), _col(moba_k_norm_g[l]),
            _col(mla_q_a_norm_g[l]), _col(mla_kv_a_norm_g[l]),
            _col(mla_q_norm_g[l]) * (MLA_QK_DIM ** -0.5 * LOG2_E), _col(mla_k_norm_g[l]),
            cos_a, sin_a, cos_b, sin_b, seq=s, tm=PROJ_TILE)
        oa_t = _attn_call(_moba_kernel, qa_t, ka, va_t,
                          _score_bound(moba_q_norm_g[l], moba_k_norm_g[l], HEAD_DIM),
                          seq=s, qk_rows=HEAD_DIM, name="moba_attn",
                          scratch_shapes=[pltpu.VMEM(
                              (ATTN_HEADS_PER_STEP // 2, s, 4 * HEAD_DIM), BF16)])
        ob_t = _attn_call(_mla_kernel, qb_t, kb, vb_t,
                          _score_bound(mla_q_norm_g[l], mla_k_norm_g[l], MLA_QK_DIM),
                          seq=s, qk_rows=MLA_QK_PAD, name="mla_attn")
        h = _out_ffn_call(h, oa_t, ob_t, w_o[l].astype(BF16), ffn_norm_g[l].reshape(1, d),
                          w_gate[l].astype(BF16), w_up[l].astype(BF16), w_down[l].astype(BF16),
                          tm=FFN_TILE)
    return h.reshape(b, s, d).astype(x.dtype)
```

```python
import functools
import math

import jax
import jax.numpy as jnp
from jax import lax
from jax.experimental import pallas as pl
from jax.experimental.pallas import tpu as pltpu

HEAD_DIM = 64
MOBA_HEADS = 8
MOBA_WIDTH = MOBA_HEADS * HEAD_DIM
MOBA_BLOCK = 256
MOBA_TOPK = 3
MLA_HEADS = 8
MLA_Q_RANK = 256
MLA_KV_RANK = 128
MLA_NOPE_DIM = 64
MLA_ROPE_DIM = 32
MLA_V_DIM = 64
MLA_QK_DIM = MLA_NOPE_DIM + MLA_ROPE_DIM
MLA_QK_PAD = 128
MLA_WIDTH = MLA_HEADS * MLA_V_DIM
ROPE_THETA = 10000.0
EPS = 1e-6

NEG_BIG = -1e30
ATTN_BLOCK = 256
PROJ_TILE = 256
FFN_TILE = 512
HEADS_PER_STEP = 2
SCORE_LOOKAHEAD = 3
ONES_ROWS = 16
LOG2_E = math.log2(math.e)
MAX_SAFE_EXPONENT = 48.0
BF16_NORM_MARGIN = 1.02
V7X_VMEM_LIMIT = 56 * 1024 * 1024

F32 = jnp.float32
BF16 = jnp.bfloat16

_NT = (((1,), (1,)), ((), ()))


def _dot(a, b):
    return jnp.dot(a, b, preferred_element_type=F32)


def _rope_t(x, cos, sin):
    half = x.shape[0] // 2
    x1, x2 = x[:half], x[half:]
    return jnp.concatenate([x1 * cos - x2 * sin, x2 * cos + x1 * sin], axis=0)


def _proj_kernel(x_ref, g_attn_ref, w_in_t_ref, w_qup_t_ref, w_kvup_t_ref,
                 gq_a_ref, gk_a_ref, g_cq_ref, g_ckv_ref, gq_b_ref, gk_b_ref,
                 cos_a_ref, sin_a_ref, cos_b_ref, sin_b_ref,
                 qa_t_ref, ka_ref, va_t_ref, qb_t_ref, kb_ref, vb_t_ref,
                 proj_even_ref, proj_odd_ref):
    t = pl.program_id(0)

    @pl.when(t == 0)
    def _():
        proj_odd_ref[...] = jnp.zeros_like(proj_odd_ref)

    def step(proj_new_ref, proj_done_ref):
        x = x_ref[...]
        hn = x * lax.rsqrt(jnp.mean(x * x, axis=-1, keepdims=True) + EPS) * g_attn_ref[...]
        hn = hn.astype(BF16)

        def project(lo, hi):
            proj_new_ref[lo:hi, :] = lax.dot_general(w_in_t_ref[lo:hi, :], hn, _NT,
                                                     preferred_element_type=F32)

        finish = _finish_stages(proj_done_ref, w_qup_t_ref, w_kvup_t_ref,
                                gq_a_ref, gk_a_ref, g_cq_ref, g_ckv_ref, gq_b_ref, gk_b_ref,
                                cos_a_ref, sin_a_ref, cos_b_ref, sin_b_ref,
                                qa_t_ref, ka_ref, va_t_ref, qb_t_ref, kb_ref, vb_t_ref)
        n_rows = w_in_t_ref.shape[0]
        bounds = [0, MOBA_WIDTH, 2 * MOBA_WIDTH, 3 * MOBA_WIDTH, n_rows]
        chunks = [functools.partial(project, lo, hi) for lo, hi in zip(bounds[:-1], bounds[1:])]
        for i in range(max(len(chunks), len(finish))):
            if i < len(finish):
                finish[i]()
            if i < len(chunks):
                chunks[i]()

    @pl.when(t % 2 == 0)
    def _():
        step(proj_even_ref, proj_odd_ref)

    @pl.when(t % 2 == 1)
    def _():
        step(proj_odd_ref, proj_even_ref)


def _finish_stages(proj_ref, w_qup_t_ref, w_kvup_t_ref,
                   gq_a_ref, gk_a_ref, g_cq_ref, g_ckv_ref, gq_b_ref, gk_b_ref,
                   cos_a_ref, sin_a_ref, cos_b_ref, sin_b_ref,
                   qa_t_ref, ka_ref, va_t_ref, qb_t_ref, kb_ref, vb_t_ref):
    tm = proj_ref.shape[1]
    o_k = MOBA_WIDTH
    o_v = 2 * MOBA_WIDTH
    o_cq = 3 * MOBA_WIDTH
    o_ckv = o_cq + MLA_Q_RANK
    o_pe = o_ckv + MLA_KV_RANK
    kv_w = MLA_NOPE_DIM + MLA_V_DIM
    up = {}

    def low_rank():
        cq = proj_ref[o_cq:o_ckv, :]
        cq = cq * lax.rsqrt(jnp.mean(cq * cq, axis=0, keepdims=True) + EPS) * g_cq_ref[...]
        up["q"] = _dot(w_qup_t_ref[...], cq.astype(BF16))
        ckv = proj_ref[o_ckv:o_pe, :]
        ckv = ckv * lax.rsqrt(jnp.mean(ckv * ckv, axis=0, keepdims=True) + EPS) * g_ckv_ref[...]
        up["kv"] = _dot(w_kvup_t_ref[...], ckv.astype(BF16))

    def moba_q():
        cos_a, sin_a, gq_a = cos_a_ref[...], sin_a_ref[...], gq_a_ref[...]
        for h in range(MOBA_HEADS):
            q = proj_ref[h * HEAD_DIM:(h + 1) * HEAD_DIM, :]
            q = q * lax.rsqrt(jnp.mean(q * q, axis=0, keepdims=True) + EPS) * gq_a
            qa_t_ref[h * HEAD_DIM:(h + 1) * HEAD_DIM, :] = _rope_t(q, cos_a, sin_a).astype(BF16)

    def moba_kv():
        cos_a, sin_a, gk_a = cos_a_ref[...], sin_a_ref[...], gk_a_ref[...]
        ka_parts = []
        for h in range(MOBA_HEADS):
            k = proj_ref[o_k + h * HEAD_DIM:o_k + (h + 1) * HEAD_DIM, :]
            k = k * lax.rsqrt(jnp.mean(k * k, axis=0, keepdims=True) + EPS) * gk_a
            ka_parts.append(_rope_t(k, cos_a, sin_a))
        ka_ref[...] = jnp.concatenate(ka_parts, axis=0).T.astype(BF16)
        va_t_ref[...] = proj_ref[o_v:o_cq, :].astype(BF16)

    def mla_q():
        cos_b, sin_b, gq_b = cos_b_ref[...], sin_b_ref[...], gq_b_ref[...]
        pad = jnp.zeros((MLA_QK_PAD - MLA_QK_DIM, tm), F32)
        for h in range(MLA_HEADS):
            q = up["q"][h * MLA_QK_DIM:(h + 1) * MLA_QK_DIM]
            q = q * lax.rsqrt(jnp.mean(q * q, axis=0, keepdims=True) + EPS) * gq_b
            q = jnp.concatenate(
                [q[:MLA_NOPE_DIM], _rope_t(q[MLA_NOPE_DIM:], cos_b, sin_b), pad], axis=0)
            qb_t_ref[h * MLA_QK_PAD:(h + 1) * MLA_QK_PAD, :] = q.astype(BF16)

    def mla_kv():
        cos_b, sin_b, gk_b = cos_b_ref[...], sin_b_ref[...], gk_b_ref[...]
        kv = up["kv"]
        k_pe = proj_ref[o_pe:o_pe + MLA_ROPE_DIM, :]
        pe_ss = jnp.sum(k_pe * k_pe, axis=0, keepdims=True)
        pe_rot = _rope_t(k_pe * gk_b[MLA_NOPE_DIM:], cos_b, sin_b)
        pad = jnp.zeros((MLA_QK_PAD - MLA_QK_DIM, tm), F32)
        kb_parts = []
        for h in range(MLA_HEADS):
            k_nope = kv[h * kv_w:h * kv_w + MLA_NOPE_DIM]
            ss = jnp.sum(k_nope * k_nope, axis=0, keepdims=True) + pe_ss
            r = lax.rsqrt(ss / MLA_QK_DIM + EPS)
            kb_parts += [k_nope * r * gk_b[:MLA_NOPE_DIM], pe_rot * r, pad]
            vb_t_ref[h * MLA_V_DIM:(h + 1) * MLA_V_DIM, :] = (
                kv[h * kv_w + MLA_NOPE_DIM:(h + 1) * kv_w].astype(BF16))
        kb_ref[...] = jnp.concatenate(kb_parts, axis=0).T.astype(BF16)

    return [low_rank, moba_q, moba_kv, mla_q, mla_kv]


def _col(v):
    return v.astype(F32).reshape(-1, 1)


def _proj_call(x2, g_attn, w_in_t, w_qup_t, w_kvup_t, gq_a, gk_a, g_cq, g_ckv, gq_b, gk_b,
               cos_a, sin_a, cos_b, sin_b, *, seq, tm):
    t, d = x2.shape
    n_pos = seq // tm
    n_tiles = t // tm
    const = lambda i: (0, 0)
    x_row = lambda i: (jnp.minimum(i, n_tiles - 1), 0)
    row = lambda i: (jnp.maximum(i - 1, 0), 0)
    colb = lambda i: (0, jnp.maximum(i - 1, 0))
    pos = lambda i: (0, jnp.maximum(i - 1, 0) % n_pos)

    def full(a):
        return pl.BlockSpec(a.shape, const)

    out_shape = [
        jax.ShapeDtypeStruct((MOBA_WIDTH, t), BF16),
        jax.ShapeDtypeStruct((t, MOBA_WIDTH), BF16),
        jax.ShapeDtypeStruct((MOBA_WIDTH, t), BF16),
        jax.ShapeDtypeStruct((MLA_HEADS * MLA_QK_PAD, t), BF16),
        jax.ShapeDtypeStruct((t, MLA_HEADS * MLA_QK_PAD), BF16),
        jax.ShapeDtypeStruct((MLA_WIDTH, t), BF16),
    ]
    out_specs = [
        pl.BlockSpec((MOBA_WIDTH, tm), colb),
        pl.BlockSpec((tm, MOBA_WIDTH), row),
        pl.BlockSpec((MOBA_WIDTH, tm), colb),
        pl.BlockSpec((MLA_HEADS * MLA_QK_PAD, tm), colb),
        pl.BlockSpec((tm, MLA_HEADS * MLA_QK_PAD), row),
        pl.BlockSpec((MLA_WIDTH, tm), colb),
    ]
    in_specs = [
        pl.BlockSpec((tm, d), x_row), full(g_attn), full(w_in_t), full(w_qup_t), full(w_kvup_t),
        full(gq_a), full(gk_a), full(g_cq), full(g_ckv), full(gq_b), full(gk_b),
        pl.BlockSpec((HEAD_DIM // 2, tm), pos), pl.BlockSpec((HEAD_DIM // 2, tm), pos),
        pl.BlockSpec((MLA_ROPE_DIM // 2, tm), pos), pl.BlockSpec((MLA_ROPE_DIM // 2, tm), pos),
    ]
    return pl.pallas_call(
        _proj_kernel, out_shape=out_shape, grid=(n_tiles + 1,),
        in_specs=in_specs, out_specs=out_specs, name="proj_heads",
        scratch_shapes=[pltpu.VMEM((w_in_t.shape[0], tm), F32)] * 2,
        compiler_params=pltpu.CompilerParams(
            dimension_semantics=("arbitrary",), vmem_limit_bytes=V7X_VMEM_LIMIT),
    )(x2, g_attn, w_in_t, w_qup_t, w_kvup_t, gq_a, gk_a, g_cq, g_ckv, gq_b, gk_b,
      cos_a, sin_a, cos_b, sin_b)


def _attend(s_past, s_diag, v_aug, bounded):
    key_i = lax.broadcasted_iota(jnp.int32, s_diag.shape, 0)
    qry_i = lax.broadcasted_iota(jnp.int32, s_diag.shape, 1)
    s_diag = jnp.where(key_i <= qry_i, s_diag, NEG_BIG)
    if bounded:
        prob = lambda s: jnp.exp2(s).astype(BF16)
    else:
        m = jnp.max(s_diag, axis=0, keepdims=True)
        if s_past is not None:
            m = jnp.maximum(m, jnp.max(s_past, axis=0, keepdims=True))
        prob = lambda s: jnp.exp2(s - m).astype(BF16)
    p = prob(s_diag)
    if s_past is not None:
        p = jnp.concatenate([prob(s_past), p], axis=0)
    r = _dot(v_aug, p)
    dv = v_aug.shape[0] - ONES_ROWS
    return r[:dv] / r[dv:dv + 1]


def _split3(a):
    hi = a.astype(BF16)
    r1 = a - hi.astype(F32)
    mid = r1.astype(BF16)
    lo = (r1 - mid.astype(F32)).astype(BF16)
    return hi, mid, lo


def _moba_kernel(q_t_ref, k_ref, v_t_ref, o_t_ref, k_ext_ref, *, bounded):
    seq = k_ref.shape[0]
    nb = seq // MOBA_BLOCK
    blk = MOBA_BLOCK
    n_sel = min(MOBA_TOPK, nb - 1)

    k_lanes = k_ref.shape[1]
    assert nb <= k_lanes
    k_ext_ref[:, 0:k_lanes] = k_ref[...]
    blk_of_key = lax.broadcasted_iota(jnp.int32, (seq, k_lanes), 0) // blk
    lane_i = lax.broadcasted_iota(jnp.int32, (seq, k_lanes), 1)
    k_ext_ref[:, k_lanes:] = jnp.where(blk_of_key == lane_i, 1.0, 0.0).astype(BF16)

    k_mean = jnp.concatenate(
        [jnp.mean(k_ref[j * blk:(j + 1) * blk, :].astype(F32), axis=0, keepdims=True)
         for j in range(nb)], axis=0)
    km_hi, km_mid, km_lo = _split3(k_mean)
    row_i = lax.broadcasted_iota(jnp.int32, (nb, seq), 0)
    blk_of_qry = lax.broadcasted_iota(jnp.int32, (nb, seq), 1) // blk
    zeros = jnp.zeros((HEAD_DIM, seq), BF16)
    bias_pad = jnp.zeros((k_lanes - nb, seq), F32)
    ones = jnp.ones((ONES_ROWS, seq), BF16)

    v_aug = [jnp.concatenate([v_t_ref[hh * HEAD_DIM:(hh + 1) * HEAD_DIM, :], ones], axis=0)
             for hh in range(HEADS_PER_STEP)]

    q_w = []
    for hh in range(HEADS_PER_STEP):
        q_h = q_t_ref[hh * HEAD_DIM:(hh + 1) * HEAD_DIM, :]
        q_pad = jnp.concatenate([q_h, zeros] if hh == 0 else [zeros, q_h], axis=0)
        gate = _dot(km_hi, q_pad) + _dot(km_mid, q_pad) + _dot(km_lo, q_pad)
        past = row_i < blk_of_qry
        gate = jnp.where(past, gate, -jnp.inf)
        rank = jnp.zeros((nb, seq), jnp.int32)
        for jp in range(nb - 1):
            g_jp = gate[jp:jp + 1, :]
            beats = (g_jp > gate) | ((g_jp == gate) & (jp < row_i))
            rank = rank + beats.astype(jnp.int32)
        bias = jnp.where(past & (rank >= n_sel), NEG_BIG, 0.0)
        q_w.append(jnp.concatenate(
            [q_pad, jnp.concatenate([bias, bias_pad], axis=0).astype(BF16)], axis=0))

    def scores(hh, i):
        s = _dot(k_ext_ref[0:(i + 1) * blk, :], q_w[hh][:, i * blk:(i + 1) * blk])
        return (s[:i * blk] if i > 0 else None), s[i * blk:]

    def finish(hh, i, s_past, s_diag):
        o = _attend(s_past, s_diag, v_aug[hh][:, :(i + 1) * blk], bounded)
        o_t_ref[hh * HEAD_DIM:(hh + 1) * HEAD_DIM, i * blk:(i + 1) * blk] = o.astype(o_t_ref.dtype)

    _pipelined_units(nb, scores, finish)


def _pipelined_units(n_qblk, scores, finish):
    units = [(hh, i) for hh in range(HEADS_PER_STEP)
             for i in (range(n_qblk) if hh % 2 == 0 else reversed(range(n_qblk)))]
    pending = [scores(*unit) for unit in units[:SCORE_LOOKAHEAD]]
    for u, unit in enumerate(units):
        if u + SCORE_LOOKAHEAD < len(units):
            pending.append(scores(*units[u + SCORE_LOOKAHEAD]))
        finish(*unit, *pending.pop(0))


def _mla_kernel(q_t_ref, k_ref, v_t_ref, o_t_ref, *, bounded):
    seq = k_ref.shape[0]
    blk = ATTN_BLOCK
    ones = jnp.ones((ONES_ROWS, seq), BF16)
    v_aug = [jnp.concatenate([v_t_ref[hh * MLA_V_DIM:(hh + 1) * MLA_V_DIM, :], ones], axis=0)
             for hh in range(HEADS_PER_STEP)]

    def scores(hh, i):
        q_rows = slice(hh * MLA_QK_PAD, (hh + 1) * MLA_QK_PAD)
        s = _dot(k_ref[0:(i + 1) * blk, q_rows], q_t_ref[q_rows, i * blk:(i + 1) * blk])
        return (s[:i * blk] if i > 0 else None), s[i * blk:]

    def finish(hh, i, s_past, s_diag):
        o = _attend(s_past, s_diag, v_aug[hh][:, :(i + 1) * blk], bounded)
        o_t_ref[hh * MLA_V_DIM:(hh + 1) * MLA_V_DIM, i * blk:(i + 1) * blk] = o.astype(o_t_ref.dtype)

    _pipelined_units(seq // blk, scores, finish)


def _attn_call(kernel, q_t, k, v_t, score_bound, *, seq, qk_rows, name, scratch_shapes=()):
    t = k.shape[0]
    n_batch = t // seq
    n_steps = q_t.shape[0] // (HEADS_PER_STEP * qk_rows)
    k_lanes = k.shape[1] // n_steps
    v_rows = v_t.shape[0] // n_steps

    def call(bounded, *operands):
        return pl.pallas_call(
            functools.partial(kernel, bounded=bounded),
            out_shape=jax.ShapeDtypeStruct(v_t.shape, BF16),
            grid=(n_batch, n_steps),
            in_specs=[
                pl.BlockSpec((HEADS_PER_STEP * qk_rows, seq), lambda b, p: (p, b)),
                pl.BlockSpec((seq, k_lanes), lambda b, p: (b, p)),
                pl.BlockSpec((v_rows, seq), lambda b, p: (p, b)),
            ],
            out_specs=pl.BlockSpec((v_rows, seq), lambda b, p: (p, b)),
            scratch_shapes=list(scratch_shapes),
            name=name + ("_bounded" if bounded else "_shifted"),
            compiler_params=pltpu.CompilerParams(
                dimension_semantics=("parallel", "parallel"), vmem_limit_bytes=V7X_VMEM_LIMIT),
        )(*operands)

    return lax.cond(score_bound <= MAX_SAFE_EXPONENT,
                    functools.partial(call, True), functools.partial(call, False),
                    q_t, k, v_t)


def _score_bound(g_q, g_k, dim):
    return (jnp.max(jnp.abs(g_q.astype(F32))) * jnp.max(jnp.abs(g_k.astype(F32)))
            * (dim ** 0.5 * LOG2_E * BF16_NORM_MARGIN))


def _out_ffn_kernel(x_ref, oa_t_ref, ob_t_ref, w_o_ref, g_ffn_ref, w_gate_ref, w_up_ref,
                    w_down_ref, out_ref):
    o_t = jnp.concatenate([oa_t_ref[...], ob_t_ref[...]], axis=0).astype(F32)
    o = o_t.T.astype(BF16)
    h = x_ref[...] + _dot(o, w_o_ref[...])
    gn = h * lax.rsqrt(jnp.mean(h * h, axis=-1, keepdims=True) + EPS) * g_ffn_ref[...]
    gn = gn.astype(BF16)
    gate = _dot(gn, w_gate_ref[...])
    up = _dot(gn, w_up_ref[...])
    act = (gate * jax.nn.sigmoid(gate) * up).astype(BF16)
    out_ref[...] = h + _dot(act, w_down_ref[...])


def _out_ffn_call(x2, oa_t, ob_t, w_o, g_ffn, w_gate, w_up, w_down, *, tm):
    t, d = x2.shape
    const = lambda i: (0, 0)

    def resident(a):
        return pl.BlockSpec(a.shape, const, pipeline_mode=pl.Buffered(1))

    return pl.pallas_call(
        _out_ffn_kernel,
        out_shape=jax.ShapeDtypeStruct((t, d), x2.dtype),
        grid=(t // tm,),
        in_specs=[
            pl.BlockSpec((tm, d), lambda i: (i, 0)),
            pl.BlockSpec((oa_t.shape[0], tm), lambda i: (0, i)),
            pl.BlockSpec((ob_t.shape[0], tm), lambda i: (0, i)),
            resident(w_o), resident(g_ffn), resident(w_gate), resident(w_up), resident(w_down),
        ],
        out_specs=pl.BlockSpec((tm, d), lambda i: (i, 0)),
        name="out_ffn",
        compiler_params=pltpu.CompilerParams(
            dimension_semantics=("parallel",), vmem_limit_bytes=V7X_VMEM_LIMIT),
    )(x2, oa_t, ob_t, w_o, g_ffn, w_gate, w_up, w_down)


def _rope_tables_t(seq, dim):
    inv = ROPE_THETA ** (-jnp.arange(0, dim, 2, dtype=F32) / dim)
    ang = jnp.arange(seq, dtype=F32)[:, None] * inv[None, :]
    return jnp.cos(ang).T, jnp.sin(ang).T


def kernel(x, attn_norm_g, w_in, moba_q_norm_g, moba_k_norm_g, mla_q_a_norm_g, w_q_up,
           mla_kv_a_norm_g, w_kv_up, mla_q_norm_g, mla_k_norm_g, w_o, ffn_norm_g,
           w_gate, w_up, w_down):
    b, s, d = x.shape
    assert s % MOBA_BLOCK == 0 and s % ATTN_BLOCK == 0
    assert s % PROJ_TILE == 0 and (b * s) % FFN_TILE == 0
    cos_a, sin_a = _rope_tables_t(s, HEAD_DIM)
    cos_b, sin_b = _rope_tables_t(s, MLA_ROPE_DIM)
    h = x.reshape(b * s, d)
    for l in range(w_in.shape[0]):
        qa_t, ka, va_t, qb_t, kb, vb_t = _proj_call(
            h, attn_norm_g[l].reshape(1, d), w_in[l].T,
            w_q_up[l].T.astype(BF16), w_kv_up[l].T.astype(BF16),
            _col(moba_q_norm_g[l]) * (HEAD_DIM ** -0.5 * LOG2_E), _col(moba_k_norm_g[l]),
            _col(mla_q_a_norm_g[l]), _col(mla_kv_a_norm_g[l]),
            _col(mla_q_norm_g[l]) * (MLA_QK_DIM ** -0.5 * LOG2_E), _col(mla_k_norm_g[l]),
            cos_a, sin_a, cos_b, sin_b, seq=s, tm=PROJ_TILE)
        oa_t = _attn_call(_moba_kernel, qa_t, ka, va_t,
                          _score_bound(moba_q_norm_g[l], moba_k_norm_g[l], HEAD_DIM),
                          seq=s, qk_rows=HEAD_DIM, name="moba_attn",
                          scratch_shapes=[pltpu.VMEM((s, 2 * HEADS_PER_STEP * HEAD_DIM), BF16)])
        ob_t = _attn_call(_mla_kernel, qb_t, kb, vb_t,
                          _score_bound(mla_q_norm_g[l], mla_k_norm_g[l], MLA_QK_DIM),
                          seq=s, qk_rows=MLA_QK_PAD, name="mla_attn")
        h = _out_ffn_call(h, oa_t, ob_t, w_o[l].astype(BF16), ffn_norm_g[l].reshape(1, d),
                          w_gate[l].astype(BF16), w_up[l].astype(BF16), w_down[l].astype(BF16),
                          tm=FFN_TILE)
    return h.reshape(b, s, d).astype(x.dtype)
```

```python
import functools
import math

import jax
import jax.numpy as jnp
from jax import lax
from jax.experimental import pallas as pl
from jax.experimental.pallas import tpu as pltpu

HEAD_DIM = 64
MOBA_HEADS = 8
MOBA_WIDTH = MOBA_HEADS * HEAD_DIM
MOBA_BLOCK = 256
MOBA_TOPK = 3
MLA_HEADS = 8
MLA_Q_RANK = 256
MLA_KV_RANK = 128
MLA_NOPE_DIM = 64
MLA_ROPE_DIM = 32
MLA_V_DIM = 64
MLA_QK_DIM = MLA_NOPE_DIM + MLA_ROPE_DIM
MLA_QK_PAD = 128
MLA_WIDTH = MLA_HEADS * MLA_V_DIM
ROPE_THETA = 10000.0
EPS = 1e-6

NEG_BIG = -1e30
ATTN_BLOCK = 256
PROJ_TILE = 512
FFN_TILE = 512
HEADS_PER_STEP = 2
SCORE_LOOKAHEAD = 2
ONES_ROWS = 16
LOG2_E = math.log2(math.e)
MAX_SAFE_EXPONENT = 48.0
BF16_NORM_MARGIN = 1.02
V7X_VMEM_LIMIT = 56 * 1024 * 1024

F32 = jnp.float32
BF16 = jnp.bfloat16

_NT = (((1,), (1,)), ((), ()))


def _dot(a, b):
    return jnp.dot(a, b, preferred_element_type=F32)


def _rope_t(x, cos, sin):
    half = x.shape[0] // 2
    x1, x2 = x[:half], x[half:]
    return jnp.concatenate([x1 * cos - x2 * sin, x2 * cos + x1 * sin], axis=0)


def _proj_kernel(x_ref, g_attn_ref, w_in_t_ref, w_qup_t_ref, w_kvup_t_ref,
                 gq_a_ref, gk_a_ref, g_cq_ref, g_ckv_ref, gq_b_ref, gk_b_ref,
                 cos_a_ref, sin_a_ref, cos_b_ref, sin_b_ref,
                 qa_t_ref, ka_ref, va_t_ref, qb_t_ref, kb_ref, vb_t_ref,
                 proj_even_ref, proj_odd_ref):
    t = pl.program_id(0)

    @pl.when(t == 0)
    def _():
        proj_odd_ref[...] = jnp.zeros_like(proj_odd_ref)

    def step(proj_new_ref, proj_done_ref):
        x = x_ref[...]
        hn = x * lax.rsqrt(jnp.mean(x * x, axis=-1, keepdims=True) + EPS) * g_attn_ref[...]
        hn = hn.astype(BF16)

        def project(lo, hi):
            proj_new_ref[lo:hi, :] = lax.dot_general(w_in_t_ref[lo:hi, :], hn, _NT,
                                                     preferred_element_type=F32)

        finish = _finish_stages(proj_done_ref, w_qup_t_ref, w_kvup_t_ref,
                                gq_a_ref, gk_a_ref, g_cq_ref, g_ckv_ref, gq_b_ref, gk_b_ref,
                                cos_a_ref, sin_a_ref, cos_b_ref, sin_b_ref,
                                qa_t_ref, ka_ref, va_t_ref, qb_t_ref, kb_ref, vb_t_ref)
        n_rows = w_in_t_ref.shape[0]
        bounds = [0, MOBA_WIDTH, 2 * MOBA_WIDTH, 3 * MOBA_WIDTH, n_rows]
        chunks = [functools.partial(project, lo, hi) for lo, hi in zip(bounds[:-1], bounds[1:])]
        for i in range(max(len(chunks), len(finish))):
            if i < len(finish):
                finish[i]()
            if i < len(chunks):
                chunks[i]()

    @pl.when(t % 2 == 0)
    def _():
        step(proj_even_ref, proj_odd_ref)

    @pl.when(t % 2 == 1)
    def _():
        step(proj_odd_ref, proj_even_ref)


def _finish_stages(proj_ref, w_qup_t_ref, w_kvup_t_ref,
                   gq_a_ref, gk_a_ref, g_cq_ref, g_ckv_ref, gq_b_ref, gk_b_ref,
                   cos_a_ref, sin_a_ref, cos_b_ref, sin_b_ref,
                   qa_t_ref, ka_ref, va_t_ref, qb_t_ref, kb_ref, vb_t_ref):
    tm = proj_ref.shape[1]
    o_k = MOBA_WIDTH
    o_v = 2 * MOBA_WIDTH
    o_cq = 3 * MOBA_WIDTH
    o_ckv = o_cq + MLA_Q_RANK
    o_pe = o_ckv + MLA_KV_RANK
    kv_w = MLA_NOPE_DIM + MLA_V_DIM
    up = {}

    def low_rank():
        cq = proj_ref[o_cq:o_ckv, :]
        cq = cq * lax.rsqrt(jnp.mean(cq * cq, axis=0, keepdims=True) + EPS) * g_cq_ref[...]
        up["q"] = _dot(w_qup_t_ref[...], cq.astype(BF16))
        ckv = proj_ref[o_ckv:o_pe, :]
        ckv = ckv * lax.rsqrt(jnp.mean(ckv * ckv, axis=0, keepdims=True) + EPS) * g_ckv_ref[...]
        up["kv"] = _dot(w_kvup_t_ref[...], ckv.astype(BF16))

    def moba_q():
        cos_a, sin_a, gq_a = cos_a_ref[...], sin_a_ref[...], gq_a_ref[...]
        for h in range(MOBA_HEADS):
            q = proj_ref[h * HEAD_DIM:(h + 1) * HEAD_DIM, :]
            q = q * lax.rsqrt(jnp.mean(q * q, axis=0, keepdims=True) + EPS) * gq_a
            qa_t_ref[h * HEAD_DIM:(h + 1) * HEAD_DIM, :] = _rope_t(q, cos_a, sin_a).astype(BF16)

    def moba_kv():
        cos_a, sin_a, gk_a = cos_a_ref[...], sin_a_ref[...], gk_a_ref[...]
        ka_parts = []
        for h in range(MOBA_HEADS):
            k = proj_ref[o_k + h * HEAD_DIM:o_k + (h + 1) * HEAD_DIM, :]
            k = k * lax.rsqrt(jnp.mean(k * k, axis=0, keepdims=True) + EPS) * gk_a
            ka_parts.append(_rope_t(k, cos_a, sin_a))
        ka_ref[...] = jnp.concatenate(ka_parts, axis=0).T.astype(BF16)
        va_t_ref[...] = proj_ref[o_v:o_cq, :].astype(BF16)

    def mla_q():
        cos_b, sin_b, gq_b = cos_b_ref[...], sin_b_ref[...], gq_b_ref[...]
        pad = jnp.zeros((MLA_QK_PAD - MLA_QK_DIM, tm), F32)
        for h in range(MLA_HEADS):
            q = up["q"][h * MLA_QK_DIM:(h + 1) * MLA_QK_DIM]
            q = q * lax.rsqrt(jnp.mean(q * q, axis=0, keepdims=True) + EPS) * gq_b
            q = jnp.concatenate(
                [q[:MLA_NOPE_DIM], _rope_t(q[MLA_NOPE_DIM:], cos_b, sin_b), pad], axis=0)
            qb_t_ref[h * MLA_QK_PAD:(h + 1) * MLA_QK_PAD, :] = q.astype(BF16)

    def mla_kv():
        cos_b, sin_b, gk_b = cos_b_ref[...], sin_b_ref[...], gk_b_ref[...]
        kv = up["kv"]
        k_pe = proj_ref[o_pe:o_pe + MLA_ROPE_DIM, :]
        pe_ss = jnp.sum(k_pe * k_pe, axis=0, keepdims=True)
        pe_rot = _rope_t(k_pe * gk_b[MLA_NOPE_DIM:], cos_b, sin_b)
        pad = jnp.zeros((MLA_QK_PAD - MLA_QK_DIM, tm), F32)
        kb_parts = []
        for h in range(MLA_HEADS):
            k_nope = kv[h * kv_w:h * kv_w + MLA_NOPE_DIM]
            ss = jnp.sum(k_nope * k_nope, axis=0, keepdims=True) + pe_ss
            r = lax.rsqrt(ss / MLA_QK_DIM + EPS)
            kb_parts += [k_nope * r * gk_b[:MLA_NOPE_DIM], pe_rot * r, pad]
            vb_t_ref[h * MLA_V_DIM:(h + 1) * MLA_V_DIM, :] = (
                kv[h * kv_w + MLA_NOPE_DIM:(h + 1) * kv_w].astype(BF16))
        kb_ref[...] = jnp.concatenate(kb_parts, axis=0).T.astype(BF16)

    return [low_rank, moba_q, moba_kv, mla_q, mla_kv]


def _col(v):
    return v.astype(F32).reshape(-1, 1)


def _proj_call(x2, g_attn, w_in_t, w_qup_t, w_kvup_t, gq_a, gk_a, g_cq, g_ckv, gq_b, gk_b,
               cos_a, sin_a, cos_b, sin_b, *, seq, tm):
    t, d = x2.shape
    n_pos = seq // tm
    n_tiles = t // tm
    const = lambda i: (0, 0)
    x_row = lambda i: (jnp.minimum(i, n_tiles - 1), 0)
    row = lambda i: (jnp.maximum(i - 1, 0), 0)
    colb = lambda i: (0, jnp.maximum(i - 1, 0))
    pos = lambda i: (0, jnp.maximum(i - 1, 0) % n_pos)

    def full(a):
        return pl.BlockSpec(a.shape, const)

    out_shape = [
        jax.ShapeDtypeStruct((MOBA_WIDTH, t), BF16),
        jax.ShapeDtypeStruct((t, MOBA_WIDTH), BF16),
        jax.ShapeDtypeStruct((MOBA_WIDTH, t), BF16),
        jax.ShapeDtypeStruct((MLA_HEADS * MLA_QK_PAD, t), BF16),
        jax.ShapeDtypeStruct((t, MLA_HEADS * MLA_QK_PAD), BF16),
        jax.ShapeDtypeStruct((MLA_WIDTH, t), BF16),
    ]
    out_specs = [
        pl.BlockSpec((MOBA_WIDTH, tm), colb),
        pl.BlockSpec((tm, MOBA_WIDTH), row),
        pl.BlockSpec((MOBA_WIDTH, tm), colb),
        pl.BlockSpec((MLA_HEADS * MLA_QK_PAD, tm), colb),
        pl.BlockSpec((tm, MLA_HEADS * MLA_QK_PAD), row),
        pl.BlockSpec((MLA_WIDTH, tm), colb),
    ]
    in_specs = [
        pl.BlockSpec((tm, d), x_row), full(g_attn), full(w_in_t), full(w_qup_t), full(w_kvup_t),
        full(gq_a), full(gk_a), full(g_cq), full(g_ckv), full(gq_b), full(gk_b),
        pl.BlockSpec((HEAD_DIM // 2, tm), pos), pl.BlockSpec((HEAD_DIM // 2, tm), pos),
        pl.BlockSpec((MLA_ROPE_DIM // 2, tm), pos), pl.BlockSpec((MLA_ROPE_DIM // 2, tm), pos),
    ]
    return pl.pallas_call(
        _proj_kernel, out_shape=out_shape, grid=(n_tiles + 1,),
        in_specs=in_specs, out_specs=out_specs, name="proj_heads",
        scratch_shapes=[pltpu.VMEM((w_in_t.shape[0], tm), F32)] * 2,
        compiler_params=pltpu.CompilerParams(
            dimension_semantics=("arbitrary",), vmem_limit_bytes=V7X_VMEM_LIMIT),
    )(x2, g_attn, w_in_t, w_qup_t, w_kvup_t, gq_a, gk_a, g_cq, g_ckv, gq_b, gk_b,
      cos_a, sin_a, cos_b, sin_b)


def _attend(s_past, s_diag, v_aug, bounded):
    key_i = lax.broadcasted_iota(jnp.int32, s_diag.shape, 0)
    qry_i = lax.broadcasted_iota(jnp.int32, s_diag.shape, 1)
    s_diag = jnp.where(key_i <= qry_i, s_diag, NEG_BIG)
    if bounded:
        prob = lambda s: jnp.exp2(s).astype(BF16)
    else:
        m = jnp.max(s_diag, axis=0, keepdims=True)
        if s_past is not None:
            m = jnp.maximum(m, jnp.max(s_past, axis=0, keepdims=True))
        prob = lambda s: jnp.exp2(s - m).astype(BF16)
    p = prob(s_diag)
    if s_past is not None:
        p = jnp.concatenate([prob(s_past), p], axis=0)
    r = _dot(v_aug, p)
    dv = v_aug.shape[0] - ONES_ROWS
    return r[:dv] / r[dv:dv + 1]


def _split3(a):
    hi = a.astype(BF16)
    r1 = a - hi.astype(F32)
    mid = r1.astype(BF16)
    lo = (r1 - mid.astype(F32)).astype(BF16)
    return hi, mid, lo


def _moba_kernel(q_t_ref, k_ref, v_t_ref, o_t_ref, k_ext_ref, *, bounded):
    seq = k_ref.shape[0]
    nb = seq // MOBA_BLOCK
    blk = MOBA_BLOCK
    n_sel = min(MOBA_TOPK, nb - 1)

    k_lanes = k_ref.shape[1]
    assert nb <= k_lanes
    k_ext_ref[:, 0:k_lanes] = k_ref[...]
    blk_of_key = lax.broadcasted_iota(jnp.int32, (seq, k_lanes), 0) // blk
    lane_i = lax.broadcasted_iota(jnp.int32, (seq, k_lanes), 1)
    k_ext_ref[:, k_lanes:] = jnp.where(blk_of_key == lane_i, 1.0, 0.0).astype(BF16)

    k_mean = jnp.concatenate(
        [jnp.mean(k_ref[j * blk:(j + 1) * blk, :].astype(F32), axis=0, keepdims=True)
         for j in range(nb)], axis=0)
    km_hi, km_mid, km_lo = _split3(k_mean)
    row_i = lax.broadcasted_iota(jnp.int32, (nb, seq), 0)
    blk_of_qry = lax.broadcasted_iota(jnp.int32, (nb, seq), 1) // blk
    zeros = jnp.zeros((HEAD_DIM, seq), BF16)
    bias_pad = jnp.zeros((k_lanes - nb, seq), F32)
    ones = jnp.ones((ONES_ROWS, seq), BF16)

    v_aug = [jnp.concatenate([v_t_ref[hh * HEAD_DIM:(hh + 1) * HEAD_DIM, :], ones], axis=0)
             for hh in range(HEADS_PER_STEP)]

    q_w = []
    for hh in range(HEADS_PER_STEP):
        q_h = q_t_ref[hh * HEAD_DIM:(hh + 1) * HEAD_DIM, :]
        q_pad = jnp.concatenate([q_h, zeros] if hh == 0 else [zeros, q_h], axis=0)
        gate = _dot(km_hi, q_pad) + _dot(km_mid, q_pad) + _dot(km_lo, q_pad)
        past = row_i < blk_of_qry
        gate = jnp.where(past, gate, -jnp.inf)
        rank = jnp.zeros((nb, seq), jnp.int32)
        for jp in range(nb - 1):
            g_jp = gate[jp:jp + 1, :]
            beats = (g_jp > gate) | ((g_jp == gate) & (jp < row_i))
            rank = rank + beats.astype(jnp.int32)
        bias = jnp.where(past & (rank >= n_sel), NEG_BIG, 0.0)
        q_w.append(jnp.concatenate(
            [q_pad, jnp.concatenate([bias, bias_pad], axis=0).astype(BF16)], axis=0))

    def scores(hh, i):
        s = _dot(k_ext_ref[0:(i + 1) * blk, :], q_w[hh][:, i * blk:(i + 1) * blk])
        return (s[:i * blk] if i > 0 else None), s[i * blk:]

    def finish(hh, i, s_past, s_diag):
        o = _attend(s_past, s_diag, v_aug[hh][:, :(i + 1) * blk], bounded)
        o_t_ref[hh * HEAD_DIM:(hh + 1) * HEAD_DIM, i * blk:(i + 1) * blk] = o.astype(o_t_ref.dtype)

    _pipelined_units(nb, scores, finish)


def _pipelined_units(n_qblk, scores, finish):
    units = [(hh, i) for hh in range(HEADS_PER_STEP)
             for i in (range(n_qblk) if hh % 2 == 0 else reversed(range(n_qblk)))]
    pending = [scores(*unit) for unit in units[:SCORE_LOOKAHEAD]]
    for u, unit in enumerate(units):
        if u + SCORE_LOOKAHEAD < len(units):
            pending.append(scores(*units[u + SCORE_LOOKAHEAD]))
        finish(*unit, *pending.pop(0))


def _mla_kernel(q_t_ref, k_ref, v_t_ref, o_t_ref, *, bounded):
    seq = k_ref.shape[0]
    blk = ATTN_BLOCK
    ones = jnp.ones((ONES_ROWS, seq), BF16)
    v_aug = [jnp.concatenate([v_t_ref[hh * MLA_V_DIM:(hh + 1) * MLA_V_DIM, :], ones], axis=0)
             for hh in range(HEADS_PER_STEP)]

    def scores(hh, i):
        q_rows = slice(hh * MLA_QK_PAD, (hh + 1) * MLA_QK_PAD)
        s = _dot(k_ref[0:(i + 1) * blk, q_rows], q_t_ref[q_rows, i * blk:(i + 1) * blk])
        return (s[:i * blk] if i > 0 else None), s[i * blk:]

    def finish(hh, i, s_past, s_diag):
        o = _attend(s_past, s_diag, v_aug[hh][:, :(i + 1) * blk], bounded)
        o_t_ref[hh * MLA_V_DIM:(hh + 1) * MLA_V_DIM, i * blk:(i + 1) * blk] = o.astype(o_t_ref.dtype)

    _pipelined_units(seq // blk, scores, finish)


def _attn_call(kernel, q_t, k, v_t, score_bound, *, seq, qk_rows, name, scratch_shapes=()):
    t = k.shape[0]
    n_batch = t // seq
    n_steps = q_t.shape[0] // (HEADS_PER_STEP * qk_rows)
    k_lanes = k.shape[1] // n_steps
    v_rows = v_t.shape[0] // n_steps

    def call(bounded, *operands):
        return pl.pallas_call(
            functools.partial(kernel, bounded=bounded),
            out_shape=jax.ShapeDtypeStruct(v_t.shape, BF16),
            grid=(n_batch, n_steps),
            in_specs=[
                pl.BlockSpec((HEADS_PER_STEP * qk_rows, seq), lambda b, p: (p, b)),
                pl.BlockSpec((seq, k_lanes), lambda b, p: (b, p)),
                pl.BlockSpec((v_rows, seq), lambda b, p: (p, b)),
            ],
            out_specs=pl.BlockSpec((v_rows, seq), lambda b, p: (p, b)),
            scratch_shapes=list(scratch_shapes),
            name=name + ("_bounded" if bounded else "_shifted"),
            compiler_params=pltpu.CompilerParams(
                dimension_semantics=("parallel", "parallel"), vmem_limit_bytes=V7X_VMEM_LIMIT),
        )(*operands)

    return lax.cond(score_bound <= MAX_SAFE_EXPONENT,
                    functools.partial(call, True), functools.partial(call, False),
                    q_t, k, v_t)


def _score_bound(g_q, g_k, dim):
    return (jnp.max(jnp.abs(g_q.astype(F32))) * jnp.max(jnp.abs(g_k.astype(F32)))
            * (dim ** 0.5 * LOG2_E * BF16_NORM_MARGIN))


def _out_ffn_kernel(x_ref, oa_t_ref, ob_t_ref, w_o_ref, g_ffn_ref, w_gate_ref, w_up_ref,
                    w_down_ref, out_ref):
    o_t = jnp.concatenate([oa_t_ref[...], ob_t_ref[...]], axis=0).astype(F32)
    o = o_t.T.astype(BF16)
    h = x_ref[...] + _dot(o, w_o_ref[...])
    gn = h * lax.rsqrt(jnp.mean(h * h, axis=-1, keepdims=True) + EPS) * g_ffn_ref[...]
    gn = gn.astype(BF16)
    gate = _dot(gn, w_gate_ref[...])
    up = _dot(gn, w_up_ref[...])
    act = (gate * jax.nn.sigmoid(gate) * up).astype(BF16)
    out_ref[...] = h + _dot(act, w_down_ref[...])


def _out_ffn_call(x2, oa_t, ob_t, w_o, g_ffn, w_gate, w_up, w_down, *, tm):
    t, d = x2.shape
    const = lambda i: (0, 0)

    def resident(a):
        return pl.BlockSpec(a.shape, const, pipeline_mode=pl.Buffered(1))

    return pl.pallas_call(
        _out_ffn_kernel,
        out_shape=jax.ShapeDtypeStruct((t, d), x2.dtype),
        grid=(t // tm,),
        in_specs=[
            pl.BlockSpec((tm, d), lambda i: (i, 0)),
            pl.BlockSpec((oa_t.shape[0], tm), lambda i: (0, i)),
            pl.BlockSpec((ob_t.shape[0], tm), lambda i: (0, i)),
            resident(w_o), resident(g_ffn), resident(w_gate), resident(w_up), resident(w_down),
        ],
        out_specs=pl.BlockSpec((tm, d), lambda i: (i, 0)),
        name="out_ffn",
        compiler_params=pltpu.CompilerParams(
            dimension_semantics=("parallel",), vmem_limit_bytes=V7X_VMEM_LIMIT),
    )(x2, oa_t, ob_t, w_o, g_ffn, w_gate, w_up, w_down)


def _rope_tables_t(seq, dim):
    inv = ROPE_THETA ** (-jnp.arange(0, dim, 2, dtype=F32) / dim)
    ang = jnp.arange(seq, dtype=F32)[:, None] * inv[None, :]
    return jnp.cos(ang).T, jnp.sin(ang).T


def kernel(x, attn_norm_g, w_in, moba_q_norm_g, moba_k_norm_g, mla_q_a_norm_g, w_q_up,
           mla_kv_a_norm_g, w_kv_up, mla_q_norm_g, mla_k_norm_g, w_o, ffn_norm_g,
           w_gate, w_up, w_down):
    b, s, d = x.shape
    assert s % MOBA_BLOCK == 0 and s % ATTN_BLOCK == 0
    assert s % PROJ_TILE == 0 and (b * s) % FFN_TILE == 0
    cos_a, sin_a = _rope_tables_t(s, HEAD_DIM)
    cos_b, sin_b = _rope_tables_t(s, MLA_ROPE_DIM)
    h = x.reshape(b * s, d)
    for l in range(w_in.shape[0]):
        qa_t, ka, va_t, qb_t, kb, vb_t = _proj_call(
            h, attn_norm_g[l].reshape(1, d), w_in[l].T,
            w_q_up[l].T.astype(BF16), w_kv_up[l].T.astype(BF16),
            _col(moba_q_norm_g[l]) * (HEAD_DIM ** -0.5 * LOG2_E), _col(moba_k_norm_g[l]),
            _col(mla_q_a_norm_g[l]), _col(mla_kv_a_norm_g[l]),
            _col(mla_q_norm_g[l]) * (MLA_QK_DIM ** -0.5 * LOG2_E), _col(mla_k_norm_g[l]),
            cos_a, sin_a, cos_b, sin_b, seq=s, tm=PROJ_TILE)
        oa_t = _attn_call(_moba_kernel, qa_t, ka, va_t,
                          _score_bound(moba_q_norm_g[l], moba_k_norm_g[l], HEAD_DIM),
                          seq=s, qk_rows=HEAD_DIM, name="moba_attn",
                          scratch_shapes=[pltpu.VMEM((s, 2 * HEADS_PER_STEP * HEAD_DIM), BF16)])
        ob_t = _attn_call(_mla_kernel, qb_t, kb, vb_t,
                          _score_bound(mla_q_norm_g[l], mla_k_norm_g[l], MLA_QK_DIM),
                          seq=s, qk_rows=MLA_QK_PAD, name="mla_attn")
        h = _out_ffn_call(h, oa_t, ob_t, w_o[l].astype(BF16), ffn_norm_g[l].reshape(1, d),
                          w_gate[l].astype(BF16), w_up[l].astype(BF16), w_down[l].astype(BF16),
                          tm=FFN_TILE)
    return h.reshape(b, s, d).astype(x.dtype)
```

```python
import functools
import math

import jax
import jax.numpy as jnp
from jax import lax
from jax.experimental import pallas as pl
from jax.experimental.pallas import tpu as pltpu

HEAD_DIM = 64
MOBA_HEADS = 8
MOBA_WIDTH = MOBA_HEADS * HEAD_DIM
MOBA_BLOCK = 256
MOBA_TOPK = 3
MLA_HEADS = 8
MLA_Q_RANK = 256
MLA_KV_RANK = 128
MLA_NOPE_DIM = 64
MLA_ROPE_DIM = 32
MLA_V_DIM = 64
MLA_QK_DIM = MLA_NOPE_DIM + MLA_ROPE_DIM
MLA_QK_PAD = 128
MLA_WIDTH = MLA_HEADS * MLA_V_DIM
ROPE_THETA = 10000.0
EPS = 1e-6

NEG_BIG = -1e30
ATTN_BLOCK = 256
PROJ_TILE = 512
FFN_TILE = 512
HEADS_PER_STEP = 2
SCORE_LOOKAHEAD = 3
ONES_ROWS = 16
LOG2_E = math.log2(math.e)
MAX_SAFE_EXPONENT = 48.0
BF16_NORM_MARGIN = 1.02
V7X_VMEM_LIMIT = 56 * 1024 * 1024

F32 = jnp.float32
BF16 = jnp.bfloat16

_NT = (((1,), (1,)), ((), ()))


def _row_slices(sizes):
    out, lo = {}, 0
    for name, n in sizes:
        out[name] = slice(lo, lo + n)
        lo += n
    return out


_GAIN_ROWS = _row_slices([("q_a", HEAD_DIM), ("k_a", HEAD_DIM), ("cq", MLA_Q_RANK),
                          ("ckv", MLA_KV_RANK), ("q_b", MLA_QK_DIM), ("k_b", MLA_QK_DIM)])
_ROPE_ROWS = _row_slices([("cos_a", HEAD_DIM // 2), ("sin_a", HEAD_DIM // 2),
                          ("cos_b", MLA_ROPE_DIM // 2), ("sin_b", MLA_ROPE_DIM // 2)])


def _dot(a, b):
    return jnp.dot(a, b, preferred_element_type=F32)


def _rope_t(x, cos, sin):
    half = x.shape[0] // 2
    x1, x2 = x[:half], x[half:]
    return jnp.concatenate([x1 * cos - x2 * sin, x2 * cos + x1 * sin], axis=0)


def _proj_kernel(x_ref, g_attn_ref, w_in_t_ref, w_qup_t_ref, w_kvup_t_ref,
                 gains_ref, rope_ref,
                 qa_t_ref, ka_ref, va_t_ref, qb_t_ref, kb_ref, vb_t_ref,
                 proj_even_ref, proj_odd_ref):
    t = pl.program_id(0)

    @pl.when(t == 0)
    def _():
        proj_odd_ref[...] = jnp.zeros_like(proj_odd_ref)

    def step(proj_new_ref, proj_done_ref):
        x = x_ref[...]
        hn = x * lax.rsqrt(jnp.mean(x * x, axis=-1, keepdims=True) + EPS) * g_attn_ref[...]
        hn = hn.astype(BF16)

        def project(lo, hi):
            proj_new_ref[lo:hi, :] = lax.dot_general(w_in_t_ref[lo:hi, :], hn, _NT,
                                                     preferred_element_type=F32)

        finish = _finish_stages(proj_done_ref, w_qup_t_ref, w_kvup_t_ref,
                                gains_ref, rope_ref,
                                qa_t_ref, ka_ref, va_t_ref, qb_t_ref, kb_ref, vb_t_ref)
        n_rows = w_in_t_ref.shape[0]
        bounds = [0, MOBA_WIDTH, 2 * MOBA_WIDTH, 3 * MOBA_WIDTH, n_rows]
        chunks = [functools.partial(project, lo, hi) for lo, hi in zip(bounds[:-1], bounds[1:])]
        for i in range(max(len(chunks), len(finish))):
            if i < len(finish):
                finish[i]()
            if i < len(chunks):
                chunks[i]()

    @pl.when(t % 2 == 0)
    def _():
        step(proj_even_ref, proj_odd_ref)

    @pl.when(t % 2 == 1)
    def _():
        step(proj_odd_ref, proj_even_ref)


def _finish_stages(proj_ref, w_qup_t_ref, w_kvup_t_ref,
                   gains_ref, rope_ref,
                   qa_t_ref, ka_ref, va_t_ref, qb_t_ref, kb_ref, vb_t_ref):
    tm = proj_ref.shape[1]
    o_k = MOBA_WIDTH
    o_v = 2 * MOBA_WIDTH
    o_cq = 3 * MOBA_WIDTH
    o_ckv = o_cq + MLA_Q_RANK
    o_pe = o_ckv + MLA_KV_RANK
    kv_w = MLA_NOPE_DIM + MLA_V_DIM
    up = {}
    gain = lambda name: gains_ref[_GAIN_ROWS[name], :]
    rope = lambda name: rope_ref[_ROPE_ROWS[name], :]

    def low_rank():
        cq = proj_ref[o_cq:o_ckv, :]
        cq = cq * lax.rsqrt(jnp.mean(cq * cq, axis=0, keepdims=True) + EPS) * gain("cq")
        up["q"] = _dot(w_qup_t_ref[...], cq.astype(BF16))
        ckv = proj_ref[o_ckv:o_pe, :]
        ckv = ckv * lax.rsqrt(jnp.mean(ckv * ckv, axis=0, keepdims=True) + EPS) * gain("ckv")
        up["kv"] = _dot(w_kvup_t_ref[...], ckv.astype(BF16))

    def moba_q():
        cos_a, sin_a, gq_a = rope("cos_a"), rope("sin_a"), gain("q_a")
        for h in range(MOBA_HEADS):
            q = proj_ref[h * HEAD_DIM:(h + 1) * HEAD_DIM, :]
            q = q * lax.rsqrt(jnp.mean(q * q, axis=0, keepdims=True) + EPS) * gq_a
            qa_t_ref[h * HEAD_DIM:(h + 1) * HEAD_DIM, :] = _rope_t(q, cos_a, sin_a).astype(BF16)

    def moba_kv():
        cos_a, sin_a, gk_a = rope("cos_a"), rope("sin_a"), gain("k_a")
        ka_parts = []
        for h in range(MOBA_HEADS):
            k = proj_ref[o_k + h * HEAD_DIM:o_k + (h + 1) * HEAD_DIM, :]
            k = k * lax.rsqrt(jnp.mean(k * k, axis=0, keepdims=True) + EPS) * gk_a
            ka_parts.append(_rope_t(k, cos_a, sin_a))
        ka_ref[...] = jnp.concatenate(ka_parts, axis=0).T.astype(BF16)
        va_t_ref[...] = proj_ref[o_v:o_cq, :].astype(BF16)

    def mla_q():
        cos_b, sin_b, gq_b = rope("cos_b"), rope("sin_b"), gain("q_b")
        pad = jnp.zeros((MLA_QK_PAD - MLA_QK_DIM, tm), F32)
        for h in range(MLA_HEADS):
            q = up["q"][h * MLA_QK_DIM:(h + 1) * MLA_QK_DIM]
            q = q * lax.rsqrt(jnp.mean(q * q, axis=0, keepdims=True) + EPS) * gq_b
            q = jnp.concatenate(
                [q[:MLA_NOPE_DIM], _rope_t(q[MLA_NOPE_DIM:], cos_b, sin_b), pad], axis=0)
            qb_t_ref[h * MLA_QK_PAD:(h + 1) * MLA_QK_PAD, :] = q.astype(BF16)

    def mla_kv():
        cos_b, sin_b, gk_b = rope("cos_b"), rope("sin_b"), gain("k_b")
        kv = up["kv"]
        k_pe = proj_ref[o_pe:o_pe + MLA_ROPE_DIM, :]
        pe_ss = jnp.sum(k_pe * k_pe, axis=0, keepdims=True)
        pe_rot = _rope_t(k_pe * gk_b[MLA_NOPE_DIM:], cos_b, sin_b)
        pad = jnp.zeros((MLA_QK_PAD - MLA_QK_DIM, tm), F32)
        kb_parts = []
        for h in range(MLA_HEADS):
            k_nope = kv[h * kv_w:h * kv_w + MLA_NOPE_DIM]
            ss = jnp.sum(k_nope * k_nope, axis=0, keepdims=True) + pe_ss
            r = lax.rsqrt(ss / MLA_QK_DIM + EPS)
            kb_parts += [k_nope * r * gk_b[:MLA_NOPE_DIM], pe_rot * r, pad]
            vb_t_ref[h * MLA_V_DIM:(h + 1) * MLA_V_DIM, :] = (
                kv[h * kv_w + MLA_NOPE_DIM:(h + 1) * kv_w].astype(BF16))
        kb_ref[...] = jnp.concatenate(kb_parts, axis=0).T.astype(BF16)

    return [low_rank, moba_q, moba_kv, mla_q, mla_kv]


def _proj_call(x2, g_attn, w_in_t, w_qup_t, w_kvup_t, gains, rope, *, seq, tm):
    t, d = x2.shape
    n_pos = seq // tm
    n_tiles = t // tm
    const = lambda i: (0, 0)
    x_row = lambda i: (jnp.minimum(i, n_tiles - 1), 0)
    row = lambda i: (jnp.maximum(i - 1, 0), 0)
    colb = lambda i: (0, jnp.maximum(i - 1, 0))
    pos = lambda i: (0, jnp.maximum(i - 1, 0) % n_pos)

    def full(a):
        return pl.BlockSpec(a.shape, const)

    out_shape = [
        jax.ShapeDtypeStruct((MOBA_WIDTH, t), BF16),
        jax.ShapeDtypeStruct((t, MOBA_WIDTH), BF16),
        jax.ShapeDtypeStruct((MOBA_WIDTH, t), BF16),
        jax.ShapeDtypeStruct((MLA_HEADS * MLA_QK_PAD, t), BF16),
        jax.ShapeDtypeStruct((t, MLA_HEADS * MLA_QK_PAD), BF16),
        jax.ShapeDtypeStruct((MLA_WIDTH, t), BF16),
    ]
    out_specs = [
        pl.BlockSpec((MOBA_WIDTH, tm), colb),
        pl.BlockSpec((tm, MOBA_WIDTH), row),
        pl.BlockSpec((MOBA_WIDTH, tm), colb),
        pl.BlockSpec((MLA_HEADS * MLA_QK_PAD, tm), colb),
        pl.BlockSpec((tm, MLA_HEADS * MLA_QK_PAD), row),
        pl.BlockSpec((MLA_WIDTH, tm), colb),
    ]
    in_specs = [
        pl.BlockSpec((tm, d), x_row), full(g_attn), full(w_in_t), full(w_qup_t), full(w_kvup_t),
        full(gains), pl.BlockSpec((rope.shape[0], tm), pos),
    ]
    return pl.pallas_call(
        _proj_kernel, out_shape=out_shape, grid=(n_tiles + 1,),
        in_specs=in_specs, out_specs=out_specs, name="proj_heads",
        scratch_shapes=[pltpu.VMEM((w_in_t.shape[0], tm), F32)] * 2,
        compiler_params=pltpu.CompilerParams(
            dimension_semantics=("arbitrary",), vmem_limit_bytes=V7X_VMEM_LIMIT),
    )(x2, g_attn, w_in_t, w_qup_t, w_kvup_t, gains, rope)


def _attend(s_past, s_diag, v_aug, bounded):
    key_i = lax.broadcasted_iota(jnp.int32, s_diag.shape, 0)
    qry_i = lax.broadcasted_iota(jnp.int32, s_diag.shape, 1)
    s_diag = jnp.where(key_i <= qry_i, s_diag, NEG_BIG)
    if bounded:
        prob = lambda s: jnp.exp2(s).astype(BF16)
    else:
        m = jnp.max(s_diag, axis=0, keepdims=True)
        if s_past is not None:
            m = jnp.maximum(m, jnp.max(s_past, axis=0, keepdims=True))
        prob = lambda s: jnp.exp2(s - m).astype(BF16)
    p = prob(s_diag)
    if s_past is not None:
        p = jnp.concatenate([prob(s_past), p], axis=0)
    r = _dot(v_aug, p)
    dv = v_aug.shape[0] - ONES_ROWS
    return r[:dv] / r[dv:dv + 1]


def _split3(a):
    hi = a.astype(BF16)
    r1 = a - hi.astype(F32)
    mid = r1.astype(BF16)
    lo = (r1 - mid.astype(F32)).astype(BF16)
    return hi, mid, lo


def _moba_kernel(q_t_ref, k_ref, v_t_ref, o_t_ref, k_ext_ref, *, bounded):
    seq = k_ref.shape[0]
    nb = seq // MOBA_BLOCK
    blk = MOBA_BLOCK
    n_sel = min(MOBA_TOPK, nb - 1)

    k_lanes = k_ref.shape[1]
    assert nb <= k_lanes
    k_ext_ref[:, 0:k_lanes] = k_ref[...]
    blk_of_key = lax.broadcasted_iota(jnp.int32, (seq, k_lanes), 0) // blk
    lane_i = lax.broadcasted_iota(jnp.int32, (seq, k_lanes), 1)
    k_ext_ref[:, k_lanes:] = jnp.where(blk_of_key == lane_i, 1.0, 0.0).astype(BF16)

    k_mean = jnp.concatenate(
        [jnp.mean(k_ref[j * blk:(j + 1) * blk, :].astype(F32), axis=0, keepdims=True)
         for j in range(nb)], axis=0)
    km_hi, km_mid, km_lo = _split3(k_mean)
    row_i = lax.broadcasted_iota(jnp.int32, (nb, seq), 0)
    blk_of_qry = lax.broadcasted_iota(jnp.int32, (nb, seq), 1) // blk
    zeros = jnp.zeros((HEAD_DIM, seq), BF16)
    bias_pad = jnp.zeros((k_lanes - nb, seq), F32)
    ones = jnp.ones((ONES_ROWS, seq), BF16)

    v_aug = [jnp.concatenate([v_t_ref[hh * HEAD_DIM:(hh + 1) * HEAD_DIM, :], ones], axis=0)
             for hh in range(HEADS_PER_STEP)]

    q_w = []
    for hh in range(HEADS_PER_STEP):
        q_h = q_t_ref[hh * HEAD_DIM:(hh + 1) * HEAD_DIM, :]
        q_pad = jnp.concatenate([q_h, zeros] if hh == 0 else [zeros, q_h], axis=0)
        gate = _dot(km_hi, q_pad) + _dot(km_mid, q_pad) + _dot(km_lo, q_pad)
        past = row_i < blk_of_qry
        gate = jnp.where(past, gate, -jnp.inf)
        rank = jnp.zeros((nb, seq), jnp.int32)
        for jp in range(nb - 1):
            g_jp = gate[jp:jp + 1, :]
            beats = (g_jp > gate) | ((g_jp == gate) & (jp < row_i))
            rank = rank + beats.astype(jnp.int32)
        bias = jnp.where(past & (rank >= n_sel), NEG_BIG, 0.0)
        q_w.append(jnp.concatenate(
            [q_pad, jnp.concatenate([bias, bias_pad], axis=0).astype(BF16)], axis=0))

    def scores(hh, i):
        s = _dot(k_ext_ref[0:(i + 1) * blk, :], q_w[hh][:, i * blk:(i + 1) * blk])
        return (s[:i * blk] if i > 0 else None), s[i * blk:]

    def finish(hh, i, s_past, s_diag):
        o = _attend(s_past, s_diag, v_aug[hh][:, :(i + 1) * blk], bounded)
        o_t_ref[hh * HEAD_DIM:(hh + 1) * HEAD_DIM, i * blk:(i + 1) * blk] = o.astype(o_t_ref.dtype)

    _pipelined_units(nb, scores, finish)


def _pipelined_units(n_qblk, scores, finish):
    units = [(hh, i) for hh in range(HEADS_PER_STEP)
             for i in (range(n_qblk) if hh % 2 == 0 else reversed(range(n_qblk)))]
    pending = [scores(*unit) for unit in units[:SCORE_LOOKAHEAD]]
    for u, unit in enumerate(units):
        if u + SCORE_LOOKAHEAD < len(units):
            pending.append(scores(*units[u + SCORE_LOOKAHEAD]))
        finish(*unit, *pending.pop(0))


def _mla_kernel(q_t_ref, k_ref, v_t_ref, o_t_ref, *, bounded):
    seq = k_ref.shape[0]
    blk = ATTN_BLOCK
    ones = jnp.ones((ONES_ROWS, seq), BF16)
    v_aug = [jnp.concatenate([v_t_ref[hh * MLA_V_DIM:(hh + 1) * MLA_V_DIM, :], ones], axis=0)
             for hh in range(HEADS_PER_STEP)]

    def scores(hh, i):
        q_rows = slice(hh * MLA_QK_PAD, (hh + 1) * MLA_QK_PAD)
        s = _dot(k_ref[0:(i + 1) * blk, q_rows], q_t_ref[q_rows, i * blk:(i + 1) * blk])
        return (s[:i * blk] if i > 0 else None), s[i * blk:]

    def finish(hh, i, s_past, s_diag):
        o = _attend(s_past, s_diag, v_aug[hh][:, :(i + 1) * blk], bounded)
        o_t_ref[hh * MLA_V_DIM:(hh + 1) * MLA_V_DIM, i * blk:(i + 1) * blk] = o.astype(o_t_ref.dtype)

    _pipelined_units(seq // blk, scores, finish)


def _attn_call(kernel, q_t, k, v_t, score_bound, *, seq, qk_rows, name, scratch_shapes=()):
    t = k.shape[0]
    n_batch = t // seq
    n_steps = q_t.shape[0] // (HEADS_PER_STEP * qk_rows)
    k_lanes = k.shape[1] // n_steps
    v_rows = v_t.shape[0] // n_steps

    def call(bounded, *operands):
        return pl.pallas_call(
            functools.partial(kernel, bounded=bounded),
            out_shape=jax.ShapeDtypeStruct(v_t.shape, BF16),
            grid=(n_batch, n_steps),
            in_specs=[
                pl.BlockSpec((HEADS_PER_STEP * qk_rows, seq), lambda b, p: (p, b)),
                pl.BlockSpec((seq, k_lanes), lambda b, p: (b, p)),
                pl.BlockSpec((v_rows, seq), lambda b, p: (p, b)),
            ],
            out_specs=pl.BlockSpec((v_rows, seq), lambda b, p: (p, b)),
            scratch_shapes=list(scratch_shapes),
            name=name + ("_bounded" if bounded else "_shifted"),
            compiler_params=pltpu.CompilerParams(
                dimension_semantics=("parallel", "parallel"), vmem_limit_bytes=V7X_VMEM_LIMIT),
        )(*operands)

    return lax.cond(score_bound <= MAX_SAFE_EXPONENT,
                    functools.partial(call, True), functools.partial(call, False),
                    q_t, k, v_t)


def _score_bound(g_q, g_k, dim):
    return (jnp.max(jnp.abs(g_q.astype(F32))) * jnp.max(jnp.abs(g_k.astype(F32)))
            * (dim ** 0.5 * LOG2_E * BF16_NORM_MARGIN))


def _out_ffn_kernel(x_ref, oa_t_ref, ob_t_ref, w_o_ref, g_ffn_ref, w_gate_ref, w_up_ref,
                    w_down_ref, out_ref):
    o_t = jnp.concatenate([oa_t_ref[...], ob_t_ref[...]], axis=0).astype(F32)
    o = o_t.T.astype(BF16)
    h = x_ref[...] + _dot(o, w_o_ref[...])
    gn = h * lax.rsqrt(jnp.mean(h * h, axis=-1, keepdims=True) + EPS) * g_ffn_ref[...]
    gn = gn.astype(BF16)
    gate = _dot(gn, w_gate_ref[...])
    up = _dot(gn, w_up_ref[...])
    act = (gate * jax.nn.sigmoid(gate) * up).astype(BF16)
    out_ref[...] = h + _dot(act, w_down_ref[...])


def _out_ffn_call(x2, oa_t, ob_t, w_o, g_ffn, w_gate, w_up, w_down, *, tm):
    t, d = x2.shape
    const = lambda i: (0, 0)

    def resident(a):
        return pl.BlockSpec(a.shape, const, pipeline_mode=pl.Buffered(1))

    return pl.pallas_call(
        _out_ffn_kernel,
        out_shape=jax.ShapeDtypeStruct((t, d), x2.dtype),
        grid=(t // tm,),
        in_specs=[
            pl.BlockSpec((tm, d), lambda i: (i, 0)),
            pl.BlockSpec((oa_t.shape[0], tm), lambda i: (0, i)),
            pl.BlockSpec((ob_t.shape[0], tm), lambda i: (0, i)),
            resident(w_o), resident(g_ffn), resident(w_gate), resident(w_up), resident(w_down),
        ],
        out_specs=pl.BlockSpec((tm, d), lambda i: (i, 0)),
        name="out_ffn",
        compiler_params=pltpu.CompilerParams(
            dimension_semantics=("parallel",), vmem_limit_bytes=V7X_VMEM_LIMIT),
    )(x2, oa_t, ob_t, w_o, g_ffn, w_gate, w_up, w_down)


def _rope_tables_t(seq, dim):
    inv = ROPE_THETA ** (-jnp.arange(0, dim, 2, dtype=F32) / dim)
    ang = jnp.arange(seq, dtype=F32)[:, None] * inv[None, :]
    return jnp.cos(ang).T, jnp.sin(ang).T


def kernel(x, attn_norm_g, w_in, moba_q_norm_g, moba_k_norm_g, mla_q_a_norm_g, w_q_up,
           mla_kv_a_norm_g, w_kv_up, mla_q_norm_g, mla_k_norm_g, w_o, ffn_norm_g,
           w_gate, w_up, w_down):
    b, s, d = x.shape
    assert s % MOBA_BLOCK == 0 and s % ATTN_BLOCK == 0
    assert s % PROJ_TILE == 0 and (b * s) % FFN_TILE == 0
    rope = jnp.concatenate(_rope_tables_t(s, HEAD_DIM) + _rope_tables_t(s, MLA_ROPE_DIM), axis=0)
    h = x.reshape(b * s, d)
    for l in range(w_in.shape[0]):
        gains = jnp.concatenate([
            moba_q_norm_g[l].astype(F32) * (HEAD_DIM ** -0.5 * LOG2_E), moba_k_norm_g[l].astype(F32),
            mla_q_a_norm_g[l].astype(F32), mla_kv_a_norm_g[l].astype(F32),
            mla_q_norm_g[l].astype(F32) * (MLA_QK_DIM ** -0.5 * LOG2_E),
            mla_k_norm_g[l].astype(F32)]).reshape(-1, 1)
        qa_t, ka, va_t, qb_t, kb, vb_t = _proj_call(
            h, attn_norm_g[l].reshape(1, d), w_in[l].T, w_q_up[l].T.astype(BF16),
            w_kv_up[l].T.astype(BF16), gains, rope, seq=s, tm=PROJ_TILE)
        oa_t = _attn_call(_moba_kernel, qa_t, ka, va_t,
                          _score_bound(moba_q_norm_g[l], moba_k_norm_g[l], HEAD_DIM),
                          seq=s, qk_rows=HEAD_DIM, name="moba_attn",
                          scratch_shapes=[pltpu.VMEM((s, 2 * HEADS_PER_STEP * HEAD_DIM), BF16)])
        ob_t = _attn_call(_mla_kernel, qb_t, kb, vb_t,
                          _score_bound(mla_q_norm_g[l], mla_k_norm_g[l], MLA_QK_DIM),
                          seq=s, qk_rows=MLA_QK_PAD, name="mla_attn")
        h = _out_ffn_call(h, oa_t, ob_t, w_o[l].astype(BF16), ffn_norm_g[l].reshape(1, d),
                          w_gate[l].astype(BF16), w_up[l].astype(BF16), w_down[l].astype(BF16),
                          tm=FFN_TILE)
    return h.reshape(b, s, d).astype(x.dtype)
```

```python
import functools
import math

import jax
import jax.numpy as jnp
from jax import lax
from jax.experimental import pallas as pl
from jax.experimental.pallas import tpu as pltpu

HEAD_DIM = 64
MOBA_HEADS = 8
MOBA_WIDTH = MOBA_HEADS * HEAD_DIM
MOBA_BLOCK = 256
MOBA_TOPK = 3
MLA_HEADS = 8
MLA_Q_RANK = 256
MLA_KV_RANK = 128
MLA_NOPE_DIM = 64
MLA_ROPE_DIM = 32
MLA_V_DIM = 64
MLA_QK_DIM = MLA_NOPE_DIM + MLA_ROPE_DIM
MLA_QK_PAD = 128
MLA_WIDTH = MLA_HEADS * MLA_V_DIM
ROPE_THETA = 10000.0
EPS = 1e-6

NEG_BIG = -1e30
ATTN_BLOCK = 256
PROJ_TILE = 512
FFN_TILE = 1024
HEADS_PER_STEP = 2
SCORE_LOOKAHEAD = 3
ONES_ROWS = 16
LOG2_E = math.log2(math.e)
MAX_SAFE_EXPONENT = 48.0
BF16_NORM_MARGIN = 1.02
V7X_VMEM_LIMIT = 56 * 1024 * 1024

F32 = jnp.float32
BF16 = jnp.bfloat16

_NT = (((1,), (1,)), ((), ()))


def _row_slices(sizes):
    out, lo = {}, 0
    for name, n in sizes:
        out[name] = slice(lo, lo + n)
        lo += n
    return out


_GAIN_ROWS = _row_slices([("q_a", HEAD_DIM), ("k_a", HEAD_DIM), ("cq", MLA_Q_RANK),
                          ("ckv", MLA_KV_RANK), ("q_b", MLA_QK_DIM), ("k_b", MLA_QK_DIM)])
_ROPE_ROWS = _row_slices([("cos_a", HEAD_DIM // 2), ("sin_a", HEAD_DIM // 2),
                          ("cos_b", MLA_ROPE_DIM // 2), ("sin_b", MLA_ROPE_DIM // 2)])


def _dot(a, b):
    return jnp.dot(a, b, preferred_element_type=F32)


def _rope_t(x, cos, sin):
    half = x.shape[0] // 2
    x1, x2 = x[:half], x[half:]
    return jnp.concatenate([x1 * cos - x2 * sin, x2 * cos + x1 * sin], axis=0)


def _proj_kernel(x_ref, g_attn_ref, w_in_t_ref, w_qup_t_ref, w_kvup_t_ref,
                 gains_ref, rope_ref,
                 qa_t_ref, ka_ref, va_t_ref, qb_t_ref, kb_ref, vb_t_ref,
                 proj_even_ref, proj_odd_ref):
    t = pl.program_id(0)

    @pl.when(t == 0)
    def _():
        proj_odd_ref[...] = jnp.zeros_like(proj_odd_ref)

    def step(proj_new_ref, proj_done_ref):
        x = x_ref[...]
        hn = x * lax.rsqrt(jnp.mean(x * x, axis=-1, keepdims=True) + EPS) * g_attn_ref[...]
        hn = hn.astype(BF16)

        def project(lo, hi):
            proj_new_ref[lo:hi, :] = lax.dot_general(w_in_t_ref[lo:hi, :], hn, _NT,
                                                     preferred_element_type=F32)

        finish = _finish_stages(proj_done_ref, w_qup_t_ref, w_kvup_t_ref,
                                gains_ref, rope_ref,
                                qa_t_ref, ka_ref, va_t_ref, qb_t_ref, kb_ref, vb_t_ref)
        n_rows = w_in_t_ref.shape[0]
        bounds = [0, MOBA_WIDTH, 2 * MOBA_WIDTH, 3 * MOBA_WIDTH, n_rows]
        chunks = [functools.partial(project, lo, hi) for lo, hi in zip(bounds[:-1], bounds[1:])]
        for i in range(max(len(chunks), len(finish))):
            if i < len(finish):
                finish[i]()
            if i < len(chunks):
                chunks[i]()

    @pl.when(t % 2 == 0)
    def _():
        step(proj_even_ref, proj_odd_ref)

    @pl.when(t % 2 == 1)
    def _():
        step(proj_odd_ref, proj_even_ref)


def _finish_stages(proj_ref, w_qup_t_ref, w_kvup_t_ref,
                   gains_ref, rope_ref,
                   qa_t_ref, ka_ref, va_t_ref, qb_t_ref, kb_ref, vb_t_ref):
    tm = proj_ref.shape[1]
    o_k = MOBA_WIDTH
    o_v = 2 * MOBA_WIDTH
    o_cq = 3 * MOBA_WIDTH
    o_ckv = o_cq + MLA_Q_RANK
    o_pe = o_ckv + MLA_KV_RANK
    kv_w = MLA_NOPE_DIM + MLA_V_DIM
    up = {}
    gain = lambda name: gains_ref[_GAIN_ROWS[name], :]
    rope = lambda name: rope_ref[_ROPE_ROWS[name], :]

    def low_rank():
        cq = proj_ref[o_cq:o_ckv, :]
        cq = cq * lax.rsqrt(jnp.mean(cq * cq, axis=0, keepdims=True) + EPS) * gain("cq")
        up["q"] = _dot(w_qup_t_ref[...], cq.astype(BF16))
        ckv = proj_ref[o_ckv:o_pe, :]
        ckv = ckv * lax.rsqrt(jnp.mean(ckv * ckv, axis=0, keepdims=True) + EPS) * gain("ckv")
        up["kv"] = _dot(w_kvup_t_ref[...], ckv.astype(BF16))

    def moba_q():
        cos_a, sin_a, gq_a = rope("cos_a"), rope("sin_a"), gain("q_a")
        for h in range(MOBA_HEADS):
            q = proj_ref[h * HEAD_DIM:(h + 1) * HEAD_DIM, :]
            q = q * lax.rsqrt(jnp.mean(q * q, axis=0, keepdims=True) + EPS) * gq_a
            qa_t_ref[h * HEAD_DIM:(h + 1) * HEAD_DIM, :] = _rope_t(q, cos_a, sin_a).astype(BF16)

    def moba_kv():
        cos_a, sin_a, gk_a = rope("cos_a"), rope("sin_a"), gain("k_a")
        ka_parts = []
        for h in range(MOBA_HEADS):
            k = proj_ref[o_k + h * HEAD_DIM:o_k + (h + 1) * HEAD_DIM, :]
            k = k * lax.rsqrt(jnp.mean(k * k, axis=0, keepdims=True) + EPS) * gk_a
            ka_parts.append(_rope_t(k, cos_a, sin_a))
        ka_ref[...] = jnp.concatenate(ka_parts, axis=0).T.astype(BF16)
        va_t_ref[...] = proj_ref[o_v:o_cq, :].astype(BF16)

    def mla_q():
        cos_b, sin_b, gq_b = rope("cos_b"), rope("sin_b"), gain("q_b")
        pad = jnp.zeros((MLA_QK_PAD - MLA_QK_DIM, tm), F32)
        for h in range(MLA_HEADS):
            q = up["q"][h * MLA_QK_DIM:(h + 1) * MLA_QK_DIM]
            q = q * lax.rsqrt(jnp.mean(q * q, axis=0, keepdims=True) + EPS) * gq_b
            q = jnp.concatenate(
                [q[:MLA_NOPE_DIM], _rope_t(q[MLA_NOPE_DIM:], cos_b, sin_b), pad], axis=0)
            qb_t_ref[h * MLA_QK_PAD:(h + 1) * MLA_QK_PAD, :] = q.astype(BF16)

    def mla_kv():
        cos_b, sin_b, gk_b = rope("cos_b"), rope("sin_b"), gain("k_b")
        kv = up["kv"]
        k_pe = proj_ref[o_pe:o_pe + MLA_ROPE_DIM, :]
        pe_ss = jnp.sum(k_pe * k_pe, axis=0, keepdims=True)
        pe_rot = _rope_t(k_pe * gk_b[MLA_NOPE_DIM:], cos_b, sin_b)
        pad = jnp.zeros((MLA_QK_PAD - MLA_QK_DIM, tm), F32)
        kb_parts = []
        for h in range(MLA_HEADS):
            k_nope = kv[h * kv_w:h * kv_w + MLA_NOPE_DIM]
            ss = jnp.sum(k_nope * k_nope, axis=0, keepdims=True) + pe_ss
            r = lax.rsqrt(ss / MLA_QK_DIM + EPS)
            kb_parts += [k_nope * r * gk_b[:MLA_NOPE_DIM], pe_rot * r, pad]
            vb_t_ref[h * MLA_V_DIM:(h + 1) * MLA_V_DIM, :] = (
                kv[h * kv_w + MLA_NOPE_DIM:(h + 1) * kv_w].astype(BF16))
        kb_ref[...] = jnp.concatenate(kb_parts, axis=0).T.astype(BF16)

    return [low_rank, moba_q, moba_kv, mla_q, mla_kv]


def _proj_call(x2, g_attn, w_in_t, w_qup_t, w_kvup_t, gains, rope, *, seq, tm):
    t, d = x2.shape
    n_pos = seq // tm
    n_tiles = t // tm
    const = lambda i: (0, 0)
    x_row = lambda i: (jnp.minimum(i, n_tiles - 1), 0)
    row = lambda i: (jnp.maximum(i - 1, 0), 0)
    colb = lambda i: (0, jnp.maximum(i - 1, 0))
    pos = lambda i: (0, jnp.maximum(i - 1, 0) % n_pos)

    def full(a):
        return pl.BlockSpec(a.shape, const)

    out_shape = [
        jax.ShapeDtypeStruct((MOBA_WIDTH, t), BF16),
        jax.ShapeDtypeStruct((t, MOBA_WIDTH), BF16),
        jax.ShapeDtypeStruct((MOBA_WIDTH, t), BF16),
        jax.ShapeDtypeStruct((MLA_HEADS * MLA_QK_PAD, t), BF16),
        jax.ShapeDtypeStruct((t, MLA_HEADS * MLA_QK_PAD), BF16),
        jax.ShapeDtypeStruct((MLA_WIDTH, t), BF16),
    ]
    out_specs = [
        pl.BlockSpec((MOBA_WIDTH, tm), colb),
        pl.BlockSpec((tm, MOBA_WIDTH), row),
        pl.BlockSpec((MOBA_WIDTH, tm), colb),
        pl.BlockSpec((MLA_HEADS * MLA_QK_PAD, tm), colb),
        pl.BlockSpec((tm, MLA_HEADS * MLA_QK_PAD), row),
        pl.BlockSpec((MLA_WIDTH, tm), colb),
    ]
    in_specs = [
        pl.BlockSpec((tm, d), x_row), full(g_attn), full(w_in_t), full(w_qup_t), full(w_kvup_t),
        full(gains), pl.BlockSpec((rope.shape[0], tm), pos),
    ]
    return pl.pallas_call(
        _proj_kernel, out_shape=out_shape, grid=(n_tiles + 1,),
        in_specs=in_specs, out_specs=out_specs, name="proj_heads",
        scratch_shapes=[pltpu.VMEM((w_in_t.shape[0], tm), F32)] * 2,
        compiler_params=pltpu.CompilerParams(
            dimension_semantics=("arbitrary",), vmem_limit_bytes=V7X_VMEM_LIMIT),
    )(x2, g_attn, w_in_t, w_qup_t, w_kvup_t, gains, rope)


def _attend(s_past, s_diag, v_aug, bounded):
    key_i = lax.broadcasted_iota(jnp.int32, s_diag.shape, 0)
    qry_i = lax.broadcasted_iota(jnp.int32, s_diag.shape, 1)
    s_diag = jnp.where(key_i <= qry_i, s_diag, NEG_BIG)
    if bounded:
        prob = lambda s: jnp.exp2(s).astype(BF16)
    else:
        m = jnp.max(s_diag, axis=0, keepdims=True)
        if s_past is not None:
            m = jnp.maximum(m, jnp.max(s_past, axis=0, keepdims=True))
        prob = lambda s: jnp.exp2(s - m).astype(BF16)
    p = prob(s_diag)
    if s_past is not None:
        p = jnp.concatenate([prob(s_past), p], axis=0)
    r = _dot(v_aug, p)
    dv = v_aug.shape[0] - ONES_ROWS
    return r[:dv] / r[dv:dv + 1]


def _split3(a):
    hi = a.astype(BF16)
    r1 = a - hi.astype(F32)
    mid = r1.astype(BF16)
    lo = (r1 - mid.astype(F32)).astype(BF16)
    return hi, mid, lo


def _moba_kernel(q_t_ref, k_ref, v_t_ref, o_t_ref, k_ext_ref, *, bounded):
    seq = k_ref.shape[0]
    nb = seq // MOBA_BLOCK
    blk = MOBA_BLOCK
    n_sel = min(MOBA_TOPK, nb - 1)

    k_lanes = k_ref.shape[1]
    assert nb <= k_lanes
    k_ext_ref[:, 0:k_lanes] = k_ref[...]

    @pl.when((pl.program_id(0) == 0) & (pl.program_id(1) == 0))
    def _():
        blk_of_key = lax.broadcasted_iota(jnp.int32, (seq, k_lanes), 0) // blk
        lane_i = lax.broadcasted_iota(jnp.int32, (seq, k_lanes), 1)
        k_ext_ref[:, k_lanes:] = jnp.where(blk_of_key == lane_i, 1.0, 0.0).astype(BF16)

    ind_rows = 2 * nb
    in_block = (lax.broadcasted_iota(jnp.int32, (ind_rows, seq), 1) // blk
                == lax.broadcasted_iota(jnp.int32, (ind_rows, seq), 0))
    k_sum = _dot(jnp.where(in_block, 1.0, 0.0).astype(BF16), k_ref[...])
    k_mean = k_sum[:nb] * (1.0 / blk)
    km_stack = jnp.concatenate([p.astype(F32) for p in _split3(k_mean)]
                               + [jnp.zeros_like(k_mean)], axis=0).astype(BF16)
    row_i = lax.broadcasted_iota(jnp.int32, (nb, seq), 0)
    blk_of_qry = lax.broadcasted_iota(jnp.int32, (nb, seq), 1) // blk
    zeros = jnp.zeros((HEAD_DIM, seq), BF16)
    bias_pad = jnp.zeros((k_lanes - nb, seq), F32)
    ones = jnp.ones((ONES_ROWS, seq), BF16)

    v_aug = [jnp.concatenate([v_t_ref[hh * HEAD_DIM:(hh + 1) * HEAD_DIM, :], ones], axis=0)
             for hh in range(HEADS_PER_STEP)]

    q_w = []
    for hh in range(HEADS_PER_STEP):
        q_h = q_t_ref[hh * HEAD_DIM:(hh + 1) * HEAD_DIM, :]
        q_pad = jnp.concatenate([q_h, zeros] if hh == 0 else [zeros, q_h], axis=0)
        parts = _dot(km_stack, q_pad)
        gate = parts[:nb] + parts[nb:2 * nb] + parts[2 * nb:3 * nb]
        past = row_i < blk_of_qry
        gate = jnp.where(past, gate, -jnp.inf)
        rank = jnp.zeros((nb, seq), jnp.int32)
        for jp in range(nb - 1):
            g_jp = gate[jp:jp + 1, :]
            beats = (g_jp > gate) | ((g_jp == gate) & (jp < row_i))
            rank = rank + beats.astype(jnp.int32)
        bias = jnp.where(past & (rank >= n_sel), NEG_BIG, 0.0)
        q_w.append(jnp.concatenate(
            [q_pad, jnp.concatenate([bias, bias_pad], axis=0).astype(BF16)], axis=0))

    def scores(hh, i):
        s = _dot(k_ext_ref[0:(i + 1) * blk, :], q_w[hh][:, i * blk:(i + 1) * blk])
        return (s[:i * blk] if i > 0 else None), s[i * blk:]

    def finish(hh, i, s_past, s_diag):
        o = _attend(s_past, s_diag, v_aug[hh][:, :(i + 1) * blk], bounded)
        o_t_ref[hh * HEAD_DIM:(hh + 1) * HEAD_DIM, i * blk:(i + 1) * blk] = o.astype(o_t_ref.dtype)

    _pipelined_units(nb, scores, finish)


def _pipelined_units(n_qblk, scores, finish):
    units = [(hh, i) for hh in range(HEADS_PER_STEP)
             for i in (range(n_qblk) if hh % 2 == 0 else reversed(range(n_qblk)))]
    pending = [scores(*unit) for unit in units[:SCORE_LOOKAHEAD]]
    for u, unit in enumerate(units):
        if u + SCORE_LOOKAHEAD < len(units):
            pending.append(scores(*units[u + SCORE_LOOKAHEAD]))
        finish(*unit, *pending.pop(0))


def _mla_kernel(q_t_ref, k_ref, v_t_ref, o_t_ref, *, bounded):
    seq = k_ref.shape[0]
    blk = ATTN_BLOCK
    ones = jnp.ones((ONES_ROWS, seq), BF16)
    v_aug = [jnp.concatenate([v_t_ref[hh * MLA_V_DIM:(hh + 1) * MLA_V_DIM, :], ones], axis=0)
             for hh in range(HEADS_PER_STEP)]

    def scores(hh, i):
        q_rows = slice(hh * MLA_QK_PAD, (hh + 1) * MLA_QK_PAD)
        s = _dot(k_ref[0:(i + 1) * blk, q_rows], q_t_ref[q_rows, i * blk:(i + 1) * blk])
        return (s[:i * blk] if i > 0 else None), s[i * blk:]

    def finish(hh, i, s_past, s_diag):
        o = _attend(s_past, s_diag, v_aug[hh][:, :(i + 1) * blk], bounded)
        o_t_ref[hh * MLA_V_DIM:(hh + 1) * MLA_V_DIM, i * blk:(i + 1) * blk] = o.astype(o_t_ref.dtype)

    _pipelined_units(seq // blk, scores, finish)


def _attn_call(kernel, q_t, k, v_t, score_bound, *, seq, qk_rows, name, scratch_shapes=()):
    t = k.shape[0]
    n_batch = t // seq
    n_steps = q_t.shape[0] // (HEADS_PER_STEP * qk_rows)
    k_lanes = k.shape[1] // n_steps
    v_rows = v_t.shape[0] // n_steps

    def call(bounded, *operands):
        return pl.pallas_call(
            functools.partial(kernel, bounded=bounded),
            out_shape=jax.ShapeDtypeStruct(v_t.shape, BF16),
            grid=(n_batch, n_steps),
            in_specs=[
                pl.BlockSpec((HEADS_PER_STEP * qk_rows, seq), lambda b, p: (p, b)),
                pl.BlockSpec((seq, k_lanes), lambda b, p: (b, p)),
                pl.BlockSpec((v_rows, seq), lambda b, p: (p, b)),
            ],
            out_specs=pl.BlockSpec((v_rows, seq), lambda b, p: (p, b)),
            scratch_shapes=list(scratch_shapes),
            name=name + ("_bounded" if bounded else "_shifted"),
            compiler_params=pltpu.CompilerParams(
                dimension_semantics=("arbitrary", "arbitrary") if scratch_shapes
                else ("parallel", "parallel"), vmem_limit_bytes=V7X_VMEM_LIMIT),
        )(*operands)

    return lax.cond(score_bound <= MAX_SAFE_EXPONENT,
                    functools.partial(call, True), functools.partial(call, False),
                    q_t, k, v_t)


def _score_bound(g_q, g_k, dim):
    return (jnp.max(jnp.abs(g_q.astype(F32))) * jnp.max(jnp.abs(g_k.astype(F32)))
            * (dim ** 0.5 * LOG2_E * BF16_NORM_MARGIN))


def _out_ffn_kernel(x_ref, oa_t_ref, ob_t_ref, w_o_ref, g_ffn_ref, w_gate_ref, w_up_ref,
                    w_down_ref, out_ref):
    o_t = jnp.concatenate([oa_t_ref[...], ob_t_ref[...]], axis=0).astype(F32)
    o = o_t.T.astype(BF16)
    h = x_ref[...] + _dot(o, w_o_ref[...])
    gn = h * lax.rsqrt(jnp.mean(h * h, axis=-1, keepdims=True) + EPS) * g_ffn_ref[...]
    gn = gn.astype(BF16)
    gate = _dot(gn, w_gate_ref[...])
    up = _dot(gn, w_up_ref[...])
    act = (gate * jax.nn.sigmoid(gate) * up).astype(BF16)
    out_ref[...] = h + _dot(act, w_down_ref[...])


def _out_ffn_call(x2, oa_t, ob_t, w_o, g_ffn, w_gate, w_up, w_down, *, tm):
    t, d = x2.shape
    const = lambda i: (0, 0)

    def resident(a):
        return pl.BlockSpec(a.shape, const, pipeline_mode=pl.Buffered(1))

    return pl.pallas_call(
        _out_ffn_kernel,
        out_shape=jax.ShapeDtypeStruct((t, d), x2.dtype),
        grid=(t // tm,),
        in_specs=[
            pl.BlockSpec((tm, d), lambda i: (i, 0)),
            pl.BlockSpec((oa_t.shape[0], tm), lambda i: (0, i)),
            pl.BlockSpec((ob_t.shape[0], tm), lambda i: (0, i)),
            resident(w_o), resident(g_ffn), resident(w_gate), resident(w_up), resident(w_down),
        ],
        out_specs=pl.BlockSpec((tm, d), lambda i: (i, 0)),
        name="out_ffn",
        compiler_params=pltpu.CompilerParams(
            dimension_semantics=("parallel",), vmem_limit_bytes=V7X_VMEM_LIMIT),
    )(x2, oa_t, ob_t, w_o, g_ffn, w_gate, w_up, w_down)


def _rope_tables_t(seq, dim):
    inv = ROPE_THETA ** (-jnp.arange(0, dim, 2, dtype=F32) / dim)
    ang = jnp.arange(seq, dtype=F32)[:, None] * inv[None, :]
    return jnp.cos(ang).T, jnp.sin(ang).T


def kernel(x, attn_norm_g, w_in, moba_q_norm_g, moba_k_norm_g, mla_q_a_norm_g, w_q_up,
           mla_kv_a_norm_g, w_kv_up, mla_q_norm_g, mla_k_norm_g, w_o, ffn_norm_g,
           w_gate, w_up, w_down):
    b, s, d = x.shape
    assert s % MOBA_BLOCK == 0 and s % ATTN_BLOCK == 0
    assert s % PROJ_TILE == 0 and (b * s) % FFN_TILE == 0
    rope = jnp.concatenate(_rope_tables_t(s, HEAD_DIM) + _rope_tables_t(s, MLA_ROPE_DIM), axis=0)
    h = x.reshape(b * s, d)
    for l in range(w_in.shape[0]):
        gains = jnp.concatenate([
            moba_q_norm_g[l].astype(F32) * (HEAD_DIM ** -0.5 * LOG2_E), moba_k_norm_g[l].astype(F32),
            mla_q_a_norm_g[l].astype(F32), mla_kv_a_norm_g[l].astype(F32),
            mla_q_norm_g[l].astype(F32) * (MLA_QK_DIM ** -0.5 * LOG2_E),
            mla_k_norm_g[l].astype(F32)]).reshape(-1, 1)
        qa_t, ka, va_t, qb_t, kb, vb_t = _proj_call(
            h, attn_norm_g[l].reshape(1, d), w_in[l].T, w_q_up[l].T.astype(BF16),
            w_kv_up[l].T.astype(BF16), gains, rope, seq=s, tm=PROJ_TILE)
        oa_t = _attn_call(_moba_kernel, qa_t, ka, va_t,
                          _score_bound(moba_q_norm_g[l], moba_k_norm_g[l], HEAD_DIM),
                          seq=s, qk_rows=HEAD_DIM, name="moba_attn",
                          scratch_shapes=[pltpu.VMEM((s, 2 * HEADS_PER_STEP * HEAD_DIM), BF16)])
        ob_t = _attn_call(_mla_kernel, qb_t, kb, vb_t,
                          _score_bound(mla_q_norm_g[l], mla_k_norm_g[l], MLA_QK_DIM),
                          seq=s, qk_rows=MLA_QK_PAD, name="mla_attn")
        h = _out_ffn_call(h, oa_t, ob_t, w_o[l].astype(BF16), ffn_norm_g[l].reshape(1, d),
                          w_gate[l].astype(BF16), w_up[l].astype(BF16), w_down[l].astype(BF16),
                          tm=FFN_TILE)
    return h.reshape(b, s, d).astype(x.dtype)
```

```python
import functools
import math

import jax
import jax.numpy as jnp
from jax import lax
from jax.experimental import pallas as pl
from jax.experimental.pallas import tpu as pltpu

HEAD_DIM = 64
MOBA_HEADS = 8
MOBA_WIDTH = MOBA_HEADS * HEAD_DIM
MOBA_BLOCK = 256
MOBA_TOPK = 3
MLA_HEADS = 8
MLA_Q_RANK = 256
MLA_KV_RANK = 128
MLA_NOPE_DIM = 64
MLA_ROPE_DIM = 32
MLA_V_DIM = 64
MLA_QK_DIM = MLA_NOPE_DIM + MLA_ROPE_DIM
MLA_QK_PAD = 128
MLA_WIDTH = MLA_HEADS * MLA_V_DIM
ROPE_THETA = 10000.0
EPS = 1e-6

NEG_BIG = -1e30
ATTN_BLOCK = 256
PROJ_TILE = 512
FFN_TILE = 1024
HEADS_PER_STEP = 2
SCORE_LOOKAHEAD = 3
ONES_ROWS = 16
LOG2_E = math.log2(math.e)
MAX_SAFE_EXPONENT = 48.0
BF16_NORM_MARGIN = 1.02
V7X_VMEM_LIMIT = 56 * 1024 * 1024

F32 = jnp.float32
BF16 = jnp.bfloat16

_NT = (((1,), (1,)), ((), ()))


def _row_slices(sizes):
    out, lo = {}, 0
    for name, n in sizes:
        out[name] = slice(lo, lo + n)
        lo += n
    return out


_GAIN_ROWS = _row_slices([("cq", MLA_Q_RANK), ("ckv", MLA_KV_RANK),
                          ("q_b", MLA_QK_DIM), ("k_b", MLA_QK_DIM)])
_ROPE_ROWS = _row_slices([("q_a", 2 * HEAD_DIM), ("k_a", 2 * HEAD_DIM),
                          ("cos_b", MLA_ROPE_DIM // 2), ("sin_b", MLA_ROPE_DIM // 2)])


def _dot(a, b):
    return jnp.dot(a, b, preferred_element_type=F32)


def _gained_rope_table(g, cos, sin):
    half = cos.shape[0]
    g1, g2 = g[:half, None], g[half:, None]
    return jnp.concatenate([g1 * cos, g2 * sin, g2 * cos, g1 * sin], axis=0)


def _gained_rope_t(n, table):
    half = n.shape[0] // 2
    n1, n2 = n[:half], n[half:]
    t = [table[i * half:(i + 1) * half] for i in range(4)]
    return jnp.concatenate([n1 * t[0] - n2 * t[1], n2 * t[2] + n1 * t[3]], axis=0)


def _rope_t(x, cos, sin):
    half = x.shape[0] // 2
    x1, x2 = x[:half], x[half:]
    return jnp.concatenate([x1 * cos - x2 * sin, x2 * cos + x1 * sin], axis=0)


def _proj_kernel(x_ref, w_in_t_ref, w_qup_t_ref, w_kvup_t_ref,
                 gains_ref, rope_ref,
                 qa_t_ref, ka_ref, va_t_ref, qb_t_ref, kb_ref, vb_t_ref,
                 proj_even_ref, proj_odd_ref):
    t = pl.program_id(0)

    @pl.when(t == 0)
    def _():
        proj_odd_ref[...] = jnp.zeros_like(proj_odd_ref)

    def step(proj_new_ref, proj_done_ref):
        x = x_ref[...]
        hn = (x * lax.rsqrt(jnp.mean(x * x, axis=-1, keepdims=True) + EPS)).astype(BF16)

        def project(lo, hi):
            proj_new_ref[lo:hi, :] = lax.dot_general(w_in_t_ref[lo:hi, :], hn, _NT,
                                                     preferred_element_type=F32)

        finish = _finish_stages(proj_done_ref, w_qup_t_ref, w_kvup_t_ref,
                                gains_ref, rope_ref,
                                qa_t_ref, ka_ref, va_t_ref, qb_t_ref, kb_ref, vb_t_ref)
        n_rows = w_in_t_ref.shape[0]
        bounds = [0, MOBA_WIDTH, 2 * MOBA_WIDTH, 3 * MOBA_WIDTH, n_rows]
        chunks = [functools.partial(project, lo, hi) for lo, hi in zip(bounds[:-1], bounds[1:])]
        for i in range(max(len(chunks), len(finish))):
            if i < len(finish):
                finish[i]()
            if i < len(chunks):
                chunks[i]()

    @pl.when(t % 2 == 0)
    def _():
        step(proj_even_ref, proj_odd_ref)

    @pl.when(t % 2 == 1)
    def _():
        step(proj_odd_ref, proj_even_ref)


def _finish_stages(proj_ref, w_qup_t_ref, w_kvup_t_ref,
                   gains_ref, rope_ref,
                   qa_t_ref, ka_ref, va_t_ref, qb_t_ref, kb_ref, vb_t_ref):
    tm = proj_ref.shape[1]
    o_k = MOBA_WIDTH
    o_v = 2 * MOBA_WIDTH
    o_cq = 3 * MOBA_WIDTH
    o_ckv = o_cq + MLA_Q_RANK
    o_pe = o_ckv + MLA_KV_RANK
    kv_w = MLA_NOPE_DIM + MLA_V_DIM
    up = {}
    gain = lambda name: gains_ref[_GAIN_ROWS[name], :]
    rope = lambda name: rope_ref[_ROPE_ROWS[name], :]

    def low_rank():
        cq = proj_ref[o_cq:o_ckv, :]
        cq = cq * lax.rsqrt(jnp.mean(cq * cq, axis=0, keepdims=True) + EPS) * gain("cq")
        up["q"] = _dot(w_qup_t_ref[...], cq.astype(BF16))
        ckv = proj_ref[o_ckv:o_pe, :]
        ckv = ckv * lax.rsqrt(jnp.mean(ckv * ckv, axis=0, keepdims=True) + EPS) * gain("ckv")
        up["kv"] = _dot(w_kvup_t_ref[...], ckv.astype(BF16))

    def moba_q():
        table = rope("q_a")
        for h in range(MOBA_HEADS):
            q = proj_ref[h * HEAD_DIM:(h + 1) * HEAD_DIM, :]
            q = q * lax.rsqrt(jnp.mean(q * q, axis=0, keepdims=True) + EPS)
            qa_t_ref[h * HEAD_DIM:(h + 1) * HEAD_DIM, :] = _gained_rope_t(q, table).astype(BF16)

    def moba_kv():
        table = rope("k_a")
        ka_parts = []
        for h in range(MOBA_HEADS):
            k = proj_ref[o_k + h * HEAD_DIM:o_k + (h + 1) * HEAD_DIM, :]
            k = k * lax.rsqrt(jnp.mean(k * k, axis=0, keepdims=True) + EPS)
            ka_parts.append(_gained_rope_t(k, table))
        ka_ref[...] = jnp.concatenate(ka_parts, axis=0).T.astype(BF16)
        va_t_ref[...] = proj_ref[o_v:o_cq, :].astype(BF16)

    def mla_q():
        cos_b, sin_b, gq_b = rope("cos_b"), rope("sin_b"), gain("q_b")
        pad = jnp.zeros((MLA_QK_PAD - MLA_QK_DIM, tm), F32)
        for h in range(MLA_HEADS):
            q = up["q"][h * MLA_QK_DIM:(h + 1) * MLA_QK_DIM]
            q = q * lax.rsqrt(jnp.mean(q * q, axis=0, keepdims=True) + EPS) * gq_b
            q = jnp.concatenate(
                [q[:MLA_NOPE_DIM], _rope_t(q[MLA_NOPE_DIM:], cos_b, sin_b), pad], axis=0)
            qb_t_ref[h * MLA_QK_PAD:(h + 1) * MLA_QK_PAD, :] = q.astype(BF16)

    def mla_kv():
        cos_b, sin_b, gk_b = rope("cos_b"), rope("sin_b"), gain("k_b")
        kv = up["kv"]
        k_pe = proj_ref[o_pe:o_pe + MLA_ROPE_DIM, :]
        pe_ss = jnp.sum(k_pe * k_pe, axis=0, keepdims=True)
        pe_rot = _rope_t(k_pe * gk_b[MLA_NOPE_DIM:], cos_b, sin_b)
        pad = jnp.zeros((MLA_QK_PAD - MLA_QK_DIM, tm), F32)
        kb_parts = []
        for h in range(MLA_HEADS):
            k_nope = kv[h * kv_w:h * kv_w + MLA_NOPE_DIM]
            ss = jnp.sum(k_nope * k_nope, axis=0, keepdims=True) + pe_ss
            r = lax.rsqrt(ss / MLA_QK_DIM + EPS)
            kb_parts += [k_nope * r * gk_b[:MLA_NOPE_DIM], pe_rot * r, pad]
            vb_t_ref[h * MLA_V_DIM:(h + 1) * MLA_V_DIM, :] = (
                kv[h * kv_w + MLA_NOPE_DIM:(h + 1) * kv_w].astype(BF16))
        kb_ref[...] = jnp.concatenate(kb_parts, axis=0).T.astype(BF16)

    return [low_rank, moba_q, moba_kv, mla_q, mla_kv]


def _proj_call(x2, w_in_t, w_qup_t, w_kvup_t, gains, rope, *, seq, tm):
    t, d = x2.shape
    n_pos = seq // tm
    n_tiles = t // tm
    const = lambda i: (0, 0)
    x_row = lambda i: (jnp.minimum(i, n_tiles - 1), 0)
    row = lambda i: (jnp.maximum(i - 1, 0), 0)
    colb = lambda i: (0, jnp.maximum(i - 1, 0))
    pos = lambda i: (0, jnp.maximum(i - 1, 0) % n_pos)

    def full(a):
        return pl.BlockSpec(a.shape, const, pipeline_mode=pl.Buffered(1))

    out_shape = [
        jax.ShapeDtypeStruct((MOBA_WIDTH, t), BF16),
        jax.ShapeDtypeStruct((t, MOBA_WIDTH), BF16),
        jax.ShapeDtypeStruct((MOBA_WIDTH, t), BF16),
        jax.ShapeDtypeStruct((MLA_HEADS * MLA_QK_PAD, t), BF16),
        jax.ShapeDtypeStruct((t, MLA_HEADS * MLA_QK_PAD), BF16),
        jax.ShapeDtypeStruct((MLA_WIDTH, t), BF16),
    ]
    out_specs = [
        pl.BlockSpec((MOBA_WIDTH, tm), colb),
        pl.BlockSpec((tm, MOBA_WIDTH), row),
        pl.BlockSpec((MOBA_WIDTH, tm), colb),
        pl.BlockSpec((MLA_HEADS * MLA_QK_PAD, tm), colb),
        pl.BlockSpec((tm, MLA_HEADS * MLA_QK_PAD), row),
        pl.BlockSpec((MLA_WIDTH, tm), colb),
    ]
    in_specs = [
        pl.BlockSpec((tm, d), x_row), full(w_in_t), full(w_qup_t), full(w_kvup_t),
        full(gains), pl.BlockSpec((rope.shape[0], tm), pos),
    ]
    return pl.pallas_call(
        _proj_kernel, out_shape=out_shape, grid=(n_tiles + 1,),
        in_specs=in_specs, out_specs=out_specs, name="proj_heads",
        scratch_shapes=[pltpu.VMEM((w_in_t.shape[0], tm), F32)] * 2,
        compiler_params=pltpu.CompilerParams(
            dimension_semantics=("arbitrary",), vmem_limit_bytes=V7X_VMEM_LIMIT),
    )(x2, w_in_t, w_qup_t, w_kvup_t, gains, rope)


def _attend(s_past, s_diag, v_aug, bounded):
    key_i = lax.broadcasted_iota(jnp.int32, s_diag.shape, 0)
    qry_i = lax.broadcasted_iota(jnp.int32, s_diag.shape, 1)
    s_diag = jnp.where(key_i <= qry_i, s_diag, NEG_BIG)
    if bounded:
        prob = lambda s: jnp.exp2(s).astype(BF16)
    else:
        m = jnp.max(s_diag, axis=0, keepdims=True)
        if s_past is not None:
            m = jnp.maximum(m, jnp.max(s_past, axis=0, keepdims=True))
        prob = lambda s: jnp.exp2(s - m).astype(BF16)
    p = prob(s_diag)
    if s_past is not None:
        p = jnp.concatenate([prob(s_past), p], axis=0)
    r = _dot(v_aug, p)
    dv = v_aug.shape[0] - ONES_ROWS
    return r[:dv] / r[dv:dv + 1]


def _split3(a):
    hi = a.astype(BF16)
    r1 = a - hi.astype(F32)
    mid = r1.astype(BF16)
    lo = (r1 - mid.astype(F32)).astype(BF16)
    return hi, mid, lo


def _moba_kernel(q_t_ref, k_ref, v_t_ref, o_t_ref, k_ext_ref, *, bounded):
    seq = k_ref.shape[0]
    nb = seq // MOBA_BLOCK
    blk = MOBA_BLOCK
    n_sel = min(MOBA_TOPK, nb - 1)

    k_lanes = k_ref.shape[1]
    assert nb <= k_lanes
    k_ext_ref[:, 0:k_lanes] = k_ref[...]

    @pl.when((pl.program_id(0) == 0) & (pl.program_id(1) == 0))
    def _():
        blk_of_key = lax.broadcasted_iota(jnp.int32, (seq, k_lanes), 0) // blk
        lane_i = lax.broadcasted_iota(jnp.int32, (seq, k_lanes), 1)
        k_ext_ref[:, k_lanes:] = jnp.where(blk_of_key == lane_i, 1.0, 0.0).astype(BF16)

    ind_rows = 2 * nb
    in_block = (lax.broadcasted_iota(jnp.int32, (ind_rows, seq), 1) // blk
                == lax.broadcasted_iota(jnp.int32, (ind_rows, seq), 0))
    k_sum = _dot(jnp.where(in_block, 1.0, 0.0).astype(BF16), k_ref[...])
    k_mean = k_sum[:nb] * (1.0 / blk)
    km_stack = jnp.concatenate([p.astype(F32) for p in _split3(k_mean)]
                               + [jnp.zeros_like(k_mean)], axis=0).astype(BF16)
    row_i = lax.broadcasted_iota(jnp.int32, (nb, seq), 0)
    blk_of_qry = lax.broadcasted_iota(jnp.int32, (nb, seq), 1) // blk
    zeros = jnp.zeros((HEAD_DIM, seq), BF16)
    bias_pad = jnp.zeros((k_lanes - nb, seq), F32)
    ones = jnp.ones((ONES_ROWS, seq), BF16)

    v_aug = [jnp.concatenate([v_t_ref[hh * HEAD_DIM:(hh + 1) * HEAD_DIM, :], ones], axis=0)
             for hh in range(HEADS_PER_STEP)]

    q_w = []
    for hh in range(HEADS_PER_STEP):
        q_h = q_t_ref[hh * HEAD_DIM:(hh + 1) * HEAD_DIM, :]
        q_pad = jnp.concatenate([q_h, zeros] if hh == 0 else [zeros, q_h], axis=0)
        parts = _dot(km_stack, q_pad)
        gate = parts[:nb] + parts[nb:2 * nb] + parts[2 * nb:3 * nb]
        past = row_i < blk_of_qry
        gate = jnp.where(past, gate, -jnp.inf)
        rank = jnp.zeros((nb, seq), jnp.int32)
        for jp in range(nb - 1):
            g_jp = gate[jp:jp + 1, :]
            beats = (g_jp > gate) | ((g_jp == gate) & (jp < row_i))
            rank = rank + beats.astype(jnp.int32)
        bias = jnp.where(past & (rank >= n_sel), NEG_BIG, 0.0)
        q_w.append(jnp.concatenate(
            [q_pad, jnp.concatenate([bias, bias_pad], axis=0).astype(BF16)], axis=0))

    def scores(hh, i):
        s = _dot(k_ext_ref[0:(i + 1) * blk, :], q_w[hh][:, i * blk:(i + 1) * blk])
        return (s[:i * blk] if i > 0 else None), s[i * blk:]

    def finish(hh, i, s_past, s_diag):
        o = _attend(s_past, s_diag, v_aug[hh][:, :(i + 1) * blk], bounded)
        o_t_ref[hh * HEAD_DIM:(hh + 1) * HEAD_DIM, i * blk:(i + 1) * blk] = o.astype(o_t_ref.dtype)

    _pipelined_units(nb, scores, finish)


def _pipelined_units(n_qblk, scores, finish):
    units = [(hh, i) for hh in range(HEADS_PER_STEP)
             for i in (range(n_qblk) if hh % 2 == 0 else reversed(range(n_qblk)))]
    pending = [scores(*unit) for unit in units[:SCORE_LOOKAHEAD]]
    for u, unit in enumerate(units):
        if u + SCORE_LOOKAHEAD < len(units):
            pending.append(scores(*units[u + SCORE_LOOKAHEAD]))
        finish(*unit, *pending.pop(0))


def _mla_kernel(q_t_ref, k_ref, v_t_ref, o_t_ref, *, bounded):
    seq = k_ref.shape[0]
    blk = ATTN_BLOCK
    ones = jnp.ones((ONES_ROWS, seq), BF16)
    v_aug = [jnp.concatenate([v_t_ref[hh * MLA_V_DIM:(hh + 1) * MLA_V_DIM, :], ones], axis=0)
             for hh in range(HEADS_PER_STEP)]

    def scores(hh, i):
        q_rows = slice(hh * MLA_QK_PAD, (hh + 1) * MLA_QK_PAD)
        s = _dot(k_ref[0:(i + 1) * blk, q_rows], q_t_ref[q_rows, i * blk:(i + 1) * blk])
        return (s[:i * blk] if i > 0 else None), s[i * blk:]

    def finish(hh, i, s_past, s_diag):
        o = _attend(s_past, s_diag, v_aug[hh][:, :(i + 1) * blk], bounded)
        o_t_ref[hh * MLA_V_DIM:(hh + 1) * MLA_V_DIM, i * blk:(i + 1) * blk] = o.astype(o_t_ref.dtype)

    _pipelined_units(seq // blk, scores, finish)


def _attn_call(kernel, q_t, k, v_t, score_bound, *, seq, qk_rows, name, scratch_shapes=()):
    t = k.shape[0]
    n_batch = t // seq
    n_steps = q_t.shape[0] // (HEADS_PER_STEP * qk_rows)
    k_lanes = k.shape[1] // n_steps
    v_rows = v_t.shape[0] // n_steps

    def call(bounded, *operands):
        return pl.pallas_call(
            functools.partial(kernel, bounded=bounded),
            out_shape=jax.ShapeDtypeStruct(v_t.shape, BF16),
            grid=(n_batch, n_steps),
            in_specs=[
                pl.BlockSpec((HEADS_PER_STEP * qk_rows, seq), lambda b, p: (p, b)),
                pl.BlockSpec((seq, k_lanes), lambda b, p: (b, p)),
                pl.BlockSpec((v_rows, seq), lambda b, p: (p, b)),
            ],
            out_specs=pl.BlockSpec((v_rows, seq), lambda b, p: (p, b)),
            scratch_shapes=list(scratch_shapes),
            name=name + ("_bounded" if bounded else "_shifted"),
            compiler_params=pltpu.CompilerParams(
                dimension_semantics=("arbitrary", "arbitrary") if scratch_shapes
                else ("parallel", "parallel"), vmem_limit_bytes=V7X_VMEM_LIMIT),
        )(*operands)

    return lax.cond(score_bound <= MAX_SAFE_EXPONENT,
                    functools.partial(call, True), functools.partial(call, False),
                    q_t, k, v_t)


def _score_bound(g_q, g_k, dim):
    return (jnp.max(jnp.abs(g_q.astype(F32))) * jnp.max(jnp.abs(g_k.astype(F32)))
            * (dim ** 0.5 * LOG2_E * BF16_NORM_MARGIN))


def _out_ffn_kernel(x_ref, oa_t_ref, ob_t_ref, w_o_ref, g_ffn_ref, w_gate_ref, w_up_ref,
                    w_down_ref, out_ref):
    o_t = jnp.concatenate([oa_t_ref[...], ob_t_ref[...]], axis=0).astype(F32)
    o = o_t.T.astype(BF16)
    h = x_ref[...] + _dot(o, w_o_ref[...])
    gn = h * lax.rsqrt(jnp.mean(h * h, axis=-1, keepdims=True) + EPS) * g_ffn_ref[...]
    gn = gn.astype(BF16)
    gate = _dot(gn, w_gate_ref[...])
    up = _dot(gn, w_up_ref[...])
    act = (gate * jax.nn.sigmoid(gate) * up).astype(BF16)
    out_ref[...] = h + _dot(act, w_down_ref[...])


def _out_ffn_call(x2, oa_t, ob_t, w_o, g_ffn, w_gate, w_up, w_down, *, tm):
    t, d = x2.shape
    const = lambda i: (0, 0)

    def resident(a):
        return pl.BlockSpec(a.shape, const, pipeline_mode=pl.Buffered(1))

    return pl.pallas_call(
        _out_ffn_kernel,
        out_shape=jax.ShapeDtypeStruct((t, d), x2.dtype),
        grid=(t // tm,),
        in_specs=[
            pl.BlockSpec((tm, d), lambda i: (i, 0)),
            pl.BlockSpec((oa_t.shape[0], tm), lambda i: (0, i)),
            pl.BlockSpec((ob_t.shape[0], tm), lambda i: (0, i)),
            resident(w_o), resident(g_ffn), resident(w_gate), resident(w_up), resident(w_down),
        ],
        out_specs=pl.BlockSpec((tm, d), lambda i: (i, 0)),
        name="out_ffn",
        compiler_params=pltpu.CompilerParams(
            dimension_semantics=("parallel",), vmem_limit_bytes=V7X_VMEM_LIMIT),
    )(x2, oa_t, ob_t, w_o, g_ffn, w_gate, w_up, w_down)


def _rope_tables_t(seq, dim):
    inv = ROPE_THETA ** (-jnp.arange(0, dim, 2, dtype=F32) / dim)
    ang = jnp.arange(seq, dtype=F32)[:, None] * inv[None, :]
    return jnp.cos(ang).T, jnp.sin(ang).T


def kernel(x, attn_norm_g, w_in, moba_q_norm_g, moba_k_norm_g, mla_q_a_norm_g, w_q_up,
           mla_kv_a_norm_g, w_kv_up, mla_q_norm_g, mla_k_norm_g, w_o, ffn_norm_g,
           w_gate, w_up, w_down):
    b, s, d = x.shape
    assert s % MOBA_BLOCK == 0 and s % ATTN_BLOCK == 0
    assert s % PROJ_TILE == 0 and (b * s) % FFN_TILE == 0
    cos_a, sin_a = _rope_tables_t(s, HEAD_DIM)
    cos_b, sin_b = _rope_tables_t(s, MLA_ROPE_DIM)
    h = x.reshape(b * s, d)
    for l in range(w_in.shape[0]):
        gains = jnp.concatenate([
            mla_q_a_norm_g[l].astype(F32), mla_kv_a_norm_g[l].astype(F32),
            mla_q_norm_g[l].astype(F32) * (MLA_QK_DIM ** -0.5 * LOG2_E),
            mla_k_norm_g[l].astype(F32)]).reshape(-1, 1)
        rope = jnp.concatenate([
            _gained_rope_table(moba_q_norm_g[l].astype(F32) * (HEAD_DIM ** -0.5 * LOG2_E),
                               cos_a, sin_a),
            _gained_rope_table(moba_k_norm_g[l].astype(F32), cos_a, sin_a), cos_b, sin_b], axis=0)
        w_in_t = (w_in[l] * attn_norm_g[l].astype(F32)[:, None]).T
        qa_t, ka, va_t, qb_t, kb, vb_t = _proj_call(
            h, w_in_t, w_q_up[l].T.astype(BF16), w_kv_up[l].T.astype(BF16), gains, rope,
            seq=s, tm=PROJ_TILE)
        oa_t = _attn_call(_moba_kernel, qa_t, ka, va_t,
                          _score_bound(moba_q_norm_g[l], moba_k_norm_g[l], HEAD_DIM),
                          seq=s, qk_rows=HEAD_DIM, name="moba_attn",
                          scratch_shapes=[pltpu.VMEM((s, 2 * HEADS_PER_STEP * HEAD_DIM), BF16)])
        ob_t = _attn_call(_mla_kernel, qb_t, kb, vb_t,
                          _score_bound(mla_q_norm_g[l], mla_k_norm_g[l], MLA_QK_DIM),
                          seq=s, qk_rows=MLA_QK_PAD, name="mla_attn")
        h = _out_ffn_call(h, oa_t, ob_t, w_o[l].astype(BF16), ffn_norm_g[l].reshape(1, d),
                          w_gate[l].astype(BF16), w_up[l].astype(BF16), w_down[l].astype(BF16),
                          tm=FFN_TILE)
    return h.reshape(b, s, d).astype(x.dtype)
```

```python
import functools
import math

import jax
import jax.numpy as jnp
from jax import lax
from jax.experimental import pallas as pl
from jax.experimental.pallas import tpu as pltpu

HEAD_DIM = 64
MOBA_HEADS = 8
MOBA_WIDTH = MOBA_HEADS * HEAD_DIM
MOBA_BLOCK = 256
MOBA_TOPK = 3
MLA_HEADS = 8
MLA_Q_RANK = 256
MLA_KV_RANK = 128
MLA_NOPE_DIM = 64
MLA_ROPE_DIM = 32
MLA_V_DIM = 64
MLA_QK_DIM = MLA_NOPE_DIM + MLA_ROPE_DIM
MLA_QK_PAD = 128
MLA_WIDTH = MLA_HEADS * MLA_V_DIM
ROPE_THETA = 10000.0
EPS = 1e-6

NEG_BIG = -1e30
ATTN_BLOCK = 256
PROJ_TILE = 512
FFN_TILE = 1024
HEADS_PER_STEP = 2
SCORE_LOOKAHEAD = 3
ONES_ROWS = 16
LOG2_E = math.log2(math.e)
MAX_SAFE_EXPONENT = 48.0
BF16_NORM_MARGIN = 1.02
V7X_VMEM_LIMIT = 56 * 1024 * 1024

F32 = jnp.float32
BF16 = jnp.bfloat16

_NT = (((1,), (1,)), ((), ()))


def _row_slices(sizes):
    out, lo = {}, 0
    for name, n in sizes:
        out[name] = slice(lo, lo + n)
        lo += n
    return out


_GAIN_ROWS = _row_slices([("cq", MLA_Q_RANK), ("ckv", MLA_KV_RANK),
                          ("q_b", MLA_QK_DIM), ("k_b", MLA_QK_DIM)])
_ROPE_ROWS = _row_slices([("q_a", 2 * HEAD_DIM), ("k_a", 2 * HEAD_DIM),
                          ("cos_b", MLA_ROPE_DIM // 2), ("sin_b", MLA_ROPE_DIM // 2)])


def _dot(a, b):
    return jnp.dot(a, b, preferred_element_type=F32)


def _gained_rope_table(g, cos, sin):
    half = cos.shape[0]
    g1, g2 = g[:half, None], g[half:, None]
    return jnp.concatenate([g1 * cos, g2 * sin, g2 * cos, g1 * sin], axis=0)


def _gained_rope_t(n, table):
    half = n.shape[0] // 2
    n1, n2 = n[:half], n[half:]
    t = [table[i * half:(i + 1) * half] for i in range(4)]
    return jnp.concatenate([n1 * t[0] - n2 * t[1], n2 * t[2] + n1 * t[3]], axis=0)


def _rope_t(x, cos, sin):
    half = x.shape[0] // 2
    x1, x2 = x[:half], x[half:]
    return jnp.concatenate([x1 * cos - x2 * sin, x2 * cos + x1 * sin], axis=0)


def _proj_kernel(x_ref, g_attn_ref, w_in_t_ref, w_qup_t_ref, w_kvup_t_ref,
                 gains_ref, rope_ref,
                 qa_t_ref, ka_ref, va_t_ref, qb_t_ref, kb_ref, vb_t_ref,
                 proj_even_ref, proj_odd_ref):
    t = pl.program_id(0)

    @pl.when(t == 0)
    def _():
        proj_odd_ref[...] = jnp.zeros_like(proj_odd_ref)

    def step(proj_new_ref, proj_done_ref):
        x = x_ref[...]
        hn = x * lax.rsqrt(jnp.mean(x * x, axis=-1, keepdims=True) + EPS) * g_attn_ref[...]
        hn = hn.astype(BF16)

        def project(lo, hi):
            proj_new_ref[lo:hi, :] = lax.dot_general(w_in_t_ref[lo:hi, :], hn, _NT,
                                                     preferred_element_type=F32)

        finish = _finish_stages(proj_done_ref, w_qup_t_ref, w_kvup_t_ref,
                                gains_ref, rope_ref,
                                qa_t_ref, ka_ref, va_t_ref, qb_t_ref, kb_ref, vb_t_ref)
        n_rows = w_in_t_ref.shape[0]
        bounds = [0, MOBA_WIDTH, 2 * MOBA_WIDTH, 3 * MOBA_WIDTH, n_rows]
        chunks = [functools.partial(project, lo, hi) for lo, hi in zip(bounds[:-1], bounds[1:])]
        for i in range(max(len(chunks), len(finish))):
            if i < len(finish):
                finish[i]()
            if i < len(chunks):
                chunks[i]()

    @pl.when(t % 2 == 0)
    def _():
        step(proj_even_ref, proj_odd_ref)

    @pl.when(t % 2 == 1)
    def _():
        step(proj_odd_ref, proj_even_ref)


def _finish_stages(proj_ref, w_qup_t_ref, w_kvup_t_ref,
                   gains_ref, rope_ref,
                   qa_t_ref, ka_ref, va_t_ref, qb_t_ref, kb_ref, vb_t_ref):
    tm = proj_ref.shape[1]
    o_k = MOBA_WIDTH
    o_v = 2 * MOBA_WIDTH
    o_cq = 3 * MOBA_WIDTH
    o_ckv = o_cq + MLA_Q_RANK
    o_pe = o_ckv + MLA_KV_RANK
    kv_w = MLA_NOPE_DIM + MLA_V_DIM
    up = {}
    gain = lambda name: gains_ref[_GAIN_ROWS[name], :]
    rope = lambda name: rope_ref[_ROPE_ROWS[name], :]

    def low_rank():
        cq = proj_ref[o_cq:o_ckv, :]
        cq = cq * lax.rsqrt(jnp.mean(cq * cq, axis=0, keepdims=True) + EPS) * gain("cq")
        up["q"] = _dot(w_qup_t_ref[...], cq.astype(BF16))
        ckv = proj_ref[o_ckv:o_pe, :]
        ckv = ckv * lax.rsqrt(jnp.mean(ckv * ckv, axis=0, keepdims=True) + EPS) * gain("ckv")
        up["kv"] = _dot(w_kvup_t_ref[...], ckv.astype(BF16))

    def moba_q():
        table = rope("q_a")
        for h in range(MOBA_HEADS):
            q = proj_ref[h * HEAD_DIM:(h + 1) * HEAD_DIM, :]
            q = q * lax.rsqrt(jnp.mean(q * q, axis=0, keepdims=True) + EPS)
            qa_t_ref[h * HEAD_DIM:(h + 1) * HEAD_DIM, :] = _gained_rope_t(q, table).astype(BF16)

    def moba_kv():
        table = rope("k_a")
        ka_parts = []
        for h in range(MOBA_HEADS):
            k = proj_ref[o_k + h * HEAD_DIM:o_k + (h + 1) * HEAD_DIM, :]
            k = k * lax.rsqrt(jnp.mean(k * k, axis=0, keepdims=True) + EPS)
            ka_parts.append(_gained_rope_t(k, table))
        ka_ref[...] = jnp.concatenate(ka_parts, axis=0).T.astype(BF16)
        va_t_ref[...] = proj_ref[o_v:o_cq, :].astype(BF16)

    def mla_q():
        cos_b, sin_b, gq_b = rope("cos_b"), rope("sin_b"), gain("q_b")
        pad = jnp.zeros((MLA_QK_PAD - MLA_QK_DIM, tm), F32)
        for h in range(MLA_HEADS):
            q = up["q"][h * MLA_QK_DIM:(h + 1) * MLA_QK_DIM]
            q = q * lax.rsqrt(jnp.mean(q * q, axis=0, keepdims=True) + EPS) * gq_b
            q = jnp.concatenate(
                [q[:MLA_NOPE_DIM], _rope_t(q[MLA_NOPE_DIM:], cos_b, sin_b), pad], axis=0)
            qb_t_ref[h * MLA_QK_PAD:(h + 1) * MLA_QK_PAD, :] = q.astype(BF16)

    def mla_kv():
        cos_b, sin_b, gk_b = rope("cos_b"), rope("sin_b"), gain("k_b")
        kv = up["kv"]
        k_pe = proj_ref[o_pe:o_pe + MLA_ROPE_DIM, :]
        pe_ss = jnp.sum(k_pe * k_pe, axis=0, keepdims=True)
        pe_rot = _rope_t(k_pe * gk_b[MLA_NOPE_DIM:], cos_b, sin_b)
        pad = jnp.zeros((MLA_QK_PAD - MLA_QK_DIM, tm), F32)
        kb_parts = []
        for h in range(MLA_HEADS):
            k_nope = kv[h * kv_w:h * kv_w + MLA_NOPE_DIM]
            ss = jnp.sum(k_nope * k_nope, axis=0, keepdims=True) + pe_ss
            r = lax.rsqrt(ss / MLA_QK_DIM + EPS)
            kb_parts += [k_nope * r * gk_b[:MLA_NOPE_DIM], pe_rot * r, pad]
            vb_t_ref[h * MLA_V_DIM:(h + 1) * MLA_V_DIM, :] = (
                kv[h * kv_w + MLA_NOPE_DIM:(h + 1) * kv_w].astype(BF16))
        kb_ref[...] = jnp.concatenate(kb_parts, axis=0).T.astype(BF16)

    return [low_rank, moba_q, moba_kv, mla_q, mla_kv]


def _proj_call(x2, g_attn, w_in_t, w_qup_t, w_kvup_t, gains, rope, *, seq, tm):
    t, d = x2.shape
    n_pos = seq // tm
    n_tiles = t // tm
    const = lambda i: (0, 0)
    x_row = lambda i: (jnp.minimum(i, n_tiles - 1), 0)
    row = lambda i: (jnp.maximum(i - 1, 0), 0)
    colb = lambda i: (0, jnp.maximum(i - 1, 0))
    pos = lambda i: (0, jnp.maximum(i - 1, 0) % n_pos)

    def full(a):
        return pl.BlockSpec(a.shape, const, pipeline_mode=pl.Buffered(1))

    out_shape = [
        jax.ShapeDtypeStruct((MOBA_WIDTH, t), BF16),
        jax.ShapeDtypeStruct((t, MOBA_WIDTH), BF16),
        jax.ShapeDtypeStruct((MOBA_WIDTH, t), BF16),
        jax.ShapeDtypeStruct((MLA_HEADS * MLA_QK_PAD, t), BF16),
        jax.ShapeDtypeStruct((t, MLA_HEADS * MLA_QK_PAD), BF16),
        jax.ShapeDtypeStruct((MLA_WIDTH, t), BF16),
    ]
    out_specs = [
        pl.BlockSpec((MOBA_WIDTH, tm), colb),
        pl.BlockSpec((tm, MOBA_WIDTH), row),
        pl.BlockSpec((MOBA_WIDTH, tm), colb),
        pl.BlockSpec((MLA_HEADS * MLA_QK_PAD, tm), colb),
        pl.BlockSpec((tm, MLA_HEADS * MLA_QK_PAD), row),
        pl.BlockSpec((MLA_WIDTH, tm), colb),
    ]
    in_specs = [
        pl.BlockSpec((tm, d), x_row), full(g_attn), full(w_in_t), full(w_qup_t), full(w_kvup_t),
        full(gains), pl.BlockSpec((rope.shape[0], tm), pos),
    ]
    return pl.pallas_call(
        _proj_kernel, out_shape=out_shape, grid=(n_tiles + 1,),
        in_specs=in_specs, out_specs=out_specs, name="proj_heads",
        scratch_shapes=[pltpu.VMEM((w_in_t.shape[0], tm), F32)] * 2,
        compiler_params=pltpu.CompilerParams(
            dimension_semantics=("arbitrary",), vmem_limit_bytes=V7X_VMEM_LIMIT),
    )(x2, g_attn, w_in_t, w_qup_t, w_kvup_t, gains, rope)


def _attend(s_past, s_diag, v_aug, bounded):
    key_i = lax.broadcasted_iota(jnp.int32, s_diag.shape, 0)
    qry_i = lax.broadcasted_iota(jnp.int32, s_diag.shape, 1)
    s_diag = jnp.where(key_i <= qry_i, s_diag, NEG_BIG)
    if bounded:
        prob = lambda s: jnp.exp2(s).astype(BF16)
    else:
        m = jnp.max(s_diag, axis=0, keepdims=True)
        if s_past is not None:
            m = jnp.maximum(m, jnp.max(s_past, axis=0, keepdims=True))
        prob = lambda s: jnp.exp2(s - m).astype(BF16)
    p = prob(s_diag)
    if s_past is not None:
        p = jnp.concatenate([prob(s_past), p], axis=0)
    r = _dot(v_aug, p)
    dv = v_aug.shape[0] - ONES_ROWS
    return r[:dv] / r[dv:dv + 1]


def _split3(a):
    hi = a.astype(BF16)
    r1 = a - hi.astype(F32)
    mid = r1.astype(BF16)
    lo = (r1 - mid.astype(F32)).astype(BF16)
    return hi, mid, lo


def _moba_kernel(q_t_ref, k_ref, v_t_ref, o_t_ref, k_ext_ref, *, bounded):
    seq = k_ref.shape[0]
    nb = seq // MOBA_BLOCK
    blk = MOBA_BLOCK
    n_sel = min(MOBA_TOPK, nb - 1)

    k_lanes = k_ref.shape[1]
    assert nb <= k_lanes
    k_ext_ref[:, 0:k_lanes] = k_ref[...]

    @pl.when((pl.program_id(0) == 0) & (pl.program_id(1) == 0))
    def _():
        blk_of_key = lax.broadcasted_iota(jnp.int32, (seq, k_lanes), 0) // blk
        lane_i = lax.broadcasted_iota(jnp.int32, (seq, k_lanes), 1)
        k_ext_ref[:, k_lanes:] = jnp.where(blk_of_key == lane_i, 1.0, 0.0).astype(BF16)

    ind_rows = 2 * nb
    in_block = (lax.broadcasted_iota(jnp.int32, (ind_rows, seq), 1) // blk
                == lax.broadcasted_iota(jnp.int32, (ind_rows, seq), 0))
    k_sum = _dot(jnp.where(in_block, 1.0, 0.0).astype(BF16), k_ref[...])
    k_mean = k_sum[:nb] * (1.0 / blk)
    km_stack = jnp.concatenate([p.astype(F32) for p in _split3(k_mean)]
                               + [jnp.zeros_like(k_mean)], axis=0).astype(BF16)
    row_i = lax.broadcasted_iota(jnp.int32, (nb, seq), 0)
    blk_of_qry = lax.broadcasted_iota(jnp.int32, (nb, seq), 1) // blk
    zeros = jnp.zeros((HEAD_DIM, seq), BF16)
    bias_pad = jnp.zeros((k_lanes - nb, seq), F32)
    ones = jnp.ones((ONES_ROWS, seq), BF16)

    v_aug = [jnp.concatenate([v_t_ref[hh * HEAD_DIM:(hh + 1) * HEAD_DIM, :], ones], axis=0)
             for hh in range(HEADS_PER_STEP)]

    q_w = []
    for hh in range(HEADS_PER_STEP):
        q_h = q_t_ref[hh * HEAD_DIM:(hh + 1) * HEAD_DIM, :]
        q_pad = jnp.concatenate([q_h, zeros] if hh == 0 else [zeros, q_h], axis=0)
        parts = _dot(km_stack, q_pad)
        gate = parts[:nb] + parts[nb:2 * nb] + parts[2 * nb:3 * nb]
        past = row_i < blk_of_qry
        gate = jnp.where(past, gate, -jnp.inf)
        rank = jnp.zeros((nb, seq), jnp.int32)
        for jp in range(nb - 1):
            g_jp = gate[jp:jp + 1, :]
            beats = (g_jp > gate) | ((g_jp == gate) & (jp < row_i))
            rank = rank + beats.astype(jnp.int32)
        bias = jnp.where(past & (rank >= n_sel), NEG_BIG, 0.0)
        q_w.append(jnp.concatenate(
            [q_pad, jnp.concatenate([bias, bias_pad], axis=0).astype(BF16)], axis=0))

    def scores(hh, i):
        s = _dot(k_ext_ref[0:(i + 1) * blk, :], q_w[hh][:, i * blk:(i + 1) * blk])
        return (s[:i * blk] if i > 0 else None), s[i * blk:]

    def finish(hh, i, s_past, s_diag):
        o = _attend(s_past, s_diag, v_aug[hh][:, :(i + 1) * blk], bounded)
        o_t_ref[hh * HEAD_DIM:(hh + 1) * HEAD_DIM, i * blk:(i + 1) * blk] = o.astype(o_t_ref.dtype)

    _pipelined_units(nb, scores, finish)


def _pipelined_units(n_qblk, scores, finish):
    units = [(hh, i) for hh in range(HEADS_PER_STEP)
             for i in (range(n_qblk) if hh % 2 == 0 else reversed(range(n_qblk)))]
    pending = [scores(*unit) for unit in units[:SCORE_LOOKAHEAD]]
    for u, unit in enumerate(units):
        if u + SCORE_LOOKAHEAD < len(units):
            pending.append(scores(*units[u + SCORE_LOOKAHEAD]))
        finish(*unit, *pending.pop(0))


def _mla_kernel(q_t_ref, k_ref, v_t_ref, o_t_ref, *, bounded):
    seq = k_ref.shape[0]
    blk = ATTN_BLOCK
    ones = jnp.ones((ONES_ROWS, seq), BF16)
    v_aug = [jnp.concatenate([v_t_ref[hh * MLA_V_DIM:(hh + 1) * MLA_V_DIM, :], ones], axis=0)
             for hh in range(HEADS_PER_STEP)]

    def scores(hh, i):
        q_rows = slice(hh * MLA_QK_PAD, (hh + 1) * MLA_QK_PAD)
        s = _dot(k_ref[0:(i + 1) * blk, q_rows], q_t_ref[q_rows, i * blk:(i + 1) * blk])
        return (s[:i * blk] if i > 0 else None), s[i * blk:]

    def finish(hh, i, s_past, s_diag):
        o = _attend(s_past, s_diag, v_aug[hh][:, :(i + 1) * blk], bounded)
        o_t_ref[hh * MLA_V_DIM:(hh + 1) * MLA_V_DIM, i * blk:(i + 1) * blk] = o.astype(o_t_ref.dtype)

    _pipelined_units(seq // blk, scores, finish)


def _with_casts(kernel, n_casts, q_t_ref, k_ref, v_t_ref, *refs, **kwargs):
    casts_in, o_t_ref, casts_out = refs[:n_casts], refs[n_casts], refs[n_casts + 1:2 * n_casts + 1]
    for src_ref, dst_ref in zip(casts_in, casts_out):
        dst_ref[...] = src_ref[...].astype(dst_ref.dtype)
    kernel(q_t_ref, k_ref, v_t_ref, o_t_ref, *refs[2 * n_casts + 1:], **kwargs)


def _cast_block_rows(rows, n_steps):
    tile = 16
    per_step = -(-rows // (n_steps * tile)) * tile
    while rows % per_step:
        per_step += tile
    return per_step


def _attn_call(kernel, q_t, k, v_t, score_bound, *, seq, qk_rows, name, scratch_shapes=(),
               casts=()):
    t = k.shape[0]
    n_batch = t // seq
    n_steps = q_t.shape[0] // (HEADS_PER_STEP * qk_rows)
    k_lanes = k.shape[1] // n_steps
    v_rows = v_t.shape[0] // n_steps

    cast_specs = []
    for w in casts:
        rows = _cast_block_rows(w.shape[0], n_batch * n_steps)
        last = w.shape[0] // rows - 1
        cast_specs.append(pl.BlockSpec(
            (rows, w.shape[1]), lambda b, p, last=last: (jnp.minimum(b * n_steps + p, last), 0)))

    def call(bounded, *operands):
        return pl.pallas_call(
            functools.partial(_with_casts, kernel, len(casts), bounded=bounded),
            out_shape=[jax.ShapeDtypeStruct(v_t.shape, BF16)]
            + [jax.ShapeDtypeStruct(w.shape, BF16) for w in casts],
            grid=(n_batch, n_steps),
            in_specs=[
                pl.BlockSpec((HEADS_PER_STEP * qk_rows, seq), lambda b, p: (p, b)),
                pl.BlockSpec((seq, k_lanes), lambda b, p: (b, p)),
                pl.BlockSpec((v_rows, seq), lambda b, p: (p, b)),
            ] + cast_specs,
            out_specs=[pl.BlockSpec((v_rows, seq), lambda b, p: (p, b))] + cast_specs,
            scratch_shapes=list(scratch_shapes),
            name=name + ("_bounded" if bounded else "_shifted"),
            compiler_params=pltpu.CompilerParams(
                dimension_semantics=("arbitrary", "arbitrary") if scratch_shapes or casts
                else ("parallel", "parallel"), vmem_limit_bytes=V7X_VMEM_LIMIT),
        )(*operands)

    return lax.cond(score_bound <= MAX_SAFE_EXPONENT,
                    functools.partial(call, True), functools.partial(call, False),
                    q_t, k, v_t, *casts)


def _score_bound(g_q, g_k, dim):
    return (jnp.max(jnp.abs(g_q.astype(F32))) * jnp.max(jnp.abs(g_k.astype(F32)))
            * (dim ** 0.5 * LOG2_E * BF16_NORM_MARGIN))


def _out_ffn_kernel(x_ref, oa_t_ref, ob_t_ref, w_o_ref, g_ffn_ref, w_gate_ref, w_up_ref,
                    w_down_ref, out_ref):
    o_t = jnp.concatenate([oa_t_ref[...], ob_t_ref[...]], axis=0).astype(F32)
    o = o_t.T.astype(BF16)
    h = x_ref[...] + _dot(o, w_o_ref[...])
    gn = h * lax.rsqrt(jnp.mean(h * h, axis=-1, keepdims=True) + EPS) * g_ffn_ref[...]
    gn = gn.astype(BF16)
    gate = _dot(gn, w_gate_ref[...])
    up = _dot(gn, w_up_ref[...])
    act = (gate * jax.nn.sigmoid(gate) * up).astype(BF16)
    out_ref[...] = h + _dot(act, w_down_ref[...])


def _out_ffn_call(x2, oa_t, ob_t, w_o, g_ffn, w_gate, w_up, w_down, *, tm):
    t, d = x2.shape
    const = lambda i: (0, 0)

    def resident(a):
        return pl.BlockSpec(a.shape, const, pipeline_mode=pl.Buffered(1))

    return pl.pallas_call(
        _out_ffn_kernel,
        out_shape=jax.ShapeDtypeStruct((t, d), x2.dtype),
        grid=(t // tm,),
        in_specs=[
            pl.BlockSpec((tm, d), lambda i: (i, 0)),
            pl.BlockSpec((oa_t.shape[0], tm), lambda i: (0, i)),
            pl.BlockSpec((ob_t.shape[0], tm), lambda i: (0, i)),
            resident(w_o), resident(g_ffn), resident(w_gate), resident(w_up), resident(w_down),
        ],
        out_specs=pl.BlockSpec((tm, d), lambda i: (i, 0)),
        name="out_ffn",
        compiler_params=pltpu.CompilerParams(
            dimension_semantics=("parallel",), vmem_limit_bytes=V7X_VMEM_LIMIT),
    )(x2, oa_t, ob_t, w_o, g_ffn, w_gate, w_up, w_down)


def _rope_tables_t(seq, dim):
    inv = ROPE_THETA ** (-jnp.arange(0, dim, 2, dtype=F32) / dim)
    ang = jnp.arange(seq, dtype=F32)[:, None] * inv[None, :]
    return jnp.cos(ang).T, jnp.sin(ang).T


def kernel(x, attn_norm_g, w_in, moba_q_norm_g, moba_k_norm_g, mla_q_a_norm_g, w_q_up,
           mla_kv_a_norm_g, w_kv_up, mla_q_norm_g, mla_k_norm_g, w_o, ffn_norm_g,
           w_gate, w_up, w_down):
    b, s, d = x.shape
    assert s % MOBA_BLOCK == 0 and s % ATTN_BLOCK == 0
    assert s % PROJ_TILE == 0 and (b * s) % FFN_TILE == 0
    cos_a, sin_a = _rope_tables_t(s, HEAD_DIM)
    cos_b, sin_b = _rope_tables_t(s, MLA_ROPE_DIM)
    h = x.reshape(b * s, d)
    for l in range(w_in.shape[0]):
        gains = jnp.concatenate([
            mla_q_a_norm_g[l].astype(F32), mla_kv_a_norm_g[l].astype(F32),
            mla_q_norm_g[l].astype(F32) * (MLA_QK_DIM ** -0.5 * LOG2_E),
            mla_k_norm_g[l].astype(F32)]).reshape(-1, 1)
        rope = jnp.concatenate([
            _gained_rope_table(moba_q_norm_g[l].astype(F32) * (HEAD_DIM ** -0.5 * LOG2_E),
                               cos_a, sin_a),
            _gained_rope_table(moba_k_norm_g[l].astype(F32), cos_a, sin_a), cos_b, sin_b], axis=0)
        qa_t, ka, va_t, qb_t, kb, vb_t = _proj_call(
            h, attn_norm_g[l].reshape(1, d), w_in[l].T, w_q_up[l].T.astype(BF16),
            w_kv_up[l].T.astype(BF16), gains, rope, seq=s, tm=PROJ_TILE)
        (oa_t,) = _attn_call(
            _moba_kernel, qa_t, ka, va_t,
            _score_bound(moba_q_norm_g[l], moba_k_norm_g[l], HEAD_DIM),
            seq=s, qk_rows=HEAD_DIM, name="moba_attn",
            scratch_shapes=[pltpu.VMEM((s, 2 * HEADS_PER_STEP * HEAD_DIM), BF16)])
        ob_t, w_o_b, w_gate_b, w_up_b, w_down_b = _attn_call(
            _mla_kernel, qb_t, kb, vb_t,
            _score_bound(mla_q_norm_g[l], mla_k_norm_g[l], MLA_QK_DIM),
            seq=s, qk_rows=MLA_QK_PAD, name="mla_attn",
            casts=(w_o[l], w_gate[l], w_up[l], w_down[l]))
        h = _out_ffn_call(h, oa_t, ob_t, w_o_b, ffn_norm_g[l].reshape(1, d),
                          w_gate_b, w_up_b, w_down_b, tm=FFN_TILE)
    return h.reshape(b, s, d).astype(x.dtype)
```

```python
import functools
import math

import jax
import jax.numpy as jnp
from jax import lax
from jax.experimental import pallas as pl
from jax.experimental.pallas import tpu as pltpu

HEAD_DIM = 64
MOBA_HEADS = 8
MOBA_WIDTH = MOBA_HEADS * HEAD_DIM
MOBA_BLOCK = 256
MOBA_TOPK = 3
MLA_HEADS = 8
MLA_Q_RANK = 256
MLA_KV_RANK = 128
MLA_NOPE_DIM = 64
MLA_ROPE_DIM = 32
MLA_V_DIM = 64
MLA_QK_DIM = MLA_NOPE_DIM + MLA_ROPE_DIM
MLA_QK_PAD = 128
MLA_WIDTH = MLA_HEADS * MLA_V_DIM
ROPE_THETA = 10000.0
EPS = 1e-6

NEG_BIG = -1e30
ATTN_BLOCK = 256
PROJ_TILE = 1024
FFN_TILE = 1024
HEADS_PER_STEP = 2
SCORE_LOOKAHEAD = 3
ONES_ROWS = 16
LOG2_E = math.log2(math.e)
MAX_SAFE_EXPONENT = 48.0
BF16_NORM_MARGIN = 1.02
V7X_VMEM_LIMIT = 56 * 1024 * 1024

F32 = jnp.float32
BF16 = jnp.bfloat16

_NT = (((1,), (1,)), ((), ()))


def _row_slices(sizes):
    out, lo = {}, 0
    for name, n in sizes:
        out[name] = slice(lo, lo + n)
        lo += n
    return out


_GAIN_ROWS = _row_slices([("cq", MLA_Q_RANK), ("ckv", MLA_KV_RANK),
                          ("q_b", MLA_QK_DIM), ("k_b", MLA_QK_DIM)])
_ROPE_ROWS = _row_slices([("q_a", 2 * HEAD_DIM), ("k_a", 2 * HEAD_DIM),
                          ("cos_b", MLA_ROPE_DIM // 2), ("sin_b", MLA_ROPE_DIM // 2)])


def _dot(a, b):
    return jnp.dot(a, b, preferred_element_type=F32)


def _gained_rope_table(g, cos, sin):
    half = cos.shape[0]
    g1, g2 = g[:half, None], g[half:, None]
    return jnp.concatenate([g1 * cos, g2 * sin, g2 * cos, g1 * sin], axis=0)


def _gained_rope_t(n, table):
    half = n.shape[0] // 2
    n1, n2 = n[:half], n[half:]
    t = [table[i * half:(i + 1) * half] for i in range(4)]
    return jnp.concatenate([n1 * t[0] - n2 * t[1], n2 * t[2] + n1 * t[3]], axis=0)


def _rope_t(x, cos, sin):
    half = x.shape[0] // 2
    x1, x2 = x[:half], x[half:]
    return jnp.concatenate([x1 * cos - x2 * sin, x2 * cos + x1 * sin], axis=0)


def _proj_kernel(x_ref, g_attn_ref, w_in_t_ref, w_qup_t_ref, w_kvup_t_ref,
                 gains_ref, rope_ref,
                 qa_t_ref, ka_ref, va_t_ref, qb_t_ref, kb_ref, vb_t_ref,
                 proj_even_ref, proj_odd_ref):
    t = pl.program_id(0)

    @pl.when(t == 0)
    def _():
        proj_odd_ref[...] = jnp.zeros_like(proj_odd_ref)

    def step(proj_new_ref, proj_done_ref):
        x = x_ref[...]
        hn = x * lax.rsqrt(jnp.mean(x * x, axis=-1, keepdims=True) + EPS) * g_attn_ref[...]
        hn = hn.astype(BF16)

        def project(lo, hi):
            proj_new_ref[lo:hi, :] = lax.dot_general(w_in_t_ref[lo:hi, :], hn, _NT,
                                                     preferred_element_type=F32)

        finish = _finish_stages(proj_done_ref, w_qup_t_ref, w_kvup_t_ref,
                                gains_ref, rope_ref,
                                qa_t_ref, ka_ref, va_t_ref, qb_t_ref, kb_ref, vb_t_ref)
        n_rows = w_in_t_ref.shape[0]
        bounds = [0, MOBA_WIDTH, 2 * MOBA_WIDTH, 3 * MOBA_WIDTH, n_rows]
        chunks = [functools.partial(project, lo, hi) for lo, hi in zip(bounds[:-1], bounds[1:])]
        for i in range(max(len(chunks), len(finish))):
            if i < len(finish):
                finish[i]()
            if i < len(chunks):
                chunks[i]()

    @pl.when(t % 2 == 0)
    def _():
        step(proj_even_ref, proj_odd_ref)

    @pl.when(t % 2 == 1)
    def _():
        step(proj_odd_ref, proj_even_ref)


def _finish_stages(proj_ref, w_qup_t_ref, w_kvup_t_ref,
                   gains_ref, rope_ref,
                   qa_t_ref, ka_ref, va_t_ref, qb_t_ref, kb_ref, vb_t_ref):
    tm = proj_ref.shape[1]
    o_k = MOBA_WIDTH
    o_v = 2 * MOBA_WIDTH
    o_cq = 3 * MOBA_WIDTH
    o_ckv = o_cq + MLA_Q_RANK
    o_pe = o_ckv + MLA_KV_RANK
    kv_w = MLA_NOPE_DIM + MLA_V_DIM
    up = {}
    gain = lambda name: gains_ref[_GAIN_ROWS[name], :]
    rope = lambda name: rope_ref[_ROPE_ROWS[name], :]

    def low_rank():
        cq = proj_ref[o_cq:o_ckv, :]
        cq = cq * lax.rsqrt(jnp.mean(cq * cq, axis=0, keepdims=True) + EPS) * gain("cq")
        up["q"] = _dot(w_qup_t_ref[...], cq.astype(BF16))
        ckv = proj_ref[o_ckv:o_pe, :]
        ckv = ckv * lax.rsqrt(jnp.mean(ckv * ckv, axis=0, keepdims=True) + EPS) * gain("ckv")
        up["kv"] = _dot(w_kvup_t_ref[...], ckv.astype(BF16))

    def moba_q():
        table = rope("q_a")
        for h in range(MOBA_HEADS):
            q = proj_ref[h * HEAD_DIM:(h + 1) * HEAD_DIM, :]
            q = q * lax.rsqrt(jnp.mean(q * q, axis=0, keepdims=True) + EPS)
            qa_t_ref[h * HEAD_DIM:(h + 1) * HEAD_DIM, :] = _gained_rope_t(q, table).astype(BF16)

    def moba_kv():
        table = rope("k_a")
        ka_parts = []
        for h in range(MOBA_HEADS):
            k = proj_ref[o_k + h * HEAD_DIM:o_k + (h + 1) * HEAD_DIM, :]
            k = k * lax.rsqrt(jnp.mean(k * k, axis=0, keepdims=True) + EPS)
            ka_parts.append(_gained_rope_t(k, table))
        ka_ref[...] = jnp.concatenate(ka_parts, axis=0).T.astype(BF16)
        va_t_ref[...] = proj_ref[o_v:o_cq, :].astype(BF16)

    def mla_q():
        cos_b, sin_b, gq_b = rope("cos_b"), rope("sin_b"), gain("q_b")
        pad = jnp.zeros((MLA_QK_PAD - MLA_QK_DIM, tm), F32)
        for h in range(MLA_HEADS):
            q = up["q"][h * MLA_QK_DIM:(h + 1) * MLA_QK_DIM]
            q = q * lax.rsqrt(jnp.mean(q * q, axis=0, keepdims=True) + EPS) * gq_b
            q = jnp.concatenate(
                [q[:MLA_NOPE_DIM], _rope_t(q[MLA_NOPE_DIM:], cos_b, sin_b), pad], axis=0)
            qb_t_ref[h * MLA_QK_PAD:(h + 1) * MLA_QK_PAD, :] = q.astype(BF16)

    def mla_kv():
        cos_b, sin_b, gk_b = rope("cos_b"), rope("sin_b"), gain("k_b")
        kv = up["kv"]
        k_pe = proj_ref[o_pe:o_pe + MLA_ROPE_DIM, :]
        pe_ss = jnp.sum(k_pe * k_pe, axis=0, keepdims=True)
        pe_rot = _rope_t(k_pe * gk_b[MLA_NOPE_DIM:], cos_b, sin_b)
        pad = jnp.zeros((MLA_QK_PAD - MLA_QK_DIM, tm), F32)
        kb_parts = []
        for h in range(MLA_HEADS):
            k_nope = kv[h * kv_w:h * kv_w + MLA_NOPE_DIM]
            ss = jnp.sum(k_nope * k_nope, axis=0, keepdims=True) + pe_ss
            r = lax.rsqrt(ss / MLA_QK_DIM + EPS)
            kb_parts += [k_nope * r * gk_b[:MLA_NOPE_DIM], pe_rot * r, pad]
            vb_t_ref[h * MLA_V_DIM:(h + 1) * MLA_V_DIM, :] = (
                kv[h * kv_w + MLA_NOPE_DIM:(h + 1) * kv_w].astype(BF16))
        kb_ref[...] = jnp.concatenate(kb_parts, axis=0).T.astype(BF16)

    return [low_rank, moba_q, moba_kv, mla_q, mla_kv]


def _proj_call(x2, g_attn, w_in_t, w_qup_t, w_kvup_t, gains, rope, *, seq, tm):
    t, d = x2.shape
    n_pos = seq // tm
    n_tiles = t // tm
    const = lambda i: (0, 0)
    x_row = lambda i: (jnp.minimum(i, n_tiles - 1), 0)
    row = lambda i: (jnp.maximum(i - 1, 0), 0)
    colb = lambda i: (0, jnp.maximum(i - 1, 0))
    pos = lambda i: (0, jnp.maximum(i - 1, 0) % n_pos)

    def full(a):
        return pl.BlockSpec(a.shape, const, pipeline_mode=pl.Buffered(1))

    out_shape = [
        jax.ShapeDtypeStruct((MOBA_WIDTH, t), BF16),
        jax.ShapeDtypeStruct((t, MOBA_WIDTH), BF16),
        jax.ShapeDtypeStruct((MOBA_WIDTH, t), BF16),
        jax.ShapeDtypeStruct((MLA_HEADS * MLA_QK_PAD, t), BF16),
        jax.ShapeDtypeStruct((t, MLA_HEADS * MLA_QK_PAD), BF16),
        jax.ShapeDtypeStruct((MLA_WIDTH, t), BF16),
    ]
    out_specs = [
        pl.BlockSpec((MOBA_WIDTH, tm), colb),
        pl.BlockSpec((tm, MOBA_WIDTH), row),
        pl.BlockSpec((MOBA_WIDTH, tm), colb),
        pl.BlockSpec((MLA_HEADS * MLA_QK_PAD, tm), colb),
        pl.BlockSpec((tm, MLA_HEADS * MLA_QK_PAD), row),
        pl.BlockSpec((MLA_WIDTH, tm), colb),
    ]
    in_specs = [
        pl.BlockSpec((tm, d), x_row), full(g_attn), full(w_in_t), full(w_qup_t), full(w_kvup_t),
        full(gains), pl.BlockSpec((rope.shape[0], tm), pos),
    ]
    return pl.pallas_call(
        _proj_kernel, out_shape=out_shape, grid=(n_tiles + 1,),
        in_specs=in_specs, out_specs=out_specs, name="proj_heads",
        scratch_shapes=[pltpu.VMEM((w_in_t.shape[0], tm), F32)] * 2,
        compiler_params=pltpu.CompilerParams(
            dimension_semantics=("arbitrary",), vmem_limit_bytes=V7X_VMEM_LIMIT),
    )(x2, g_attn, w_in_t, w_qup_t, w_kvup_t, gains, rope)


def _attend(s_past, s_diag, v_aug, bounded):
    key_i = lax.broadcasted_iota(jnp.int32, s_diag.shape, 0)
    qry_i = lax.broadcasted_iota(jnp.int32, s_diag.shape, 1)
    s_diag = jnp.where(key_i <= qry_i, s_diag, NEG_BIG)
    if bounded:
        prob = lambda s: jnp.exp2(s).astype(BF16)
    else:
        m = jnp.max(s_diag, axis=0, keepdims=True)
        if s_past is not None:
            m = jnp.maximum(m, jnp.max(s_past, axis=0, keepdims=True))
        prob = lambda s: jnp.exp2(s - m).astype(BF16)
    p = prob(s_diag)
    if s_past is not None:
        p = jnp.concatenate([prob(s_past), p], axis=0)
    r = _dot(v_aug, p)
    dv = v_aug.shape[0] - ONES_ROWS
    return r[:dv] / r[dv:dv + 1]


def _split3(a):
    hi = a.astype(BF16)
    r1 = a - hi.astype(F32)
    mid = r1.astype(BF16)
    lo = (r1 - mid.astype(F32)).astype(BF16)
    return hi, mid, lo


def _moba_kernel(q_t_ref, k_ref, v_t_ref, o_t_ref, k_ext_ref, *, bounded):
    seq = k_ref.shape[0]
    nb = seq // MOBA_BLOCK
    blk = MOBA_BLOCK
    n_sel = min(MOBA_TOPK, nb - 1)

    k_lanes = k_ref.shape[1]
    assert nb <= k_lanes
    k_ext_ref[:, 0:k_lanes] = k_ref[...]

    @pl.when((pl.program_id(0) == 0) & (pl.program_id(1) == 0))
    def _():
        blk_of_key = lax.broadcasted_iota(jnp.int32, (seq, k_lanes), 0) // blk
        lane_i = lax.broadcasted_iota(jnp.int32, (seq, k_lanes), 1)
        k_ext_ref[:, k_lanes:] = jnp.where(blk_of_key == lane_i, 1.0, 0.0).astype(BF16)

    ind_rows = 2 * nb
    in_block = (lax.broadcasted_iota(jnp.int32, (ind_rows, seq), 1) // blk
                == lax.broadcasted_iota(jnp.int32, (ind_rows, seq), 0))
    k_sum = _dot(jnp.where(in_block, 1.0, 0.0).astype(BF16), k_ref[...])
    k_mean = k_sum[:nb] * (1.0 / blk)
    km_stack = jnp.concatenate([p.astype(F32) for p in _split3(k_mean)]
                               + [jnp.zeros_like(k_mean)], axis=0).astype(BF16)
    row_i = lax.broadcasted_iota(jnp.int32, (nb, seq), 0)
    blk_of_qry = lax.broadcasted_iota(jnp.int32, (nb, seq), 1) // blk
    zeros = jnp.zeros((HEAD_DIM, seq), BF16)
    bias_pad = jnp.zeros((k_lanes - nb, seq), F32)
    ones = jnp.ones((ONES_ROWS, seq), BF16)

    v_aug = [jnp.concatenate([v_t_ref[hh * HEAD_DIM:(hh + 1) * HEAD_DIM, :], ones], axis=0)
             for hh in range(HEADS_PER_STEP)]

    q_w = []
    for hh in range(HEADS_PER_STEP):
        q_h = q_t_ref[hh * HEAD_DIM:(hh + 1) * HEAD_DIM, :]
        q_pad = jnp.concatenate([q_h, zeros] if hh == 0 else [zeros, q_h], axis=0)
        parts = _dot(km_stack, q_pad)
        gate = parts[:nb] + parts[nb:2 * nb] + parts[2 * nb:3 * nb]
        past = row_i < blk_of_qry
        gate = jnp.where(past, gate, -jnp.inf)
        rank = jnp.zeros((nb, seq), jnp.int32)
        for jp in range(nb - 1):
            g_jp = gate[jp:jp + 1, :]
            beats = (g_jp > gate) | ((g_jp == gate) & (jp < row_i))
            rank = rank + beats.astype(jnp.int32)
        bias = jnp.where(past & (rank >= n_sel), NEG_BIG, 0.0)
        q_w.append(jnp.concatenate(
            [q_pad, jnp.concatenate([bias, bias_pad], axis=0).astype(BF16)], axis=0))

    def scores(hh, i):
        s = _dot(k_ext_ref[0:(i + 1) * blk, :], q_w[hh][:, i * blk:(i + 1) * blk])
        return (s[:i * blk] if i > 0 else None), s[i * blk:]

    def finish(hh, i, s_past, s_diag):
        o = _attend(s_past, s_diag, v_aug[hh][:, :(i + 1) * blk], bounded)
        o_t_ref[hh * HEAD_DIM:(hh + 1) * HEAD_DIM, i * blk:(i + 1) * blk] = o.astype(o_t_ref.dtype)

    _pipelined_units(nb, scores, finish)


def _pipelined_units(n_qblk, scores, finish):
    units = [(hh, i) for hh in range(HEADS_PER_STEP)
             for i in (range(n_qblk) if hh % 2 == 0 else reversed(range(n_qblk)))]
    pending = [scores(*unit) for unit in units[:SCORE_LOOKAHEAD]]
    for u, unit in enumerate(units):
        if u + SCORE_LOOKAHEAD < len(units):
            pending.append(scores(*units[u + SCORE_LOOKAHEAD]))
        finish(*unit, *pending.pop(0))


def _mla_kernel(q_t_ref, k_ref, v_t_ref, o_t_ref, *, bounded):
    seq = k_ref.shape[0]
    blk = ATTN_BLOCK
    ones = jnp.ones((ONES_ROWS, seq), BF16)
    v_aug = [jnp.concatenate([v_t_ref[hh * MLA_V_DIM:(hh + 1) * MLA_V_DIM, :], ones], axis=0)
             for hh in range(HEADS_PER_STEP)]

    def scores(hh, i):
        q_rows = slice(hh * MLA_QK_PAD, (hh + 1) * MLA_QK_PAD)
        s = _dot(k_ref[0:(i + 1) * blk, q_rows], q_t_ref[q_rows, i * blk:(i + 1) * blk])
        return (s[:i * blk] if i > 0 else None), s[i * blk:]

    def finish(hh, i, s_past, s_diag):
        o = _attend(s_past, s_diag, v_aug[hh][:, :(i + 1) * blk], bounded)
        o_t_ref[hh * MLA_V_DIM:(hh + 1) * MLA_V_DIM, i * blk:(i + 1) * blk] = o.astype(o_t_ref.dtype)

    _pipelined_units(seq // blk, scores, finish)


def _with_casts(kernel, n_casts, q_t_ref, k_ref, v_t_ref, *refs, **kwargs):
    casts_in, o_t_ref, casts_out = refs[:n_casts], refs[n_casts], refs[n_casts + 1:2 * n_casts + 1]
    for src_ref, dst_ref in zip(casts_in, casts_out):
        dst_ref[...] = src_ref[...].astype(dst_ref.dtype)
    kernel(q_t_ref, k_ref, v_t_ref, o_t_ref, *refs[2 * n_casts + 1:], **kwargs)


def _cast_block_rows(rows, n_steps):
    tile = 16
    per_step = -(-rows // (n_steps * tile)) * tile
    while rows % per_step:
        per_step += tile
    return per_step


def _attn_call(kernel, q_t, k, v_t, score_bound, *, seq, qk_rows, name, scratch_shapes=(),
               casts=()):
    t = k.shape[0]
    n_batch = t // seq
    n_steps = q_t.shape[0] // (HEADS_PER_STEP * qk_rows)
    k_lanes = k.shape[1] // n_steps
    v_rows = v_t.shape[0] // n_steps

    cast_specs = []
    for w in casts:
        rows = _cast_block_rows(w.shape[0], n_batch * n_steps)
        last = w.shape[0] // rows - 1
        cast_specs.append(pl.BlockSpec(
            (rows, w.shape[1]), lambda b, p, last=last: (jnp.minimum(b * n_steps + p, last), 0)))

    def call(bounded, *operands):
        return pl.pallas_call(
            functools.partial(_with_casts, kernel, len(casts), bounded=bounded),
            out_shape=[jax.ShapeDtypeStruct(v_t.shape, BF16)]
            + [jax.ShapeDtypeStruct(w.shape, BF16) for w in casts],
            grid=(n_batch, n_steps),
            in_specs=[
                pl.BlockSpec((HEADS_PER_STEP * qk_rows, seq), lambda b, p: (p, b)),
                pl.BlockSpec((seq, k_lanes), lambda b, p: (b, p)),
                pl.BlockSpec((v_rows, seq), lambda b, p: (p, b)),
            ] + cast_specs,
            out_specs=[pl.BlockSpec((v_rows, seq), lambda b, p: (p, b))] + cast_specs,
            scratch_shapes=list(scratch_shapes),
            name=name + ("_bounded" if bounded else "_shifted"),
            compiler_params=pltpu.CompilerParams(
                dimension_semantics=("arbitrary", "arbitrary") if scratch_shapes or casts
                else ("parallel", "parallel"), vmem_limit_bytes=V7X_VMEM_LIMIT),
        )(*operands)

    return lax.cond(score_bound <= MAX_SAFE_EXPONENT,
                    functools.partial(call, True), functools.partial(call, False),
                    q_t, k, v_t, *casts)


def _score_bounds(gains_and_dims):
    vecs = [g.astype(F32) for g_q, g_k, _ in gains_and_dims for g in (g_q, g_k)]
    width = max(v.shape[0] for v in vecs)
    peak = jnp.max(jnp.stack([jnp.pad(jnp.abs(v), (0, width - v.shape[0])) for v in vecs]),
                   axis=1)
    return [peak[2 * i] * peak[2 * i + 1] * (dim ** 0.5 * LOG2_E * BF16_NORM_MARGIN)
            for i, (_, _, dim) in enumerate(gains_and_dims)]


def _out_ffn_kernel(x_ref, oa_t_ref, ob_t_ref, w_o_ref, g_ffn_ref, w_gate_ref, w_up_ref,
                    w_down_ref, out_ref):
    o_t = jnp.concatenate([oa_t_ref[...], ob_t_ref[...]], axis=0).astype(F32)
    o = o_t.T.astype(BF16)
    h = x_ref[...] + _dot(o, w_o_ref[...])
    gn = h * lax.rsqrt(jnp.mean(h * h, axis=-1, keepdims=True) + EPS) * g_ffn_ref[...]
    gn = gn.astype(BF16)
    gate = _dot(gn, w_gate_ref[...])
    up = _dot(gn, w_up_ref[...])
    act = (gate * jax.nn.sigmoid(gate) * up).astype(BF16)
    out_ref[...] = h + _dot(act, w_down_ref[...])


def _out_ffn_call(x2, oa_t, ob_t, w_o, g_ffn, w_gate, w_up, w_down, *, tm):
    t, d = x2.shape
    const = lambda i: (0, 0)

    def resident(a):
        return pl.BlockSpec(a.shape, const, pipeline_mode=pl.Buffered(1))

    return pl.pallas_call(
        _out_ffn_kernel,
        out_shape=jax.ShapeDtypeStruct((t, d), x2.dtype),
        grid=(t // tm,),
        in_specs=[
            pl.BlockSpec((tm, d), lambda i: (i, 0)),
            pl.BlockSpec((oa_t.shape[0], tm), lambda i: (0, i)),
            pl.BlockSpec((ob_t.shape[0], tm), lambda i: (0, i)),
            resident(w_o), resident(g_ffn), resident(w_gate), resident(w_up), resident(w_down),
        ],
        out_specs=pl.BlockSpec((tm, d), lambda i: (i, 0)),
        name="out_ffn",
        compiler_params=pltpu.CompilerParams(
            dimension_semantics=("parallel",), vmem_limit_bytes=V7X_VMEM_LIMIT),
    )(x2, oa_t, ob_t, w_o, g_ffn, w_gate, w_up, w_down)


def _rope_tables_t(seq, dim):
    inv = ROPE_THETA ** (-jnp.arange(0, dim, 2, dtype=F32) / dim)
    ang = jnp.arange(seq, dtype=F32)[:, None] * inv[None, :]
    return jnp.cos(ang).T, jnp.sin(ang).T


def kernel(x, attn_norm_g, w_in, moba_q_norm_g, moba_k_norm_g, mla_q_a_norm_g, w_q_up,
           mla_kv_a_norm_g, w_kv_up, mla_q_norm_g, mla_k_norm_g, w_o, ffn_norm_g,
           w_gate, w_up, w_down):
    b, s, d = x.shape
    assert s % MOBA_BLOCK == 0 and s % ATTN_BLOCK == 0
    assert s % PROJ_TILE == 0 and (b * s) % FFN_TILE == 0
    cos_a, sin_a = _rope_tables_t(s, HEAD_DIM)
    cos_b, sin_b = _rope_tables_t(s, MLA_ROPE_DIM)
    h = x.reshape(b * s, d)
    for l in range(w_in.shape[0]):
        gains = jnp.concatenate([
            mla_q_a_norm_g[l].astype(F32), mla_kv_a_norm_g[l].astype(F32),
            mla_q_norm_g[l].astype(F32) * (MLA_QK_DIM ** -0.5 * LOG2_E),
            mla_k_norm_g[l].astype(F32)]).reshape(-1, 1)
        rope = jnp.concatenate([
            _gained_rope_table(moba_q_norm_g[l].astype(F32) * (HEAD_DIM ** -0.5 * LOG2_E),
                               cos_a, sin_a),
            _gained_rope_table(moba_k_norm_g[l].astype(F32), cos_a, sin_a), cos_b, sin_b], axis=0)
        qa_t, ka, va_t, qb_t, kb, vb_t = _proj_call(
            h, attn_norm_g[l].reshape(1, d), w_in[l].T, w_q_up[l].T.astype(BF16),
            w_kv_up[l].T.astype(BF16), gains, rope, seq=s, tm=PROJ_TILE)
        bound_a, bound_b = _score_bounds([(moba_q_norm_g[l], moba_k_norm_g[l], HEAD_DIM),
                                          (mla_q_norm_g[l], mla_k_norm_g[l], MLA_QK_DIM)])
        (oa_t,) = _attn_call(
            _moba_kernel, qa_t, ka, va_t, bound_a, seq=s, qk_rows=HEAD_DIM, name="moba_attn",
            scratch_shapes=[pltpu.VMEM((s, 2 * HEADS_PER_STEP * HEAD_DIM), BF16)])
        ob_t, w_o_b, w_gate_b, w_up_b, w_down_b = _attn_call(
            _mla_kernel, qb_t, kb, vb_t, bound_b, seq=s, qk_rows=MLA_QK_PAD, name="mla_attn",
            casts=(w_o[l], w_gate[l], w_up[l], w_down[l]))
        h = _out_ffn_call(h, oa_t, ob_t, w_o_b, ffn_norm_g[l].reshape(1, d),
                          w_gate_b, w_up_b, w_down_b, tm=FFN_TILE)
    return h.reshape(b, s, d).astype(x.dtype)
```

```python
import functools
import math

import jax
import jax.numpy as jnp
from jax import lax
from jax.experimental import pallas as pl
from jax.experimental.pallas import tpu as pltpu

HEAD_DIM = 64
MOBA_HEADS = 8
MOBA_WIDTH = MOBA_HEADS * HEAD_DIM
MOBA_BLOCK = 256
MOBA_TOPK = 3
MLA_HEADS = 8
MLA_Q_RANK = 256
MLA_KV_RANK = 128
MLA_NOPE_DIM = 64
MLA_ROPE_DIM = 32
MLA_V_DIM = 64
MLA_QK_DIM = MLA_NOPE_DIM + MLA_ROPE_DIM
MLA_QK_PAD = 128
MLA_WIDTH = MLA_HEADS * MLA_V_DIM
ROPE_THETA = 10000.0
EPS = 1e-6

NEG_BIG = -1e30
ATTN_BLOCK = 256
PROJ_TILE = 512
FFN_TILE = 1024
HEADS_PER_STEP = 2
SCORE_LOOKAHEAD = 3
ONES_ROWS = 16
LOG2_E = math.log2(math.e)
MAX_SAFE_EXPONENT = 48.0
BF16_NORM_MARGIN = 1.02
V7X_VMEM_LIMIT = 56 * 1024 * 1024

F32 = jnp.float32
BF16 = jnp.bfloat16

_NT = (((1,), (1,)), ((), ()))


def _row_slices(sizes):
    out, lo = {}, 0
    for name, n in sizes:
        out[name] = slice(lo, lo + n)
        lo += n
    return out


_GAIN_ROWS = _row_slices([("q_b", MLA_QK_DIM), ("k_b", MLA_QK_DIM)])
_ROPE_ROWS = _row_slices([("q_a", 2 * HEAD_DIM), ("k_a", 2 * HEAD_DIM),
                          ("cos_b", MLA_ROPE_DIM // 2), ("sin_b", MLA_ROPE_DIM // 2)])


def _dot(a, b):
    return jnp.dot(a, b, preferred_element_type=F32)


def _gained_rope_table(g, cos, sin):
    half = cos.shape[0]
    g1, g2 = g[:half, None], g[half:, None]
    return jnp.concatenate([g1 * cos, g2 * sin, g2 * cos, g1 * sin], axis=0)


def _gained_rope_t(n, table):
    half = n.shape[0] // 2
    n1, n2 = n[:half], n[half:]
    t = [table[i * half:(i + 1) * half] for i in range(4)]
    return jnp.concatenate([n1 * t[0] - n2 * t[1], n2 * t[2] + n1 * t[3]], axis=0)


def _rope_t(x, cos, sin):
    half = x.shape[0] // 2
    x1, x2 = x[:half], x[half:]
    return jnp.concatenate([x1 * cos - x2 * sin, x2 * cos + x1 * sin], axis=0)


def _proj_kernel(x_ref, g_attn_ref, w_in_t_ref, w_qup_t_ref, w_kvup_t_ref,
                 gains_ref, rope_ref,
                 qa_t_ref, ka_ref, va_t_ref, qb_t_ref, kb_ref, vb_t_ref,
                 proj_even_ref, proj_odd_ref, w_in_g_ref):
    t = pl.program_id(0)

    @pl.when(t == 0)
    def _():
        proj_odd_ref[...] = jnp.zeros_like(proj_odd_ref)
        w_in_g_ref[...] = w_in_t_ref[...] * g_attn_ref[...]

    def step(proj_new_ref, proj_done_ref):
        x = x_ref[...]
        hn = (x * lax.rsqrt(jnp.mean(x * x, axis=-1, keepdims=True) + EPS)).astype(BF16)

        def project(lo, hi):
            proj_new_ref[lo:hi, :] = lax.dot_general(w_in_g_ref[lo:hi, :], hn, _NT,
                                                     preferred_element_type=F32)

        finish = _finish_stages(proj_done_ref, w_qup_t_ref, w_kvup_t_ref,
                                gains_ref, rope_ref,
                                qa_t_ref, ka_ref, va_t_ref, qb_t_ref, kb_ref, vb_t_ref)
        n_rows = w_in_t_ref.shape[0]
        bounds = [0, MOBA_WIDTH, 2 * MOBA_WIDTH, 3 * MOBA_WIDTH, n_rows]
        chunks = [functools.partial(project, lo, hi) for lo, hi in zip(bounds[:-1], bounds[1:])]
        for i in range(max(len(chunks), len(finish))):
            if i < len(finish):
                finish[i]()
            if i < len(chunks):
                chunks[i]()

    @pl.when(t % 2 == 0)
    def _():
        step(proj_even_ref, proj_odd_ref)

    @pl.when(t % 2 == 1)
    def _():
        step(proj_odd_ref, proj_even_ref)


def _finish_stages(proj_ref, w_qup_t_ref, w_kvup_t_ref,
                   gains_ref, rope_ref,
                   qa_t_ref, ka_ref, va_t_ref, qb_t_ref, kb_ref, vb_t_ref):
    tm = proj_ref.shape[1]
    o_k = MOBA_WIDTH
    o_v = 2 * MOBA_WIDTH
    o_cq = 3 * MOBA_WIDTH
    o_ckv = o_cq + MLA_Q_RANK
    o_pe = o_ckv + MLA_KV_RANK
    kv_w = MLA_NOPE_DIM + MLA_V_DIM
    up = {}
    gain = lambda name: gains_ref[_GAIN_ROWS[name], :]
    rope = lambda name: rope_ref[_ROPE_ROWS[name], :]

    def low_rank():
        cq = proj_ref[o_cq:o_ckv, :]
        cq = cq * lax.rsqrt(jnp.mean(cq * cq, axis=0, keepdims=True) + EPS)
        up["q"] = _dot(w_qup_t_ref[...], cq.astype(BF16))
        ckv = proj_ref[o_ckv:o_pe, :]
        ckv = ckv * lax.rsqrt(jnp.mean(ckv * ckv, axis=0, keepdims=True) + EPS)
        up["kv"] = _dot(w_kvup_t_ref[...], ckv.astype(BF16))

    def moba_q():
        table = rope("q_a")
        for h in range(MOBA_HEADS):
            q = proj_ref[h * HEAD_DIM:(h + 1) * HEAD_DIM, :]
            q = q * lax.rsqrt(jnp.mean(q * q, axis=0, keepdims=True) + EPS)
            qa_t_ref[h * HEAD_DIM:(h + 1) * HEAD_DIM, :] = _gained_rope_t(q, table).astype(BF16)

    def moba_kv():
        table = rope("k_a")
        ka_parts = []
        for h in range(MOBA_HEADS):
            k = proj_ref[o_k + h * HEAD_DIM:o_k + (h + 1) * HEAD_DIM, :]
            k = k * lax.rsqrt(jnp.mean(k * k, axis=0, keepdims=True) + EPS)
            ka_parts.append(_gained_rope_t(k, table))
        ka_ref[...] = jnp.concatenate(ka_parts, axis=0).astype(BF16).T
        va_t_ref[...] = proj_ref[o_v:o_cq, :].astype(BF16)

    def mla_q():
        cos_b, sin_b, gq_b = rope("cos_b"), rope("sin_b"), gain("q_b")
        pad = jnp.zeros((MLA_QK_PAD - MLA_QK_DIM, tm), F32)
        for h in range(MLA_HEADS):
            q = up["q"][h * MLA_QK_DIM:(h + 1) * MLA_QK_DIM]
            q = q * lax.rsqrt(jnp.mean(q * q, axis=0, keepdims=True) + EPS) * gq_b
            q = jnp.concatenate(
                [q[:MLA_NOPE_DIM], _rope_t(q[MLA_NOPE_DIM:], cos_b, sin_b), pad], axis=0)
            qb_t_ref[h * MLA_QK_PAD:(h + 1) * MLA_QK_PAD, :] = q.astype(BF16)

    def mla_kv():
        cos_b, sin_b, gk_b = rope("cos_b"), rope("sin_b"), gain("k_b")
        kv = up["kv"]
        k_pe = proj_ref[o_pe:o_pe + MLA_ROPE_DIM, :]
        pe_ss = jnp.sum(k_pe * k_pe, axis=0, keepdims=True)
        pe_rot = _rope_t(k_pe * gk_b[MLA_NOPE_DIM:], cos_b, sin_b)
        pad = jnp.zeros((MLA_QK_PAD - MLA_QK_DIM, tm), F32)
        kb_parts = []
        for h in range(MLA_HEADS):
            k_nope = kv[h * kv_w:h * kv_w + MLA_NOPE_DIM]
            ss = jnp.sum(k_nope * k_nope, axis=0, keepdims=True) + pe_ss
            r = lax.rsqrt(ss / MLA_QK_DIM + EPS)
            kb_parts += [k_nope * r * gk_b[:MLA_NOPE_DIM], pe_rot * r, pad]
            vb_t_ref[h * MLA_V_DIM:(h + 1) * MLA_V_DIM, :] = (
                kv[h * kv_w + MLA_NOPE_DIM:(h + 1) * kv_w].astype(BF16))
        kb_ref[...] = jnp.concatenate(kb_parts, axis=0).astype(BF16).T

    return [low_rank, moba_q, moba_kv, mla_q, mla_kv]


def _proj_call(x2, g_attn, w_in_t, w_qup_t, w_kvup_t, gains, rope, *, seq, tm):
    t, d = x2.shape
    n_pos = seq // tm
    n_tiles = t // tm
    const = lambda i: (0, 0)
    x_row = lambda i: (jnp.minimum(i, n_tiles - 1), 0)
    row = lambda i: (jnp.maximum(i - 1, 0), 0)
    colb = lambda i: (0, jnp.maximum(i - 1, 0))
    pos = lambda i: (0, jnp.maximum(i - 1, 0) % n_pos)

    def full(a):
        return pl.BlockSpec(a.shape, const, pipeline_mode=pl.Buffered(1))

    out_shape = [
        jax.ShapeDtypeStruct((MOBA_WIDTH, t), BF16),
        jax.ShapeDtypeStruct((t, MOBA_WIDTH), BF16),
        jax.ShapeDtypeStruct((MOBA_WIDTH, t), BF16),
        jax.ShapeDtypeStruct((MLA_HEADS * MLA_QK_PAD, t), BF16),
        jax.ShapeDtypeStruct((t, MLA_HEADS * MLA_QK_PAD), BF16),
        jax.ShapeDtypeStruct((MLA_WIDTH, t), BF16),
    ]
    out_specs = [
        pl.BlockSpec((MOBA_WIDTH, tm), colb),
        pl.BlockSpec((tm, MOBA_WIDTH), row),
        pl.BlockSpec((MOBA_WIDTH, tm), colb),
        pl.BlockSpec((MLA_HEADS * MLA_QK_PAD, tm), colb),
        pl.BlockSpec((tm, MLA_HEADS * MLA_QK_PAD), row),
        pl.BlockSpec((MLA_WIDTH, tm), colb),
    ]
    in_specs = [
        pl.BlockSpec((tm, d), x_row), full(g_attn), full(w_in_t), full(w_qup_t), full(w_kvup_t),
        full(gains), pl.BlockSpec((rope.shape[0], tm), pos),
    ]
    return pl.pallas_call(
        _proj_kernel, out_shape=out_shape, grid=(n_tiles + 1,),
        in_specs=in_specs, out_specs=out_specs, name="proj_heads",
        scratch_shapes=[pltpu.VMEM((w_in_t.shape[0], tm), F32)] * 2
        + [pltpu.VMEM(w_in_t.shape, F32)],
        compiler_params=pltpu.CompilerParams(
            dimension_semantics=("arbitrary",), vmem_limit_bytes=V7X_VMEM_LIMIT),
    )(x2, g_attn, w_in_t, w_qup_t, w_kvup_t, gains, rope)


def _attend(s_past, s_diag, v_aug, bounded):
    key_i = lax.broadcasted_iota(jnp.int32, s_diag.shape, 0)
    qry_i = lax.broadcasted_iota(jnp.int32, s_diag.shape, 1)
    s_diag = jnp.where(key_i <= qry_i, s_diag, NEG_BIG)
    if bounded:
        prob = lambda s: jnp.exp2(s).astype(BF16)
    else:
        m = jnp.max(s_diag, axis=0, keepdims=True)
        if s_past is not None:
            m = jnp.maximum(m, jnp.max(s_past, axis=0, keepdims=True))
        prob = lambda s: jnp.exp2(s - m).astype(BF16)
    p = prob(s_diag)
    if s_past is not None:
        p = jnp.concatenate([prob(s_past), p], axis=0)
    r = _dot(v_aug, p)
    dv = v_aug.shape[0] - ONES_ROWS
    return r[:dv] / r[dv:dv + 1]


def _split3(a):
    hi = a.astype(BF16)
    r1 = a - hi.astype(F32)
    mid = r1.astype(BF16)
    lo = (r1 - mid.astype(F32)).astype(BF16)
    return hi, mid, lo


def _moba_kernel(q_t_ref, k_ref, v_t_ref, o_t_ref, k_ext_ref, *, bounded):
    seq = k_ref.shape[0]
    nb = seq // MOBA_BLOCK
    blk = MOBA_BLOCK
    n_sel = min(MOBA_TOPK, nb - 1)

    k_lanes = k_ref.shape[1]
    assert nb <= k_lanes
    k_ext_ref[:, 0:k_lanes] = k_ref[...]

    @pl.when((pl.program_id(0) == 0) & (pl.program_id(1) == 0))
    def _():
        blk_of_key = lax.broadcasted_iota(jnp.int32, (seq, k_lanes), 0) // blk
        lane_i = lax.broadcasted_iota(jnp.int32, (seq, k_lanes), 1)
        k_ext_ref[:, k_lanes:] = jnp.where(blk_of_key == lane_i, 1.0, 0.0).astype(BF16)

    ind_rows = 2 * nb
    in_block = (lax.broadcasted_iota(jnp.int32, (ind_rows, seq), 1) // blk
                == lax.broadcasted_iota(jnp.int32, (ind_rows, seq), 0))
    k_sum = _dot(jnp.where(in_block, 1.0, 0.0).astype(BF16), k_ref[...])
    k_mean = k_sum[:nb] * (1.0 / blk)
    km_stack = jnp.concatenate([p.astype(F32) for p in _split3(k_mean)]
                               + [jnp.zeros_like(k_mean)], axis=0).astype(BF16)
    row_i = lax.broadcasted_iota(jnp.int32, (nb, seq), 0)
    blk_of_qry = lax.broadcasted_iota(jnp.int32, (nb, seq), 1) // blk
    zeros = jnp.zeros((HEAD_DIM, seq), BF16)
    bias_pad = jnp.zeros((k_lanes - nb, seq), F32)
    ones = jnp.ones((ONES_ROWS, seq), BF16)

    v_aug = [jnp.concatenate([v_t_ref[hh * HEAD_DIM:(hh + 1) * HEAD_DIM, :], ones], axis=0)
             for hh in range(HEADS_PER_STEP)]

    q_w = []
    for hh in range(HEADS_PER_STEP):
        q_h = q_t_ref[hh * HEAD_DIM:(hh + 1) * HEAD_DIM, :]
        q_pad = jnp.concatenate([q_h, zeros] if hh == 0 else [zeros, q_h], axis=0)
        parts = _dot(km_stack, q_pad)
        gate = parts[:nb] + parts[nb:2 * nb] + parts[2 * nb:3 * nb]
        past = row_i < blk_of_qry
        gate = jnp.where(past, gate, -jnp.inf)
        rank = jnp.zeros((nb, seq), jnp.int32)
        for jp in range(nb - 1):
            g_jp = gate[jp:jp + 1, :]
            beats = (g_jp > gate) | ((g_jp == gate) & (jp < row_i))
            rank = rank + beats.astype(jnp.int32)
        bias = jnp.where(past & (rank >= n_sel), NEG_BIG, 0.0)
        q_w.append(jnp.concatenate(
            [q_pad, jnp.concatenate([bias, bias_pad], axis=0).astype(BF16)], axis=0))

    def scores(hh, i):
        s = _dot(k_ext_ref[0:(i + 1) * blk, :], q_w[hh][:, i * blk:(i + 1) * blk])
        return (s[:i * blk] if i > 0 else None), s[i * blk:]

    def finish(hh, i, s_past, s_diag):
        o = _attend(s_past, s_diag, v_aug[hh][:, :(i + 1) * blk], bounded)
        o_t_ref[hh * HEAD_DIM:(hh + 1) * HEAD_DIM, i * blk:(i + 1) * blk] = o.astype(o_t_ref.dtype)

    _pipelined_units(nb, scores, finish)


def _pipelined_units(n_qblk, scores, finish):
    units = [(hh, i) for hh in range(HEADS_PER_STEP)
             for i in (range(n_qblk) if hh % 2 == 0 else reversed(range(n_qblk)))]
    pending = [scores(*unit) for unit in units[:SCORE_LOOKAHEAD]]
    for u, unit in enumerate(units):
        if u + SCORE_LOOKAHEAD < len(units):
            pending.append(scores(*units[u + SCORE_LOOKAHEAD]))
        finish(*unit, *pending.pop(0))


def _mla_kernel(q_t_ref, k_ref, v_t_ref, o_t_ref, *, bounded):
    seq = k_ref.shape[0]
    blk = ATTN_BLOCK
    ones = jnp.ones((ONES_ROWS, seq), BF16)
    v_aug = [jnp.concatenate([v_t_ref[hh * MLA_V_DIM:(hh + 1) * MLA_V_DIM, :], ones], axis=0)
             for hh in range(HEADS_PER_STEP)]

    def scores(hh, i):
        q_rows = slice(hh * MLA_QK_PAD, (hh + 1) * MLA_QK_PAD)
        s = _dot(k_ref[0:(i + 1) * blk, q_rows], q_t_ref[q_rows, i * blk:(i + 1) * blk])
        return (s[:i * blk] if i > 0 else None), s[i * blk:]

    def finish(hh, i, s_past, s_diag):
        o = _attend(s_past, s_diag, v_aug[hh][:, :(i + 1) * blk], bounded)
        o_t_ref[hh * MLA_V_DIM:(hh + 1) * MLA_V_DIM, i * blk:(i + 1) * blk] = o.astype(o_t_ref.dtype)

    _pipelined_units(seq // blk, scores, finish)


def _with_casts(kernel, n_casts, q_t_ref, k_ref, v_t_ref, *refs, **kwargs):
    casts_in, o_t_ref, casts_out = refs[:n_casts], refs[n_casts], refs[n_casts + 1:2 * n_casts + 1]
    for src_ref, dst_ref in zip(casts_in, casts_out):
        dst_ref[...] = src_ref[...].astype(dst_ref.dtype)
    kernel(q_t_ref, k_ref, v_t_ref, o_t_ref, *refs[2 * n_casts + 1:], **kwargs)


def _cast_block_rows(rows, n_steps):
    tile = 16
    per_step = -(-rows // (n_steps * tile)) * tile
    while rows % per_step:
        per_step += tile
    return per_step


def _attn_call(kernel, q_t, k, v_t, score_bound, *, seq, qk_rows, name, scratch_shapes=(),
               casts=()):
    t = k.shape[0]
    n_batch = t // seq
    n_steps = q_t.shape[0] // (HEADS_PER_STEP * qk_rows)
    k_lanes = k.shape[1] // n_steps
    v_rows = v_t.shape[0] // n_steps

    cast_specs = []
    for w in casts:
        rows = _cast_block_rows(w.shape[0], n_batch * n_steps)
        last = w.shape[0] // rows - 1
        cast_specs.append(pl.BlockSpec(
            (rows, w.shape[1]), lambda b, p, last=last: (jnp.minimum(b * n_steps + p, last), 0)))

    def call(bounded, *operands):
        return pl.pallas_call(
            functools.partial(_with_casts, kernel, len(casts), bounded=bounded),
            out_shape=[jax.ShapeDtypeStruct(v_t.shape, BF16)]
            + [jax.ShapeDtypeStruct(w.shape, BF16) for w in casts],
            grid=(n_batch, n_steps),
            in_specs=[
                pl.BlockSpec((HEADS_PER_STEP * qk_rows, seq), lambda b, p: (p, b)),
                pl.BlockSpec((seq, k_lanes), lambda b, p: (b, p)),
                pl.BlockSpec((v_rows, seq), lambda b, p: (p, b)),
            ] + cast_specs,
            out_specs=[pl.BlockSpec((v_rows, seq), lambda b, p: (p, b))] + cast_specs,
            scratch_shapes=list(scratch_shapes),
            name=name + ("_bounded" if bounded else "_shifted"),
            compiler_params=pltpu.CompilerParams(
                dimension_semantics=("arbitrary", "arbitrary") if scratch_shapes or casts
                else ("parallel", "parallel"), vmem_limit_bytes=V7X_VMEM_LIMIT),
        )(*operands)

    return lax.cond(score_bound <= MAX_SAFE_EXPONENT,
                    functools.partial(call, True), functools.partial(call, False),
                    q_t, k, v_t, *casts)


def _score_bound(g_q, g_k, dim):
    return (jnp.max(jnp.abs(g_q.astype(F32))) * jnp.max(jnp.abs(g_k.astype(F32)))
            * (dim ** 0.5 * LOG2_E * BF16_NORM_MARGIN))


def _out_ffn_kernel(x_ref, oa_t_ref, ob_t_ref, w_o_ref, g_ffn_ref, w_gate_ref, w_up_ref,
                    w_down_ref, out_ref):
    o = jnp.concatenate([oa_t_ref[...], ob_t_ref[...]], axis=0).T
    h = x_ref[...] + _dot(o, w_o_ref[...])
    gn = h * lax.rsqrt(jnp.mean(h * h, axis=-1, keepdims=True) + EPS) * g_ffn_ref[...]
    gn = gn.astype(BF16)
    gate = _dot(gn, w_gate_ref[...])
    up = _dot(gn, w_up_ref[...])
    act = (gate * jax.nn.sigmoid(gate) * up).astype(BF16)
    out_ref[...] = h + _dot(act, w_down_ref[...])


def _out_ffn_call(x2, oa_t, ob_t, w_o, g_ffn, w_gate, w_up, w_down, *, tm):
    t, d = x2.shape
    const = lambda i: (0, 0)

    def resident(a):
        return pl.BlockSpec(a.shape, const, pipeline_mode=pl.Buffered(1))

    return pl.pallas_call(
        _out_ffn_kernel,
        out_shape=jax.ShapeDtypeStruct((t, d), x2.dtype),
        grid=(t // tm,),
        in_specs=[
            pl.BlockSpec((tm, d), lambda i: (i, 0)),
            pl.BlockSpec((oa_t.shape[0], tm), lambda i: (0, i)),
            pl.BlockSpec((ob_t.shape[0], tm), lambda i: (0, i)),
            resident(w_o), resident(g_ffn), resident(w_gate), resident(w_up), resident(w_down),
        ],
        out_specs=pl.BlockSpec((tm, d), lambda i: (i, 0)),
        name="out_ffn",
        compiler_params=pltpu.CompilerParams(
            dimension_semantics=("parallel",), vmem_limit_bytes=V7X_VMEM_LIMIT),
    )(x2, oa_t, ob_t, w_o, g_ffn, w_gate, w_up, w_down)


def _rope_tables_t(seq, dim):
    inv = ROPE_THETA ** (-jnp.arange(0, dim, 2, dtype=F32) / dim)
    ang = jnp.arange(seq, dtype=F32)[:, None] * inv[None, :]
    return jnp.cos(ang).T, jnp.sin(ang).T


def kernel(x, attn_norm_g, w_in, moba_q_norm_g, moba_k_norm_g, mla_q_a_norm_g, w_q_up,
           mla_kv_a_norm_g, w_kv_up, mla_q_norm_g, mla_k_norm_g, w_o, ffn_norm_g,
           w_gate, w_up, w_down):
    b, s, d = x.shape
    assert s % MOBA_BLOCK == 0 and s % ATTN_BLOCK == 0
    assert s % PROJ_TILE == 0 and (b * s) % FFN_TILE == 0
    cos_a, sin_a = _rope_tables_t(s, HEAD_DIM)
    cos_b, sin_b = _rope_tables_t(s, MLA_ROPE_DIM)
    h = x.reshape(b * s, d)
    for l in range(w_in.shape[0]):
        gains = jnp.concatenate([
            mla_q_norm_g[l].astype(F32) * (MLA_QK_DIM ** -0.5 * LOG2_E),
            mla_k_norm_g[l].astype(F32)]).reshape(-1, 1)
        w_qup_t = (w_q_up[l] * mla_q_a_norm_g[l].astype(F32)[:, None]).T.astype(BF16)
        w_kvup_t = (w_kv_up[l] * mla_kv_a_norm_g[l].astype(F32)[:, None]).T.astype(BF16)
        rope = jnp.concatenate([
            _gained_rope_table(moba_q_norm_g[l].astype(F32) * (HEAD_DIM ** -0.5 * LOG2_E),
                               cos_a, sin_a),
            _gained_rope_table(moba_k_norm_g[l].astype(F32), cos_a, sin_a), cos_b, sin_b], axis=0)
        qa_t, ka, va_t, qb_t, kb, vb_t = _proj_call(
            h, attn_norm_g[l].reshape(1, d), w_in[l].T, w_qup_t, w_kvup_t, gains, rope,
            seq=s, tm=PROJ_TILE)
        (oa_t,) = _attn_call(
            _moba_kernel, qa_t, ka, va_t,
            _score_bound(moba_q_norm_g[l], moba_k_norm_g[l], HEAD_DIM),
            seq=s, qk_rows=HEAD_DIM, name="moba_attn",
            scratch_shapes=[pltpu.VMEM((s, 2 * HEADS_PER_STEP * HEAD_DIM), BF16)])
        ob_t, w_o_b, w_gate_b, w_up_b, w_down_b = _attn_call(
            _mla_kernel, qb_t, kb, vb_t,
            _score_bound(mla_q_norm_g[l], mla_k_norm_g[l], MLA_QK_DIM),
            seq=s, qk_rows=MLA_QK_PAD, name="mla_attn",
            casts=(w_o[l], w_gate[l], w_up[l], w_down[l]))
        h = _out_ffn_call(h, oa_t, ob_t, w_o_b, ffn_norm_g[l].reshape(1, d),
                          w_gate_b, w_up_b, w_down_b, tm=FFN_TILE)
    return h.reshape(b, s, d).astype(x.dtype)
```

```python
import functools
import math

import jax
import jax.numpy as jnp
from jax import lax
from jax.experimental import pallas as pl
from jax.experimental.pallas import tpu as pltpu

HEAD_DIM = 64
MOBA_HEADS = 8
MOBA_WIDTH = MOBA_HEADS * HEAD_DIM
MOBA_BLOCK = 256
MOBA_TOPK = 3
MLA_HEADS = 8
MLA_Q_RANK = 256
MLA_KV_RANK = 128
MLA_NOPE_DIM = 64
MLA_ROPE_DIM = 32
MLA_V_DIM = 64
MLA_QK_DIM = MLA_NOPE_DIM + MLA_ROPE_DIM
MLA_QK_PAD = 128
MLA_WIDTH = MLA_HEADS * MLA_V_DIM
ROPE_THETA = 10000.0
EPS = 1e-6

NEG_BIG = -1e30
ATTN_BLOCK = 256
PROJ_TILE = 512
FFN_TILE = 1024
HEADS_PER_STEP = 2
ONES_ROWS = 16
LOG2_E = math.log2(math.e)
MAX_SAFE_EXPONENT = 48.0
BF16_NORM_MARGIN = 1.02
V7X_VMEM_LIMIT = 56 * 1024 * 1024

F32 = jnp.float32
BF16 = jnp.bfloat16

_NT = (((1,), (1,)), ((), ()))


def _row_slices(sizes):
    out, lo = {}, 0
    for name, n in sizes:
        out[name] = slice(lo, lo + n)
        lo += n
    return out


_GAIN_ROWS = _row_slices([("cq", MLA_Q_RANK), ("ckv", MLA_KV_RANK),
                          ("q_b", MLA_QK_DIM), ("k_b", MLA_QK_DIM)])
_ROPE_ROWS = _row_slices([("q_a", 2 * HEAD_DIM), ("k_a", 2 * HEAD_DIM),
                          ("cos_b", MLA_ROPE_DIM // 2), ("sin_b", MLA_ROPE_DIM // 2)])


def _dot(a, b):
    return jnp.dot(a, b, preferred_element_type=F32)


def _gained_rope_table(g, cos, sin):
    half = cos.shape[0]
    g1, g2 = g[:half, None], g[half:, None]
    return jnp.concatenate([g1 * cos, g2 * sin, g2 * cos, g1 * sin], axis=0)


def _gained_rope_t(n, table):
    half = n.shape[0] // 2
    n1, n2 = n[:half], n[half:]
    t = [table[i * half:(i + 1) * half] for i in range(4)]
    return jnp.concatenate([n1 * t[0] - n2 * t[1], n2 * t[2] + n1 * t[3]], axis=0)


def _rope_t(x, cos, sin):
    half = x.shape[0] // 2
    x1, x2 = x[:half], x[half:]
    return jnp.concatenate([x1 * cos - x2 * sin, x2 * cos + x1 * sin], axis=0)


def _proj_kernel(x_ref, g_attn_ref, w_in_t_ref, w_qup_t_ref, w_kvup_t_ref,
                 gains_ref, rope_ref,
                 qa_t_ref, ka_ref, va_t_ref, qb_t_ref, kb_ref, vb_t_ref,
                 proj_even_ref, proj_odd_ref):
    t = pl.program_id(0)

    @pl.when(t == 0)
    def _():
        proj_odd_ref[...] = jnp.zeros_like(proj_odd_ref)

    def step(proj_new_ref, proj_done_ref):
        x = x_ref[...]
        hn = x * lax.rsqrt(jnp.mean(x * x, axis=-1, keepdims=True) + EPS) * g_attn_ref[...]
        hn = hn.astype(BF16)

        def project(lo, hi):
            proj_new_ref[lo:hi, :] = lax.dot_general(w_in_t_ref[lo:hi, :], hn, _NT,
                                                     preferred_element_type=F32)

        finish = _finish_stages(proj_done_ref, w_qup_t_ref, w_kvup_t_ref,
                                gains_ref, rope_ref,
                                qa_t_ref, ka_ref, va_t_ref, qb_t_ref, kb_ref, vb_t_ref)
        n_rows = w_in_t_ref.shape[0]
        bounds = [0, MOBA_WIDTH, 2 * MOBA_WIDTH, 3 * MOBA_WIDTH, n_rows]
        chunks = [functools.partial(project, lo, hi) for lo, hi in zip(bounds[:-1], bounds[1:])]
        for i in range(max(len(chunks), len(finish))):
            if i < len(finish):
                finish[i]()
            if i < len(chunks):
                chunks[i]()

    @pl.when(t % 2 == 0)
    def _():
        step(proj_even_ref, proj_odd_ref)

    @pl.when(t % 2 == 1)
    def _():
        step(proj_odd_ref, proj_even_ref)


def _finish_stages(proj_ref, w_qup_t_ref, w_kvup_t_ref,
                   gains_ref, rope_ref,
                   qa_t_ref, ka_ref, va_t_ref, qb_t_ref, kb_ref, vb_t_ref):
    tm = proj_ref.shape[1]
    o_k = MOBA_WIDTH
    o_v = 2 * MOBA_WIDTH
    o_cq = 3 * MOBA_WIDTH
    o_ckv = o_cq + MLA_Q_RANK
    o_pe = o_ckv + MLA_KV_RANK
    kv_w = MLA_NOPE_DIM + MLA_V_DIM
    up = {}
    gain = lambda name: gains_ref[_GAIN_ROWS[name], :]
    rope = lambda name: rope_ref[_ROPE_ROWS[name], :]

    def low_rank():
        cq = proj_ref[o_cq:o_ckv, :]
        cq = cq * lax.rsqrt(jnp.mean(cq * cq, axis=0, keepdims=True) + EPS) * gain("cq")
        up["q"] = _dot(w_qup_t_ref[...], cq.astype(BF16))
        ckv = proj_ref[o_ckv:o_pe, :]
        ckv = ckv * lax.rsqrt(jnp.mean(ckv * ckv, axis=0, keepdims=True) + EPS) * gain("ckv")
        up["kv"] = _dot(w_kvup_t_ref[...], ckv.astype(BF16))

    def moba_q():
        table = rope("q_a")
        for h in range(MOBA_HEADS):
            q = proj_ref[h * HEAD_DIM:(h + 1) * HEAD_DIM, :]
            q = q * lax.rsqrt(jnp.mean(q * q, axis=0, keepdims=True) + EPS)
            qa_t_ref[h * HEAD_DIM:(h + 1) * HEAD_DIM, :] = _gained_rope_t(q, table).astype(BF16)

    def moba_kv():
        table = rope("k_a")
        ka_parts = []
        for h in range(MOBA_HEADS):
            k = proj_ref[o_k + h * HEAD_DIM:o_k + (h + 1) * HEAD_DIM, :]
            k = k * lax.rsqrt(jnp.mean(k * k, axis=0, keepdims=True) + EPS)
            ka_parts.append(_gained_rope_t(k, table))
        ka_ref[...] = jnp.concatenate(ka_parts, axis=0).astype(BF16).T
        va_t_ref[...] = proj_ref[o_v:o_cq, :].astype(BF16)

    def mla_q():
        cos_b, sin_b, gq_b = rope("cos_b"), rope("sin_b"), gain("q_b")
        pad = jnp.zeros((MLA_QK_PAD - MLA_QK_DIM, tm), F32)
        for h in range(MLA_HEADS):
            q = up["q"][h * MLA_QK_DIM:(h + 1) * MLA_QK_DIM]
            q = q * lax.rsqrt(jnp.mean(q * q, axis=0, keepdims=True) + EPS) * gq_b
            q = jnp.concatenate(
                [q[:MLA_NOPE_DIM], _rope_t(q[MLA_NOPE_DIM:], cos_b, sin_b), pad], axis=0)
            qb_t_ref[h * MLA_QK_PAD:(h + 1) * MLA_QK_PAD, :] = q.astype(BF16)

    def mla_kv():
        cos_b, sin_b, gk_b = rope("cos_b"), rope("sin_b"), gain("k_b")
        kv = up["kv"]
        k_pe = proj_ref[o_pe:o_pe + MLA_ROPE_DIM, :]
        pe_ss = jnp.sum(k_pe * k_pe, axis=0, keepdims=True)
        pe_rot = _rope_t(k_pe * gk_b[MLA_NOPE_DIM:], cos_b, sin_b)
        pad = jnp.zeros((MLA_QK_PAD - MLA_QK_DIM, tm), F32)
        kb_parts = []
        for h in range(MLA_HEADS):
            k_nope = kv[h * kv_w:h * kv_w + MLA_NOPE_DIM]
            ss = jnp.sum(k_nope * k_nope, axis=0, keepdims=True) + pe_ss
            r = lax.rsqrt(ss / MLA_QK_DIM + EPS)
            kb_parts += [k_nope * r * gk_b[:MLA_NOPE_DIM], pe_rot * r, pad]
            vb_t_ref[h * MLA_V_DIM:(h + 1) * MLA_V_DIM, :] = (
                kv[h * kv_w + MLA_NOPE_DIM:(h + 1) * kv_w].astype(BF16))
        kb_ref[...] = jnp.concatenate(kb_parts, axis=0).astype(BF16).T

    return [low_rank, moba_q, moba_kv, mla_q, mla_kv]


def _proj_call(x2, g_attn, w_in_t, w_qup_t, w_kvup_t, gains, rope, *, seq, tm):
    t, d = x2.shape
    n_pos = seq // tm
    n_tiles = t // tm
    const = lambda i: (0, 0)
    x_row = lambda i: (jnp.minimum(i, n_tiles - 1), 0)
    row = lambda i: (jnp.maximum(i - 1, 0), 0)
    colb = lambda i: (0, jnp.maximum(i - 1, 0))
    pos = lambda i: (0, jnp.maximum(i - 1, 0) % n_pos)

    def full(a):
        return pl.BlockSpec(a.shape, const, pipeline_mode=pl.Buffered(1))

    out_shape = [
        jax.ShapeDtypeStruct((MOBA_WIDTH, t), BF16),
        jax.ShapeDtypeStruct((t, MOBA_WIDTH), BF16),
        jax.ShapeDtypeStruct((MOBA_WIDTH, t), BF16),
        jax.ShapeDtypeStruct((MLA_HEADS * MLA_QK_PAD, t), BF16),
        jax.ShapeDtypeStruct((t, MLA_HEADS * MLA_QK_PAD), BF16),
        jax.ShapeDtypeStruct((MLA_WIDTH, t), BF16),
    ]
    out_specs = [
        pl.BlockSpec((MOBA_WIDTH, tm), colb),
        pl.BlockSpec((tm, MOBA_WIDTH), row),
        pl.BlockSpec((MOBA_WIDTH, tm), colb),
        pl.BlockSpec((MLA_HEADS * MLA_QK_PAD, tm), colb),
        pl.BlockSpec((tm, MLA_HEADS * MLA_QK_PAD), row),
        pl.BlockSpec((MLA_WIDTH, tm), colb),
    ]
    in_specs = [
        pl.BlockSpec((tm, d), x_row), full(g_attn), full(w_in_t), full(w_qup_t), full(w_kvup_t),
        full(gains), pl.BlockSpec((rope.shape[0], tm), pos),
    ]
    return pl.pallas_call(
        _proj_kernel, out_shape=out_shape, grid=(n_tiles + 1,),
        in_specs=in_specs, out_specs=out_specs, name="proj_heads",
        scratch_shapes=[pltpu.VMEM((w_in_t.shape[0], tm), F32)] * 2,
        compiler_params=pltpu.CompilerParams(
            dimension_semantics=("arbitrary",), vmem_limit_bytes=V7X_VMEM_LIMIT),
    )(x2, g_attn, w_in_t, w_qup_t, w_kvup_t, gains, rope)


def _probabilities(s_blocks, bounded):
    diag = s_blocks[-1]
    key_i = lax.broadcasted_iota(jnp.int32, diag.shape, 0)
    qry_i = lax.broadcasted_iota(jnp.int32, diag.shape, 1)
    blocks = list(s_blocks[:-1]) + [jnp.where(key_i <= qry_i, diag, NEG_BIG)]
    if bounded:
        return [jnp.exp2(s).astype(BF16) for s in blocks]
    m = functools.reduce(jnp.maximum, [jnp.max(s, axis=0, keepdims=True) for s in blocks])
    return [jnp.exp2(s - m).astype(BF16) for s in blocks]


def _split3(a):
    hi = a.astype(BF16)
    r1 = a - hi.astype(F32)
    mid = r1.astype(BF16)
    lo = (r1 - mid.astype(F32)).astype(BF16)
    return hi, mid, lo


def _moba_kernel(q_t_ref, k_ref, v_t_ref, o_t_ref, k_ext_ref, *, bounded):
    seq = k_ref.shape[0]
    nb = seq // MOBA_BLOCK
    blk = MOBA_BLOCK
    n_sel = min(MOBA_TOPK, nb - 1)

    k_lanes = k_ref.shape[1]
    assert nb <= k_lanes
    k_ext_ref[:, 0:k_lanes] = k_ref[...]

    @pl.when((pl.program_id(0) == 0) & (pl.program_id(1) == 0))
    def _():
        blk_of_key = lax.broadcasted_iota(jnp.int32, (seq, k_lanes), 0) // blk
        lane_i = lax.broadcasted_iota(jnp.int32, (seq, k_lanes), 1)
        k_ext_ref[:, k_lanes:] = jnp.where(blk_of_key == lane_i, 1.0, 0.0).astype(BF16)

    ind_rows = 2 * nb
    in_block = (lax.broadcasted_iota(jnp.int32, (ind_rows, seq), 1) // blk
                == lax.broadcasted_iota(jnp.int32, (ind_rows, seq), 0))
    k_sum = _dot(jnp.where(in_block, 1.0, 0.0).astype(BF16), k_ref[...])
    k_mean = k_sum[:nb] * (1.0 / blk)
    km_stack = jnp.concatenate([p.astype(F32) for p in _split3(k_mean)]
                               + [jnp.zeros_like(k_mean)], axis=0).astype(BF16)
    row_i = lax.broadcasted_iota(jnp.int32, (nb, seq), 0)
    blk_of_qry = lax.broadcasted_iota(jnp.int32, (nb, seq), 1) // blk
    zeros = jnp.zeros((HEAD_DIM, seq), BF16)
    bias_pad = jnp.zeros((k_lanes - nb, seq), F32)
    ones = jnp.ones((ONES_ROWS, seq), BF16)

    v_aug = [jnp.concatenate([v_t_ref[hh * HEAD_DIM:(hh + 1) * HEAD_DIM, :], ones], axis=0)
             for hh in range(HEADS_PER_STEP)]

    q_w = []
    for hh in range(HEADS_PER_STEP):
        q_h = q_t_ref[hh * HEAD_DIM:(hh + 1) * HEAD_DIM, :]
        q_pad = jnp.concatenate([q_h, zeros] if hh == 0 else [zeros, q_h], axis=0)
        parts = _dot(km_stack, q_pad)
        gate = parts[:nb] + parts[nb:2 * nb] + parts[2 * nb:3 * nb]
        past = row_i < blk_of_qry
        gate = jnp.where(past, gate, -jnp.inf)
        rank = jnp.zeros((nb, seq), jnp.int32)
        for jp in range(nb - 1):
            g_jp = gate[jp:jp + 1, :]
            beats = (g_jp > gate) | ((g_jp == gate) & (jp < row_i))
            rank = rank + beats.astype(jnp.int32)
        bias = jnp.where(past & (rank >= n_sel), NEG_BIG, 0.0)
        q_w.append(jnp.concatenate(
            [q_pad, jnp.concatenate([bias, bias_pad], axis=0).astype(BF16)], axis=0))

    def score_block(hh, i, j):
        return _dot(k_ext_ref[j * blk:(j + 1) * blk, :], q_w[hh][:, i * blk:(i + 1) * blk])

    def v_block(hh, j):
        return v_aug[hh][:, j * blk:(j + 1) * blk]

    def store(hh, i, o):
        o_t_ref[hh * HEAD_DIM:(hh + 1) * HEAD_DIM, i * blk:(i + 1) * blk] = o.astype(o_t_ref.dtype)

    _pipelined_units(nb, score_block, v_block, store, bounded)


def _pipelined_units(n_qblk, score_block, v_block, store, bounded):
    units = [(hh, i) for hh in range(HEADS_PER_STEP)
             for i in (range(n_qblk) if hh % 2 == 0 else reversed(range(n_qblk)))]
    lookahead = n_qblk
    pending = [[score_block(hh, i, j) for j in range(i + 1)] for hh, i in units[:lookahead]]
    for u, (hh, i) in enumerate(units):
        ahead = units[u + lookahead] if u + lookahead < len(units) else None
        probs = _probabilities(pending.pop(0), bounded)
        ahead_blocks, acc = [], None
        for j in range(max(i + 1, ahead[1] + 1 if ahead else 0)):
            if ahead is not None and j <= ahead[1]:
                ahead_blocks.append(score_block(*ahead, j))
            if j <= i:
                part = _dot(v_block(hh, j), probs[j])
                acc = part if acc is None else acc + part
        if ahead is not None:
            pending.append(ahead_blocks)
        dv = acc.shape[0] - ONES_ROWS
        store(hh, i, acc[:dv] / acc[dv:dv + 1])


def _mla_kernel(q_t_ref, k_ref, v_t_ref, o_t_ref, *, bounded):
    seq = k_ref.shape[0]
    blk = ATTN_BLOCK
    ones = jnp.ones((ONES_ROWS, seq), BF16)
    v_aug = [jnp.concatenate([v_t_ref[hh * MLA_V_DIM:(hh + 1) * MLA_V_DIM, :], ones], axis=0)
             for hh in range(HEADS_PER_STEP)]

    def score_block(hh, i, j):
        q_rows = slice(hh * MLA_QK_PAD, (hh + 1) * MLA_QK_PAD)
        return _dot(k_ref[j * blk:(j + 1) * blk, q_rows], q_t_ref[q_rows, i * blk:(i + 1) * blk])

    def v_block(hh, j):
        return v_aug[hh][:, j * blk:(j + 1) * blk]

    def store(hh, i, o):
        o_t_ref[hh * MLA_V_DIM:(hh + 1) * MLA_V_DIM, i * blk:(i + 1) * blk] = o.astype(o_t_ref.dtype)

    _pipelined_units(seq // blk, score_block, v_block, store, bounded)


def _with_casts(kernel, n_casts, q_t_ref, k_ref, v_t_ref, *refs, **kwargs):
    casts_in, o_t_ref, casts_out = refs[:n_casts], refs[n_casts], refs[n_casts + 1:2 * n_casts + 1]
    for src_ref, dst_ref in zip(casts_in, casts_out):
        dst_ref[...] = src_ref[...].astype(dst_ref.dtype)
    kernel(q_t_ref, k_ref, v_t_ref, o_t_ref, *refs[2 * n_casts + 1:], **kwargs)


def _cast_block_rows(rows, n_steps):
    tile = 16
    per_step = -(-rows // (n_steps * tile)) * tile
    while rows % per_step:
        per_step += tile
    return per_step


def _attn_call(kernel, q_t, k, v_t, score_bound, *, seq, qk_rows, name, scratch_shapes=(),
               casts=()):
    t = k.shape[0]
    n_batch = t // seq
    n_steps = q_t.shape[0] // (HEADS_PER_STEP * qk_rows)
    k_lanes = k.shape[1] // n_steps
    v_rows = v_t.shape[0] // n_steps

    cast_specs = []
    for w in casts:
        rows = _cast_block_rows(w.shape[0], n_batch * n_steps)
        last = w.shape[0] // rows - 1
        cast_specs.append(pl.BlockSpec(
            (rows, w.shape[1]), lambda b, p, last=last: (jnp.minimum(b * n_steps + p, last), 0)))

    def call(bounded, *operands):
        return pl.pallas_call(
            functools.partial(_with_casts, kernel, len(casts), bounded=bounded),
            out_shape=[jax.ShapeDtypeStruct(v_t.shape, BF16)]
            + [jax.ShapeDtypeStruct(w.shape, BF16) for w in casts],
            grid=(n_batch, n_steps),
            in_specs=[
                pl.BlockSpec((HEADS_PER_STEP * qk_rows, seq), lambda b, p: (p, b)),
                pl.BlockSpec((seq, k_lanes), lambda b, p: (b, p)),
                pl.BlockSpec((v_rows, seq), lambda b, p: (p, b)),
            ] + cast_specs,
            out_specs=[pl.BlockSpec((v_rows, seq), lambda b, p: (p, b))] + cast_specs,
            scratch_shapes=list(scratch_shapes),
            name=name + ("_bounded" if bounded else "_shifted"),
            compiler_params=pltpu.CompilerParams(
                dimension_semantics=("arbitrary", "arbitrary") if scratch_shapes or casts
                else ("parallel", "parallel"), vmem_limit_bytes=V7X_VMEM_LIMIT),
        )(*operands)

    return lax.cond(score_bound <= MAX_SAFE_EXPONENT,
                    functools.partial(call, True), functools.partial(call, False),
                    q_t, k, v_t, *casts)


def _score_bound(g_q, g_k, dim):
    return (jnp.max(jnp.abs(g_q.astype(F32))) * jnp.max(jnp.abs(g_k.astype(F32)))
            * (dim ** 0.5 * LOG2_E * BF16_NORM_MARGIN))


def _out_ffn_kernel(x_ref, oa_t_ref, ob_t_ref, w_o_ref, g_ffn_ref, w_gate_ref, w_up_ref,
                    w_down_ref, out_ref):
    o = jnp.concatenate([oa_t_ref[...], ob_t_ref[...]], axis=0).T
    h = x_ref[...] + _dot(o, w_o_ref[...])
    gn = h * lax.rsqrt(jnp.mean(h * h, axis=-1, keepdims=True) + EPS) * g_ffn_ref[...]
    gn = gn.astype(BF16)
    gate = _dot(gn, w_gate_ref[...])
    up = _dot(gn, w_up_ref[...])
    act = (gate * jax.nn.sigmoid(gate) * up).astype(BF16)
    out_ref[...] = h + _dot(act, w_down_ref[...])


def _out_ffn_call(x2, oa_t, ob_t, w_o, g_ffn, w_gate, w_up, w_down, *, tm):
    t, d = x2.shape
    const = lambda i: (0, 0)

    def resident(a):
        return pl.BlockSpec(a.shape, const, pipeline_mode=pl.Buffered(1))

    return pl.pallas_call(
        _out_ffn_kernel,
        out_shape=jax.ShapeDtypeStruct((t, d), x2.dtype),
        grid=(t // tm,),
        in_specs=[
            pl.BlockSpec((tm, d), lambda i: (i, 0)),
            pl.BlockSpec((oa_t.shape[0], tm), lambda i: (0, i)),
            pl.BlockSpec((ob_t.shape[0], tm), lambda i: (0, i)),
            resident(w_o), resident(g_ffn), resident(w_gate), resident(w_up), resident(w_down),
        ],
        out_specs=pl.BlockSpec((tm, d), lambda i: (i, 0)),
        name="out_ffn",
        compiler_params=pltpu.CompilerParams(
            dimension_semantics=("parallel",), vmem_limit_bytes=V7X_VMEM_LIMIT),
    )(x2, oa_t, ob_t, w_o, g_ffn, w_gate, w_up, w_down)


def _rope_tables_t(seq, dim):
    inv = ROPE_THETA ** (-jnp.arange(0, dim, 2, dtype=F32) / dim)
    ang = jnp.arange(seq, dtype=F32)[:, None] * inv[None, :]
    return jnp.cos(ang).T, jnp.sin(ang).T


def kernel(x, attn_norm_g, w_in, moba_q_norm_g, moba_k_norm_g, mla_q_a_norm_g, w_q_up,
           mla_kv_a_norm_g, w_kv_up, mla_q_norm_g, mla_k_norm_g, w_o, ffn_norm_g,
           w_gate, w_up, w_down):
    b, s, d = x.shape
    assert s % MOBA_BLOCK == 0 and s % ATTN_BLOCK == 0
    assert s % PROJ_TILE == 0 and (b * s) % FFN_TILE == 0
    cos_a, sin_a = _rope_tables_t(s, HEAD_DIM)
    cos_b, sin_b = _rope_tables_t(s, MLA_ROPE_DIM)
    h = x.reshape(b * s, d)
    for l in range(w_in.shape[0]):
        gains = jnp.concatenate([
            mla_q_a_norm_g[l].astype(F32), mla_kv_a_norm_g[l].astype(F32),
            mla_q_norm_g[l].astype(F32) * (MLA_QK_DIM ** -0.5 * LOG2_E),
            mla_k_norm_g[l].astype(F32)]).reshape(-1, 1)
        rope = jnp.concatenate([
            _gained_rope_table(moba_q_norm_g[l].astype(F32) * (HEAD_DIM ** -0.5 * LOG2_E),
                               cos_a, sin_a),
            _gained_rope_table(moba_k_norm_g[l].astype(F32), cos_a, sin_a), cos_b, sin_b], axis=0)
        qa_t, ka, va_t, qb_t, kb, vb_t = _proj_call(
            h, attn_norm_g[l].reshape(1, d), w_in[l].T, w_q_up[l].T.astype(BF16),
            w_kv_up[l].T.astype(BF16), gains, rope, seq=s, tm=PROJ_TILE)
        (oa_t,) = _attn_call(
            _moba_kernel, qa_t, ka, va_t,
            _score_bound(moba_q_norm_g[l], moba_k_norm_g[l], HEAD_DIM),
            seq=s, qk_rows=HEAD_DIM, name="moba_attn",
            scratch_shapes=[pltpu.VMEM((s, 2 * HEADS_PER_STEP * HEAD_DIM), BF16)])
        ob_t, w_o_b, w_gate_b, w_up_b, w_down_b = _attn_call(
            _mla_kernel, qb_t, kb, vb_t,
            _score_bound(mla_q_norm_g[l], mla_k_norm_g[l], MLA_QK_DIM),
            seq=s, qk_rows=MLA_QK_PAD, name="mla_attn",
            casts=(w_o[l], w_gate[l], w_up[l], w_down[l]))
        h = _out_ffn_call(h, oa_t, ob_t, w_o_b, ffn_norm_g[l].reshape(1, d),
                          w_gate_b, w_up_b, w_down_b, tm=FFN_TILE)
    return h.reshape(b, s, d).astype(x.dtype)
```

```python
import functools
import math

import jax
import jax.numpy as jnp
from jax import lax
from jax.experimental import pallas as pl
from jax.experimental.pallas import tpu as pltpu

HEAD_DIM = 64
MOBA_HEADS = 8
MOBA_WIDTH = MOBA_HEADS * HEAD_DIM
MOBA_BLOCK = 256
MOBA_TOPK = 3
MLA_HEADS = 8
MLA_Q_RANK = 256
MLA_KV_RANK = 128
MLA_NOPE_DIM = 64
MLA_ROPE_DIM = 32
MLA_V_DIM = 64
MLA_QK_DIM = MLA_NOPE_DIM + MLA_ROPE_DIM
MLA_QK_PAD = 128
MLA_WIDTH = MLA_HEADS * MLA_V_DIM
ROPE_THETA = 10000.0
EPS = 1e-6

NEG_BIG = -1e30
ATTN_BLOCK = 256
PROJ_TILE = 512
FINISH_COLS = 256
FFN_TILE = 1024
HEADS_PER_STEP = 2
ONES_ROWS = 16
LOG2_E = math.log2(math.e)
MAX_SAFE_EXPONENT = 48.0
BF16_NORM_MARGIN = 1.02
V7X_VMEM_LIMIT = 56 * 1024 * 1024

F32 = jnp.float32
BF16 = jnp.bfloat16

_NT = (((1,), (1,)), ((), ()))


def _row_slices(sizes):
    out, lo = {}, 0
    for name, n in sizes:
        out[name] = slice(lo, lo + n)
        lo += n
    return out


_GAIN_ROWS = _row_slices([("cq", MLA_Q_RANK), ("ckv", MLA_KV_RANK),
                          ("q_b", MLA_QK_DIM), ("k_b", MLA_QK_DIM)])
_ROPE_ROWS = _row_slices([("q_a", 2 * HEAD_DIM), ("k_a", 2 * HEAD_DIM),
                          ("cos_b", MLA_ROPE_DIM // 2), ("sin_b", MLA_ROPE_DIM // 2)])


def _dot(a, b):
    return jnp.dot(a, b, preferred_element_type=F32)


def _gained_rope_table(g, cos, sin):
    half = cos.shape[0]
    g1, g2 = g[:half, None], g[half:, None]
    return jnp.concatenate([g1 * cos, g2 * sin, g2 * cos, g1 * sin], axis=0)


def _gained_rope_t(n, table):
    half = n.shape[0] // 2
    n1, n2 = n[:half], n[half:]
    t = [table[i * half:(i + 1) * half] for i in range(4)]
    return jnp.concatenate([n1 * t[0] - n2 * t[1], n2 * t[2] + n1 * t[3]], axis=0)


def _rope_t(x, cos, sin):
    half = x.shape[0] // 2
    x1, x2 = x[:half], x[half:]
    return jnp.concatenate([x1 * cos - x2 * sin, x2 * cos + x1 * sin], axis=0)


def _proj_kernel(x_ref, g_attn_ref, w_in_t_ref, w_qup_t_ref, w_kvup_t_ref,
                 gains_ref, rope_ref,
                 qa_t_ref, ka_ref, va_t_ref, qb_t_ref, kb_ref, vb_t_ref,
                 proj_even_ref, proj_odd_ref):
    t = pl.program_id(0)

    @pl.when(t == 0)
    def _():
        proj_odd_ref[...] = jnp.zeros_like(proj_odd_ref)

    def step(proj_new_ref, proj_done_ref):
        x = x_ref[...]
        hn = x * lax.rsqrt(jnp.mean(x * x, axis=-1, keepdims=True) + EPS) * g_attn_ref[...]
        hn = hn.astype(BF16)

        def project(lo, hi):
            proj_new_ref[lo:hi, :] = lax.dot_general(w_in_t_ref[lo:hi, :], hn, _NT,
                                                     preferred_element_type=F32)

        tm = proj_done_ref.shape[1]
        finish = []
        for lo in range(0, tm, FINISH_COLS):
            finish += _finish_stages(proj_done_ref, w_qup_t_ref, w_kvup_t_ref,
                                     gains_ref, rope_ref,
                                     qa_t_ref, ka_ref, va_t_ref, qb_t_ref, kb_ref, vb_t_ref,
                                     slice(lo, lo + FINISH_COLS))
        n_rows = w_in_t_ref.shape[0]
        bounds = [0, MOBA_WIDTH, 2 * MOBA_WIDTH, 3 * MOBA_WIDTH, n_rows]
        chunks = [functools.partial(project, lo, hi) for lo, hi in zip(bounds[:-1], bounds[1:])]
        per_chunk = -(-len(finish) // len(chunks))
        for i, chunk in enumerate(chunks):
            for stage in finish[i * per_chunk:(i + 1) * per_chunk]:
                stage()
            chunk()

    @pl.when(t % 2 == 0)
    def _():
        step(proj_even_ref, proj_odd_ref)

    @pl.when(t % 2 == 1)
    def _():
        step(proj_odd_ref, proj_even_ref)


def _finish_stages(proj_ref, w_qup_t_ref, w_kvup_t_ref,
                   gains_ref, rope_ref,
                   qa_t_ref, ka_ref, va_t_ref, qb_t_ref, kb_ref, vb_t_ref, cols):
    tm = cols.stop - cols.start
    o_k = MOBA_WIDTH
    o_v = 2 * MOBA_WIDTH
    o_cq = 3 * MOBA_WIDTH
    o_ckv = o_cq + MLA_Q_RANK
    o_pe = o_ckv + MLA_KV_RANK
    kv_w = MLA_NOPE_DIM + MLA_V_DIM
    up = {}
    gain = lambda name: gains_ref[_GAIN_ROWS[name], :]
    rope = lambda name: rope_ref[_ROPE_ROWS[name], cols]

    def low_rank():
        cq = proj_ref[o_cq:o_ckv, cols]
        cq = cq * lax.rsqrt(jnp.mean(cq * cq, axis=0, keepdims=True) + EPS) * gain("cq")
        up["q"] = _dot(w_qup_t_ref[...], cq.astype(BF16))
        ckv = proj_ref[o_ckv:o_pe, cols]
        ckv = ckv * lax.rsqrt(jnp.mean(ckv * ckv, axis=0, keepdims=True) + EPS) * gain("ckv")
        up["kv"] = _dot(w_kvup_t_ref[...], ckv.astype(BF16))

    def moba_q():
        table = rope("q_a")
        for h in range(MOBA_HEADS):
            q = proj_ref[h * HEAD_DIM:(h + 1) * HEAD_DIM, cols]
            q = q * lax.rsqrt(jnp.mean(q * q, axis=0, keepdims=True) + EPS)
            qa_t_ref[h * HEAD_DIM:(h + 1) * HEAD_DIM, cols] = _gained_rope_t(q, table).astype(BF16)

    def moba_kv():
        table = rope("k_a")
        ka_parts = []
        for h in range(MOBA_HEADS):
            k = proj_ref[o_k + h * HEAD_DIM:o_k + (h + 1) * HEAD_DIM, cols]
            k = k * lax.rsqrt(jnp.mean(k * k, axis=0, keepdims=True) + EPS)
            ka_parts.append(_gained_rope_t(k, table))
        ka_ref[cols, :] = jnp.concatenate(ka_parts, axis=0).astype(BF16).T
        va_t_ref[:, cols] = proj_ref[o_v:o_cq, cols].astype(BF16)

    def mla_q():
        cos_b, sin_b, gq_b = rope("cos_b"), rope("sin_b"), gain("q_b")
        pad = jnp.zeros((MLA_QK_PAD - MLA_QK_DIM, tm), F32)
        for h in range(MLA_HEADS):
            q = up["q"][h * MLA_QK_DIM:(h + 1) * MLA_QK_DIM]
            q = q * lax.rsqrt(jnp.mean(q * q, axis=0, keepdims=True) + EPS) * gq_b
            q = jnp.concatenate(
                [q[:MLA_NOPE_DIM], _rope_t(q[MLA_NOPE_DIM:], cos_b, sin_b), pad], axis=0)
            qb_t_ref[h * MLA_QK_PAD:(h + 1) * MLA_QK_PAD, cols] = q.astype(BF16)

    def mla_kv():
        cos_b, sin_b, gk_b = rope("cos_b"), rope("sin_b"), gain("k_b")
        kv = up["kv"]
        k_pe = proj_ref[o_pe:o_pe + MLA_ROPE_DIM, cols]
        pe_ss = jnp.sum(k_pe * k_pe, axis=0, keepdims=True)
        pe_rot = _rope_t(k_pe * gk_b[MLA_NOPE_DIM:], cos_b, sin_b)
        pad = jnp.zeros((MLA_QK_PAD - MLA_QK_DIM, tm), F32)
        kb_parts = []
        for h in range(MLA_HEADS):
            k_nope = kv[h * kv_w:h * kv_w + MLA_NOPE_DIM]
            ss = jnp.sum(k_nope * k_nope, axis=0, keepdims=True) + pe_ss
            r = lax.rsqrt(ss / MLA_QK_DIM + EPS)
            kb_parts += [k_nope * r * gk_b[:MLA_NOPE_DIM], pe_rot * r, pad]
            vb_t_ref[h * MLA_V_DIM:(h + 1) * MLA_V_DIM, cols] = (
                kv[h * kv_w + MLA_NOPE_DIM:(h + 1) * kv_w].astype(BF16))
        kb_ref[cols, :] = jnp.concatenate(kb_parts, axis=0).astype(BF16).T

    return [low_rank, moba_q, moba_kv, mla_q, mla_kv]


def _proj_call(x2, g_attn, w_in_t, w_qup_t, w_kvup_t, gains, rope, *, seq, tm):
    t, d = x2.shape
    n_pos = seq // tm
    n_tiles = t // tm
    const = lambda i: (0, 0)
    x_row = lambda i: (jnp.minimum(i, n_tiles - 1), 0)
    row = lambda i: (jnp.maximum(i - 1, 0), 0)
    colb = lambda i: (0, jnp.maximum(i - 1, 0))
    pos = lambda i: (0, jnp.maximum(i - 1, 0) % n_pos)

    def full(a):
        return pl.BlockSpec(a.shape, const, pipeline_mode=pl.Buffered(1))

    out_shape = [
        jax.ShapeDtypeStruct((MOBA_WIDTH, t), BF16),
        jax.ShapeDtypeStruct((t, MOBA_WIDTH), BF16),
        jax.ShapeDtypeStruct((MOBA_WIDTH, t), BF16),
        jax.ShapeDtypeStruct((MLA_HEADS * MLA_QK_PAD, t), BF16),
        jax.ShapeDtypeStruct((t, MLA_HEADS * MLA_QK_PAD), BF16),
        jax.ShapeDtypeStruct((MLA_WIDTH, t), BF16),
    ]
    out_specs = [
        pl.BlockSpec((MOBA_WIDTH, tm), colb),
        pl.BlockSpec((tm, MOBA_WIDTH), row),
        pl.BlockSpec((MOBA_WIDTH, tm), colb),
        pl.BlockSpec((MLA_HEADS * MLA_QK_PAD, tm), colb),
        pl.BlockSpec((tm, MLA_HEADS * MLA_QK_PAD), row),
        pl.BlockSpec((MLA_WIDTH, tm), colb),
    ]
    in_specs = [
        pl.BlockSpec((tm, d), x_row), full(g_attn), full(w_in_t), full(w_qup_t), full(w_kvup_t),
        full(gains), pl.BlockSpec((rope.shape[0], tm), pos),
    ]
    return pl.pallas_call(
        _proj_kernel, out_shape=out_shape, grid=(n_tiles + 1,),
        in_specs=in_specs, out_specs=out_specs, name="proj_heads",
        scratch_shapes=[pltpu.VMEM((w_in_t.shape[0], tm), F32)] * 2,
        compiler_params=pltpu.CompilerParams(
            dimension_semantics=("arbitrary",), vmem_limit_bytes=V7X_VMEM_LIMIT),
    )(x2, g_attn, w_in_t, w_qup_t, w_kvup_t, gains, rope)


def _probabilities(s_blocks, bounded):
    diag = s_blocks[-1]
    key_i = lax.broadcasted_iota(jnp.int32, diag.shape, 0)
    qry_i = lax.broadcasted_iota(jnp.int32, diag.shape, 1)
    blocks = list(s_blocks[:-1]) + [jnp.where(key_i <= qry_i, diag, NEG_BIG)]
    if bounded:
        return [jnp.exp2(s).astype(BF16) for s in blocks]
    m = functools.reduce(jnp.maximum, [jnp.max(s, axis=0, keepdims=True) for s in blocks])
    return [jnp.exp2(s - m).astype(BF16) for s in blocks]


def _split3(a):
    hi = a.astype(BF16)
    r1 = a - hi.astype(F32)
    mid = r1.astype(BF16)
    lo = (r1 - mid.astype(F32)).astype(BF16)
    return hi, mid, lo


def _moba_kernel(q_t_ref, k_ref, v_t_ref, o_t_ref, k_ext_ref, *, bounded):
    seq = k_ref.shape[0]
    nb = seq // MOBA_BLOCK
    blk = MOBA_BLOCK
    n_sel = min(MOBA_TOPK, nb - 1)

    k_lanes = k_ref.shape[1]
    assert nb <= k_lanes
    k_ext_ref[:, 0:k_lanes] = k_ref[...]

    @pl.when((pl.program_id(0) == 0) & (pl.program_id(1) == 0))
    def _():
        blk_of_key = lax.broadcasted_iota(jnp.int32, (seq, k_lanes), 0) // blk
        lane_i = lax.broadcasted_iota(jnp.int32, (seq, k_lanes), 1)
        k_ext_ref[:, k_lanes:] = jnp.where(blk_of_key == lane_i, 1.0, 0.0).astype(BF16)

    ind_rows = 2 * nb
    in_block = (lax.broadcasted_iota(jnp.int32, (ind_rows, seq), 1) // blk
                == lax.broadcasted_iota(jnp.int32, (ind_rows, seq), 0))
    k_sum = _dot(jnp.where(in_block, 1.0, 0.0).astype(BF16), k_ref[...])
    k_mean = k_sum[:nb] * (1.0 / blk)
    km_stack = jnp.concatenate([p.astype(F32) for p in _split3(k_mean)]
                               + [jnp.zeros_like(k_mean)], axis=0).astype(BF16)
    row_i = lax.broadcasted_iota(jnp.int32, (nb, seq), 0)
    blk_of_qry = lax.broadcasted_iota(jnp.int32, (nb, seq), 1) // blk
    zeros = jnp.zeros((HEAD_DIM, seq), BF16)
    bias_pad = jnp.zeros((k_lanes - nb, seq), F32)
    ones = jnp.ones((ONES_ROWS, seq), BF16)

    v_aug = [jnp.concatenate([v_t_ref[hh * HEAD_DIM:(hh + 1) * HEAD_DIM, :], ones], axis=0)
             for hh in range(HEADS_PER_STEP)]

    q_w = []
    for hh in range(HEADS_PER_STEP):
        q_h = q_t_ref[hh * HEAD_DIM:(hh + 1) * HEAD_DIM, :]
        q_pad = jnp.concatenate([q_h, zeros] if hh == 0 else [zeros, q_h], axis=0)
        parts = _dot(km_stack, q_pad)
        gate = parts[:nb] + parts[nb:2 * nb] + parts[2 * nb:3 * nb]
        past = row_i < blk_of_qry
        gate = jnp.where(past, gate, -jnp.inf)
        rank = jnp.zeros((nb, seq), jnp.int32)
        for jp in range(nb - 1):
            g_jp = gate[jp:jp + 1, :]
            beats = (g_jp > gate) | ((g_jp == gate) & (jp < row_i))
            rank = rank + beats.astype(jnp.int32)
        bias = jnp.where(past & (rank >= n_sel), NEG_BIG, 0.0)
        q_w.append(jnp.concatenate(
            [q_pad, jnp.concatenate([bias, bias_pad], axis=0).astype(BF16)], axis=0))

    def score_block(hh, i, j):
        return _dot(k_ext_ref[j * blk:(j + 1) * blk, :], q_w[hh][:, i * blk:(i + 1) * blk])

    def v_block(hh, j):
        return v_aug[hh][:, j * blk:(j + 1) * blk]

    def store(hh, i, o):
        o_t_ref[hh * HEAD_DIM:(hh + 1) * HEAD_DIM, i * blk:(i + 1) * blk] = o.astype(o_t_ref.dtype)

    _pipelined_units(nb, score_block, v_block, store, bounded)


def _pipelined_units(n_qblk, score_block, v_block, store, bounded):
    units = [(hh, i) for hh in range(HEADS_PER_STEP)
             for i in (range(n_qblk) if hh % 2 == 0 else reversed(range(n_qblk)))]
    lookahead = n_qblk
    pending = [[score_block(hh, i, j) for j in range(i + 1)] for hh, i in units[:lookahead]]
    for u, (hh, i) in enumerate(units):
        ahead = units[u + lookahead] if u + lookahead < len(units) else None
        probs = _probabilities(pending.pop(0), bounded)
        ahead_blocks, acc = [], None
        for j in range(max(i + 1, ahead[1] + 1 if ahead else 0)):
            if ahead is not None and j <= ahead[1]:
                ahead_blocks.append(score_block(*ahead, j))
            if j <= i:
                part = _dot(v_block(hh, j), probs[j])
                acc = part if acc is None else acc + part
        if ahead is not None:
            pending.append(ahead_blocks)
        dv = acc.shape[0] - ONES_ROWS
        store(hh, i, acc[:dv] / acc[dv:dv + 1])


def _mla_kernel(q_t_ref, k_ref, v_t_ref, o_t_ref, *, bounded):
    seq = k_ref.shape[0]
    blk = ATTN_BLOCK
    ones = jnp.ones((ONES_ROWS, seq), BF16)
    v_aug = [jnp.concatenate([v_t_ref[hh * MLA_V_DIM:(hh + 1) * MLA_V_DIM, :], ones], axis=0)
             for hh in range(HEADS_PER_STEP)]

    def score_block(hh, i, j):
        q_rows = slice(hh * MLA_QK_PAD, (hh + 1) * MLA_QK_PAD)
        return _dot(k_ref[j * blk:(j + 1) * blk, q_rows], q_t_ref[q_rows, i * blk:(i + 1) * blk])

    def v_block(hh, j):
        return v_aug[hh][:, j * blk:(j + 1) * blk]

    def store(hh, i, o):
        o_t_ref[hh * MLA_V_DIM:(hh + 1) * MLA_V_DIM, i * blk:(i + 1) * blk] = o.astype(o_t_ref.dtype)

    _pipelined_units(seq // blk, score_block, v_block, store, bounded)


def _with_casts(kernel, n_casts, q_t_ref, k_ref, v_t_ref, *refs, **kwargs):
    casts_in, o_t_ref, casts_out = refs[:n_casts], refs[n_casts], refs[n_casts + 1:2 * n_casts + 1]
    for src_ref, dst_ref in zip(casts_in, casts_out):
        dst_ref[...] = src_ref[...].astype(dst_ref.dtype)
    kernel(q_t_ref, k_ref, v_t_ref, o_t_ref, *refs[2 * n_casts + 1:], **kwargs)


def _cast_block_rows(rows, n_steps):
    tile = 16
    per_step = -(-rows // (n_steps * tile)) * tile
    while rows % per_step:
        per_step += tile
    return per_step


def _attn_call(kernel, q_t, k, v_t, score_bound, *, seq, qk_rows, name, scratch_shapes=(),
               casts=()):
    t = k.shape[0]
    n_batch = t // seq
    n_steps = q_t.shape[0] // (HEADS_PER_STEP * qk_rows)
    k_lanes = k.shape[1] // n_steps
    v_rows = v_t.shape[0] // n_steps

    cast_specs = []
    for w in casts:
        rows = _cast_block_rows(w.shape[0], n_batch * n_steps)
        last = w.shape[0] // rows - 1
        cast_specs.append(pl.BlockSpec(
            (rows, w.shape[1]), lambda b, p, last=last: (jnp.minimum(b * n_steps + p, last), 0)))

    def call(bounded, *operands):
        return pl.pallas_call(
            functools.partial(_with_casts, kernel, len(casts), bounded=bounded),
            out_shape=[jax.ShapeDtypeStruct(v_t.shape, BF16)]
            + [jax.ShapeDtypeStruct(w.shape, BF16) for w in casts],
            grid=(n_batch, n_steps),
            in_specs=[
                pl.BlockSpec((HEADS_PER_STEP * qk_rows, seq), lambda b, p: (p, b)),
                pl.BlockSpec((seq, k_lanes), lambda b, p: (b, p)),
                pl.BlockSpec((v_rows, seq), lambda b, p: (p, b)),
            ] + cast_specs,
            out_specs=[pl.BlockSpec((v_rows, seq), lambda b, p: (p, b))] + cast_specs,
            scratch_shapes=list(scratch_shapes),
            name=name + ("_bounded" if bounded else "_shifted"),
            compiler_params=pltpu.CompilerParams(
                dimension_semantics=("arbitrary", "arbitrary") if scratch_shapes or casts
                else ("parallel", "parallel"), vmem_limit_bytes=V7X_VMEM_LIMIT),
        )(*operands)

    return lax.cond(score_bound <= MAX_SAFE_EXPONENT,
                    functools.partial(call, True), functools.partial(call, False),
                    q_t, k, v_t, *casts)


def _score_bound(g_q, g_k, dim):
    return (jnp.max(jnp.abs(g_q.astype(F32))) * jnp.max(jnp.abs(g_k.astype(F32)))
            * (dim ** 0.5 * LOG2_E * BF16_NORM_MARGIN))


def _out_ffn_kernel(x_ref, oa_t_ref, ob_t_ref, w_o_ref, g_ffn_ref, w_gate_ref, w_up_ref,
                    w_down_ref, out_ref):
    o = jnp.concatenate([oa_t_ref[...], ob_t_ref[...]], axis=0).T
    h = x_ref[...] + _dot(o, w_o_ref[...])
    gn = h * lax.rsqrt(jnp.mean(h * h, axis=-1, keepdims=True) + EPS) * g_ffn_ref[...]
    gn = gn.astype(BF16)
    gate = _dot(gn, w_gate_ref[...])
    up = _dot(gn, w_up_ref[...])
    act = (gate * jax.nn.sigmoid(gate) * up).astype(BF16)
    out_ref[...] = h + _dot(act, w_down_ref[...])


def _out_ffn_call(x2, oa_t, ob_t, w_o, g_ffn, w_gate, w_up, w_down, *, tm):
    t, d = x2.shape
    const = lambda i: (0, 0)

    def resident(a):
        return pl.BlockSpec(a.shape, const, pipeline_mode=pl.Buffered(1))

    return pl.pallas_call(
        _out_ffn_kernel,
        out_shape=jax.ShapeDtypeStruct((t, d), x2.dtype),
        grid=(t // tm,),
        in_specs=[
            pl.BlockSpec((tm, d), lambda i: (i, 0)),
            pl.BlockSpec((oa_t.shape[0], tm), lambda i: (0, i)),
            pl.BlockSpec((ob_t.shape[0], tm), lambda i: (0, i)),
            resident(w_o), resident(g_ffn), resident(w_gate), resident(w_up), resident(w_down),
        ],
        out_specs=pl.BlockSpec((tm, d), lambda i: (i, 0)),
        name="out_ffn",
        compiler_params=pltpu.CompilerParams(
            dimension_semantics=("parallel",), vmem_limit_bytes=V7X_VMEM_LIMIT),
    )(x2, oa_t, ob_t, w_o, g_ffn, w_gate, w_up, w_down)


def _rope_tables_t(seq, dim):
    inv = ROPE_THETA ** (-jnp.arange(0, dim, 2, dtype=F32) / dim)
    ang = jnp.arange(seq, dtype=F32)[:, None] * inv[None, :]
    return jnp.cos(ang).T, jnp.sin(ang).T


def kernel(x, attn_norm_g, w_in, moba_q_norm_g, moba_k_norm_g, mla_q_a_norm_g, w_q_up,
           mla_kv_a_norm_g, w_kv_up, mla_q_norm_g, mla_k_norm_g, w_o, ffn_norm_g,
           w_gate, w_up, w_down):
    b, s, d = x.shape
    assert s % MOBA_BLOCK == 0 and s % ATTN_BLOCK == 0
    assert s % PROJ_TILE == 0 and (b * s) % FFN_TILE == 0
    cos_a, sin_a = _rope_tables_t(s, HEAD_DIM)
    cos_b, sin_b = _rope_tables_t(s, MLA_ROPE_DIM)
    h = x.reshape(b * s, d)
    for l in range(w_in.shape[0]):
        gains = jnp.concatenate([
            mla_q_a_norm_g[l].astype(F32), mla_kv_a_norm_g[l].astype(F32),
            mla_q_norm_g[l].astype(F32) * (MLA_QK_DIM ** -0.5 * LOG2_E),
            mla_k_norm_g[l].astype(F32)]).reshape(-1, 1)
        rope = jnp.concatenate([
            _gained_rope_table(moba_q_norm_g[l].astype(F32) * (HEAD_DIM ** -0.5 * LOG2_E),
                               cos_a, sin_a),
            _gained_rope_table(moba_k_norm_g[l].astype(F32), cos_a, sin_a), cos_b, sin_b], axis=0)
        qa_t, ka, va_t, qb_t, kb, vb_t = _proj_call(
            h, attn_norm_g[l].reshape(1, d), w_in[l].T, w_q_up[l].T.astype(BF16),
            w_kv_up[l].T.astype(BF16), gains, rope, seq=s, tm=PROJ_TILE)
        (oa_t,) = _attn_call(
            _moba_kernel, qa_t, ka, va_t,
            _score_bound(moba_q_norm_g[l], moba_k_norm_g[l], HEAD_DIM),
            seq=s, qk_rows=HEAD_DIM, name="moba_attn",
            scratch_shapes=[pltpu.VMEM((s, 2 * HEADS_PER_STEP * HEAD_DIM), BF16)])
        ob_t, w_o_b, w_gate_b, w_up_b, w_down_b = _attn_call(
            _mla_kernel, qb_t, kb, vb_t,
            _score_bound(mla_q_norm_g[l], mla_k_norm_g[l], MLA_QK_DIM),
            seq=s, qk_rows=MLA_QK_PAD, name="mla_attn",
            casts=(w_o[l], w_gate[l], w_up[l], w_down[l]))
        h = _out_ffn_call(h, oa_t, ob_t, w_o_b, ffn_norm_g[l].reshape(1, d),
                          w_gate_b, w_up_b, w_down_b, tm=FFN_TILE)
    return h.reshape(b, s, d).astype(x.dtype)
```

```python
import functools
import math

import jax
import jax.numpy as jnp
from jax import lax
from jax.experimental import pallas as pl
from jax.experimental.pallas import tpu as pltpu

HEAD_DIM = 64
MOBA_HEADS = 8
MOBA_WIDTH = MOBA_HEADS * HEAD_DIM
MOBA_BLOCK = 256
MOBA_TOPK = 3
MLA_HEADS = 8
MLA_Q_RANK = 256
MLA_KV_RANK = 128
MLA_NOPE_DIM = 64
MLA_ROPE_DIM = 32
MLA_V_DIM = 64
MLA_QK_DIM = MLA_NOPE_DIM + MLA_ROPE_DIM
MLA_QK_PAD = 128
MLA_WIDTH = MLA_HEADS * MLA_V_DIM
ROPE_THETA = 10000.0
EPS = 1e-6

NEG_BIG = -1e30
ATTN_BLOCK = 256
PROJ_TILE = 512
FINISH_COLS = 256
FFN_TILE = 1024
HEADS_PER_STEP = 2
ONES_ROWS = 16
LOG2_E = math.log2(math.e)
MAX_SAFE_EXPONENT = 48.0
BF16_NORM_MARGIN = 1.02
V7X_VMEM_LIMIT = 56 * 1024 * 1024

F32 = jnp.float32
BF16 = jnp.bfloat16

_NT = (((1,), (1,)), ((), ()))


def _row_slices(sizes):
    out, lo = {}, 0
    for name, n in sizes:
        out[name] = slice(lo, lo + n)
        lo += n
    return out


_GAIN_ROWS = _row_slices([("cq", MLA_Q_RANK), ("ckv", MLA_KV_RANK),
                          ("q_b", MLA_QK_DIM), ("k_b", MLA_QK_DIM)])
_ROPE_ROWS = _row_slices([("q_a", 2 * HEAD_DIM), ("k_a", 2 * HEAD_DIM),
                          ("cos_b", MLA_ROPE_DIM // 2), ("sin_b", MLA_ROPE_DIM // 2)])


def _dot(a, b):
    return jnp.dot(a, b, preferred_element_type=F32)


def _gained_rope_table(g, cos, sin):
    half = cos.shape[0]
    g1, g2 = g[:half, None], g[half:, None]
    return jnp.concatenate([g1 * cos, g2 * sin, g2 * cos, g1 * sin], axis=0)


def _gained_rope_t(n, table):
    half = n.shape[0] // 2
    n1, n2 = n[:half], n[half:]
    t = [table[i * half:(i + 1) * half] for i in range(4)]
    return jnp.concatenate([n1 * t[0] - n2 * t[1], n2 * t[2] + n1 * t[3]], axis=0)


def _rope_t(x, cos, sin):
    half = x.shape[0] // 2
    x1, x2 = x[:half], x[half:]
    return jnp.concatenate([x1 * cos - x2 * sin, x2 * cos + x1 * sin], axis=0)


def _proj_kernel(x_ref, g_attn_ref, w_in_t_ref, w_qup_t_ref, w_kvup_t_ref,
                 gains_ref, rope_ref,
                 qa_t_ref, ka_ref, va_t_ref, qb_t_ref, kb_ref, vb_t_ref,
                 proj_even_ref, proj_odd_ref):
    t = pl.program_id(0)

    @pl.when(t == 0)
    def _():
        proj_odd_ref[...] = jnp.zeros_like(proj_odd_ref)

    def step(proj_new_ref, proj_done_ref):
        g_attn = g_attn_ref[...]
        hn_rows = []
        for lo in range(0, x_ref.shape[0], FINISH_COLS):
            x = x_ref[lo:lo + FINISH_COLS, :]
            x = x * lax.rsqrt(jnp.mean(x * x, axis=-1, keepdims=True) + EPS) * g_attn
            hn_rows.append(x.astype(BF16))
        hn = jnp.concatenate(hn_rows, axis=0)

        def project(lo, hi):
            proj_new_ref[lo:hi, :] = lax.dot_general(w_in_t_ref[lo:hi, :], hn, _NT,
                                                     preferred_element_type=F32)

        tm = proj_done_ref.shape[1]
        finish = []
        for lo in range(0, tm, FINISH_COLS):
            finish += _finish_stages(proj_done_ref, w_qup_t_ref, w_kvup_t_ref,
                                     gains_ref, rope_ref,
                                     qa_t_ref, ka_ref, va_t_ref, qb_t_ref, kb_ref, vb_t_ref,
                                     slice(lo, lo + FINISH_COLS))
        n_rows = w_in_t_ref.shape[0]
        bounds = [0, MOBA_WIDTH, 2 * MOBA_WIDTH, 3 * MOBA_WIDTH, n_rows]
        chunks = [functools.partial(project, lo, hi) for lo, hi in zip(bounds[:-1], bounds[1:])]
        per_chunk = -(-len(finish) // len(chunks))
        for i, chunk in enumerate(chunks):
            for stage in finish[i * per_chunk:(i + 1) * per_chunk]:
                stage()
            chunk()

    @pl.when(t % 2 == 0)
    def _():
        step(proj_even_ref, proj_odd_ref)

    @pl.when(t % 2 == 1)
    def _():
        step(proj_odd_ref, proj_even_ref)


def _finish_stages(proj_ref, w_qup_t_ref, w_kvup_t_ref,
                   gains_ref, rope_ref,
                   qa_t_ref, ka_ref, va_t_ref, qb_t_ref, kb_ref, vb_t_ref, cols):
    tm = cols.stop - cols.start
    o_k = MOBA_WIDTH
    o_v = 2 * MOBA_WIDTH
    o_cq = 3 * MOBA_WIDTH
    o_ckv = o_cq + MLA_Q_RANK
    o_pe = o_ckv + MLA_KV_RANK
    kv_w = MLA_NOPE_DIM + MLA_V_DIM
    up = {}
    gain = lambda name: gains_ref[_GAIN_ROWS[name], :]
    rope = lambda name: rope_ref[_ROPE_ROWS[name], cols]

    def low_rank():
        cq = proj_ref[o_cq:o_ckv, cols]
        cq = cq * lax.rsqrt(jnp.mean(cq * cq, axis=0, keepdims=True) + EPS) * gain("cq")
        up["q"] = _dot(w_qup_t_ref[...], cq.astype(BF16))
        ckv = proj_ref[o_ckv:o_pe, cols]
        ckv = ckv * lax.rsqrt(jnp.mean(ckv * ckv, axis=0, keepdims=True) + EPS) * gain("ckv")
        up["kv"] = _dot(w_kvup_t_ref[...], ckv.astype(BF16))

    def moba_q():
        table = rope("q_a")
        for h in range(MOBA_HEADS):
            q = proj_ref[h * HEAD_DIM:(h + 1) * HEAD_DIM, cols]
            q = q * lax.rsqrt(jnp.mean(q * q, axis=0, keepdims=True) + EPS)
            qa_t_ref[h * HEAD_DIM:(h + 1) * HEAD_DIM, cols] = _gained_rope_t(q, table).astype(BF16)

    def moba_kv():
        table = rope("k_a")
        ka_parts = []
        for h in range(MOBA_HEADS):
            k = proj_ref[o_k + h * HEAD_DIM:o_k + (h + 1) * HEAD_DIM, cols]
            k = k * lax.rsqrt(jnp.mean(k * k, axis=0, keepdims=True) + EPS)
            ka_parts.append(_gained_rope_t(k, table))
        ka_ref[cols, :] = jnp.concatenate(ka_parts, axis=0).astype(BF16).T
        va_t_ref[:, cols] = proj_ref[o_v:o_cq, cols].astype(BF16)

    def mla_q():
        cos_b, sin_b, gq_b = rope("cos_b"), rope("sin_b"), gain("q_b")
        pad = jnp.zeros((MLA_QK_PAD - MLA_QK_DIM, tm), F32)
        for h in range(MLA_HEADS):
            q = up["q"][h * MLA_QK_DIM:(h + 1) * MLA_QK_DIM]
            q = q * lax.rsqrt(jnp.mean(q * q, axis=0, keepdims=True) + EPS) * gq_b
            q = jnp.concatenate(
                [q[:MLA_NOPE_DIM], _rope_t(q[MLA_NOPE_DIM:], cos_b, sin_b), pad], axis=0)
            qb_t_ref[h * MLA_QK_PAD:(h + 1) * MLA_QK_PAD, cols] = q.astype(BF16)

    def mla_kv():
        cos_b, sin_b, gk_b = rope("cos_b"), rope("sin_b"), gain("k_b")
        kv = up["kv"]
        k_pe = proj_ref[o_pe:o_pe + MLA_ROPE_DIM, cols]
        pe_ss = jnp.sum(k_pe * k_pe, axis=0, keepdims=True)
        pe_rot = _rope_t(k_pe * gk_b[MLA_NOPE_DIM:], cos_b, sin_b)
        pad = jnp.zeros((MLA_QK_PAD - MLA_QK_DIM, tm), F32)
        kb_parts = []
        for h in range(MLA_HEADS):
            k_nope = kv[h * kv_w:h * kv_w + MLA_NOPE_DIM]
            ss = jnp.sum(k_nope * k_nope, axis=0, keepdims=True) + pe_ss
            r = lax.rsqrt(ss / MLA_QK_DIM + EPS)
            kb_parts += [k_nope * r * gk_b[:MLA_NOPE_DIM], pe_rot * r, pad]
            vb_t_ref[h * MLA_V_DIM:(h + 1) * MLA_V_DIM, cols] = (
                kv[h * kv_w + MLA_NOPE_DIM:(h + 1) * kv_w].astype(BF16))
        kb_ref[cols, :] = jnp.concatenate(kb_parts, axis=0).astype(BF16).T

    return [low_rank, moba_q, moba_kv, mla_q, mla_kv]


def _proj_call(x2, g_attn, w_in_t, w_qup_t, w_kvup_t, gains, rope, *, seq, tm):
    t, d = x2.shape
    n_pos = seq // tm
    n_tiles = t // tm
    const = lambda i: (0, 0)
    x_row = lambda i: (jnp.minimum(i, n_tiles - 1), 0)
    row = lambda i: (jnp.maximum(i - 1, 0), 0)
    colb = lambda i: (0, jnp.maximum(i - 1, 0))
    pos = lambda i: (0, jnp.maximum(i - 1, 0) % n_pos)

    def full(a):
        return pl.BlockSpec(a.shape, const, pipeline_mode=pl.Buffered(1))

    out_shape = [
        jax.ShapeDtypeStruct((MOBA_WIDTH, t), BF16),
        jax.ShapeDtypeStruct((t, MOBA_WIDTH), BF16),
        jax.ShapeDtypeStruct((MOBA_WIDTH, t), BF16),
        jax.ShapeDtypeStruct((MLA_HEADS * MLA_QK_PAD, t), BF16),
        jax.ShapeDtypeStruct((t, MLA_HEADS * MLA_QK_PAD), BF16),
        jax.ShapeDtypeStruct((MLA_WIDTH, t), BF16),
    ]
    out_specs = [
        pl.BlockSpec((MOBA_WIDTH, tm), colb),
        pl.BlockSpec((tm, MOBA_WIDTH), row),
        pl.BlockSpec((MOBA_WIDTH, tm), colb),
        pl.BlockSpec((MLA_HEADS * MLA_QK_PAD, tm), colb),
        pl.BlockSpec((tm, MLA_HEADS * MLA_QK_PAD), row),
        pl.BlockSpec((MLA_WIDTH, tm), colb),
    ]
    in_specs = [
        pl.BlockSpec((tm, d), x_row), full(g_attn), full(w_in_t), full(w_qup_t), full(w_kvup_t),
        full(gains), pl.BlockSpec((rope.shape[0], tm), pos),
    ]
    return pl.pallas_call(
        _proj_kernel, out_shape=out_shape, grid=(n_tiles + 1,),
        in_specs=in_specs, out_specs=out_specs, name="proj_heads",
        scratch_shapes=[pltpu.VMEM((w_in_t.shape[0], tm), F32)] * 2,
        compiler_params=pltpu.CompilerParams(
            dimension_semantics=("arbitrary",), vmem_limit_bytes=V7X_VMEM_LIMIT),
    )(x2, g_attn, w_in_t, w_qup_t, w_kvup_t, gains, rope)


def _probabilities(s_blocks, bounded):
    diag = s_blocks[-1]
    key_i = lax.broadcasted_iota(jnp.int32, diag.shape, 0)
    qry_i = lax.broadcasted_iota(jnp.int32, diag.shape, 1)
    blocks = list(s_blocks[:-1]) + [jnp.where(key_i <= qry_i, diag, NEG_BIG)]
    if bounded:
        return [jnp.exp2(s).astype(BF16) for s in blocks]
    m = functools.reduce(jnp.maximum, [jnp.max(s, axis=0, keepdims=True) for s in blocks])
    return [jnp.exp2(s - m).astype(BF16) for s in blocks]


def _split3(a):
    hi = a.astype(BF16)
    r1 = a - hi.astype(F32)
    mid = r1.astype(BF16)
    lo = (r1 - mid.astype(F32)).astype(BF16)
    return hi, mid, lo


def _moba_kernel(q_t_ref, k_ref, v_t_ref, o_t_ref, k_ext_ref, *, bounded):
    seq = k_ref.shape[0]
    nb = seq // MOBA_BLOCK
    blk = MOBA_BLOCK
    n_sel = min(MOBA_TOPK, nb - 1)

    k_lanes = k_ref.shape[1]
    assert nb <= k_lanes
    k_ext_ref[:, 0:k_lanes] = k_ref[...]

    @pl.when((pl.program_id(0) == 0) & (pl.program_id(1) == 0))
    def _():
        blk_of_key = lax.broadcasted_iota(jnp.int32, (seq, k_lanes), 0) // blk
        lane_i = lax.broadcasted_iota(jnp.int32, (seq, k_lanes), 1)
        k_ext_ref[:, k_lanes:] = jnp.where(blk_of_key == lane_i, 1.0, 0.0).astype(BF16)

    ind_rows = 2 * nb
    in_block = (lax.broadcasted_iota(jnp.int32, (ind_rows, seq), 1) // blk
                == lax.broadcasted_iota(jnp.int32, (ind_rows, seq), 0))
    k_sum = _dot(jnp.where(in_block, 1.0, 0.0).astype(BF16), k_ref[...])
    k_mean = k_sum[:nb] * (1.0 / blk)
    km_stack = jnp.concatenate([p.astype(F32) for p in _split3(k_mean)]
                               + [jnp.zeros_like(k_mean)], axis=0).astype(BF16)
    row_i = lax.broadcasted_iota(jnp.int32, (nb, seq), 0)
    blk_of_qry = lax.broadcasted_iota(jnp.int32, (nb, seq), 1) // blk
    zeros = jnp.zeros((HEAD_DIM, seq), BF16)
    bias_pad = jnp.zeros((k_lanes - nb, seq), F32)
    ones = jnp.ones((ONES_ROWS, seq), BF16)

    v_aug = [jnp.concatenate([v_t_ref[hh * HEAD_DIM:(hh + 1) * HEAD_DIM, :], ones], axis=0)
             for hh in range(HEADS_PER_STEP)]

    q_w = []
    for hh in range(HEADS_PER_STEP):
        q_h = q_t_ref[hh * HEAD_DIM:(hh + 1) * HEAD_DIM, :]
        q_pad = jnp.concatenate([q_h, zeros] if hh == 0 else [zeros, q_h], axis=0)
        parts = _dot(km_stack, q_pad)
        gate = parts[:nb] + parts[nb:2 * nb] + parts[2 * nb:3 * nb]
        past = row_i < blk_of_qry
        gate = jnp.where(past, gate, -jnp.inf)
        rank = jnp.zeros((nb, seq), jnp.int32)
        for jp in range(nb - 1):
            g_jp = gate[jp:jp + 1, :]
            beats = (g_jp > gate) | ((g_jp == gate) & (jp < row_i))
            rank = rank + beats.astype(jnp.int32)
        bias = jnp.where(past & (rank >= n_sel), NEG_BIG, 0.0)
        q_w.append(jnp.concatenate(
            [q_pad, jnp.concatenate([bias, bias_pad], axis=0).astype(BF16)], axis=0))

    def score_block(hh, i, j):
        return _dot(k_ext_ref[j * blk:(j + 1) * blk, :], q_w[hh][:, i * blk:(i + 1) * blk])

    def v_block(hh, j):
        return v_aug[hh][:, j * blk:(j + 1) * blk]

    def store(hh, i, o):
        o_t_ref[hh * HEAD_DIM:(hh + 1) * HEAD_DIM, i * blk:(i + 1) * blk] = o.astype(o_t_ref.dtype)

    _pipelined_units(nb, score_block, v_block, store, bounded)


def _pipelined_units(n_qblk, score_block, v_block, store, bounded):
    units = [(hh, i) for hh in range(HEADS_PER_STEP)
             for i in (range(n_qblk) if hh % 2 == 0 else reversed(range(n_qblk)))]
    lookahead = n_qblk
    pending = [[score_block(hh, i, j) for j in range(i + 1)] for hh, i in units[:lookahead]]
    for u, (hh, i) in enumerate(units):
        ahead = units[u + lookahead] if u + lookahead < len(units) else None
        probs = _probabilities(pending.pop(0), bounded)
        ahead_blocks, acc = [], None
        for j in range(max(i + 1, ahead[1] + 1 if ahead else 0)):
            if ahead is not None and j <= ahead[1]:
                ahead_blocks.append(score_block(*ahead, j))
            if j <= i:
                part = _dot(v_block(hh, j), probs[j])
                acc = part if acc is None else acc + part
        if ahead is not None:
            pending.append(ahead_blocks)
        dv = acc.shape[0] - ONES_ROWS
        store(hh, i, acc[:dv] / acc[dv:dv + 1])


def _mla_kernel(q_t_ref, k_ref, v_t_ref, o_t_ref, *, bounded):
    seq = k_ref.shape[0]
    blk = ATTN_BLOCK
    ones = jnp.ones((ONES_ROWS, seq), BF16)
    v_aug = [jnp.concatenate([v_t_ref[hh * MLA_V_DIM:(hh + 1) * MLA_V_DIM, :], ones], axis=0)
             for hh in range(HEADS_PER_STEP)]

    def score_block(hh, i, j):
        q_rows = slice(hh * MLA_QK_PAD, (hh + 1) * MLA_QK_PAD)
        return _dot(k_ref[j * blk:(j + 1) * blk, q_rows], q_t_ref[q_rows, i * blk:(i + 1) * blk])

    def v_block(hh, j):
        return v_aug[hh][:, j * blk:(j + 1) * blk]

    def store(hh, i, o):
        o_t_ref[hh * MLA_V_DIM:(hh + 1) * MLA_V_DIM, i * blk:(i + 1) * blk] = o.astype(o_t_ref.dtype)

    _pipelined_units(seq // blk, score_block, v_block, store, bounded)


def _with_casts(kernel, n_casts, q_t_ref, k_ref, v_t_ref, *refs, **kwargs):
    casts_in, o_t_ref, casts_out = refs[:n_casts], refs[n_casts], refs[n_casts + 1:2 * n_casts + 1]
    for src_ref, dst_ref in zip(casts_in, casts_out):
        dst_ref[...] = src_ref[...].astype(dst_ref.dtype)
    kernel(q_t_ref, k_ref, v_t_ref, o_t_ref, *refs[2 * n_casts + 1:], **kwargs)


def _cast_block_rows(rows, n_steps):
    tile = 16
    per_step = -(-rows // (n_steps * tile)) * tile
    while rows % per_step:
        per_step += tile
    return per_step


def _attn_call(kernel, q_t, k, v_t, score_bound, *, seq, qk_rows, name, scratch_shapes=(),
               casts=()):
    t = k.shape[0]
    n_batch = t // seq
    n_steps = q_t.shape[0] // (HEADS_PER_STEP * qk_rows)
    k_lanes = k.shape[1] // n_steps
    v_rows = v_t.shape[0] // n_steps

    cast_specs = []
    for w in casts:
        rows = _cast_block_rows(w.shape[0], n_batch * n_steps)
        last = w.shape[0] // rows - 1
        cast_specs.append(pl.BlockSpec(
            (rows, w.shape[1]), lambda b, p, last=last: (jnp.minimum(b * n_steps + p, last), 0)))

    def call(bounded, *operands):
        return pl.pallas_call(
            functools.partial(_with_casts, kernel, len(casts), bounded=bounded),
            out_shape=[jax.ShapeDtypeStruct(v_t.shape, BF16)]
            + [jax.ShapeDtypeStruct(w.shape, BF16) for w in casts],
            grid=(n_batch, n_steps),
            in_specs=[
                pl.BlockSpec((HEADS_PER_STEP * qk_rows, seq), lambda b, p: (p, b)),
                pl.BlockSpec((seq, k_lanes), lambda b, p: (b, p)),
                pl.BlockSpec((v_rows, seq), lambda b, p: (p, b)),
            ] + cast_specs,
            out_specs=[pl.BlockSpec((v_rows, seq), lambda b, p: (p, b))] + cast_specs,
            scratch_shapes=list(scratch_shapes),
            name=name + ("_bounded" if bounded else "_shifted"),
            compiler_params=pltpu.CompilerParams(
                dimension_semantics=("arbitrary", "arbitrary") if scratch_shapes or casts
                else ("parallel", "parallel"), vmem_limit_bytes=V7X_VMEM_LIMIT),
        )(*operands)

    return lax.cond(score_bound <= MAX_SAFE_EXPONENT,
                    functools.partial(call, True), functools.partial(call, False),
                    q_t, k, v_t, *casts)


def _score_bound(g_q, g_k, dim):
    return (jnp.max(jnp.abs(g_q.astype(F32))) * jnp.max(jnp.abs(g_k.astype(F32)))
            * (dim ** 0.5 * LOG2_E * BF16_NORM_MARGIN))


def _out_ffn_kernel(x_ref, oa_t_ref, ob_t_ref, w_o_ref, g_ffn_ref, w_gate_ref, w_up_ref,
                    w_down_ref, out_ref):
    o = jnp.concatenate([oa_t_ref[...], ob_t_ref[...]], axis=0).T
    h = x_ref[...] + _dot(o, w_o_ref[...])
    gn = h * lax.rsqrt(jnp.mean(h * h, axis=-1, keepdims=True) + EPS) * g_ffn_ref[...]
    gn = gn.astype(BF16)
    gate = _dot(gn, w_gate_ref[...])
    up = _dot(gn, w_up_ref[...])
    act = (gate * jax.nn.sigmoid(gate) * up).astype(BF16)
    out_ref[...] = h + _dot(act, w_down_ref[...])


def _out_ffn_call(x2, oa_t, ob_t, w_o, g_ffn, w_gate, w_up, w_down, *, tm):
    t, d = x2.shape
    const = lambda i: (0, 0)

    def resident(a):
        return pl.BlockSpec(a.shape, const, pipeline_mode=pl.Buffered(1))

    return pl.pallas_call(
        _out_ffn_kernel,
        out_shape=jax.ShapeDtypeStruct((t, d), x2.dtype),
        grid=(t // tm,),
        in_specs=[
            pl.BlockSpec((tm, d), lambda i: (i, 0)),
            pl.BlockSpec((oa_t.shape[0], tm), lambda i: (0, i)),
            pl.BlockSpec((ob_t.shape[0], tm), lambda i: (0, i)),
            resident(w_o), resident(g_ffn), resident(w_gate), resident(w_up), resident(w_down),
        ],
        out_specs=pl.BlockSpec((tm, d), lambda i: (i, 0)),
        name="out_ffn",
        compiler_params=pltpu.CompilerParams(
            dimension_semantics=("parallel",), vmem_limit_bytes=V7X_VMEM_LIMIT),
    )(x2, oa_t, ob_t, w_o, g_ffn, w_gate, w_up, w_down)


def _rope_tables_t(seq, dim):
    inv = ROPE_THETA ** (-jnp.arange(0, dim, 2, dtype=F32) / dim)
    ang = jnp.arange(seq, dtype=F32)[:, None] * inv[None, :]
    return jnp.cos(ang).T, jnp.sin(ang).T


def kernel(x, attn_norm_g, w_in, moba_q_norm_g, moba_k_norm_g, mla_q_a_norm_g, w_q_up,
           mla_kv_a_norm_g, w_kv_up, mla_q_norm_g, mla_k_norm_g, w_o, ffn_norm_g,
           w_gate, w_up, w_down):
    b, s, d = x.shape
    assert s % MOBA_BLOCK == 0 and s % ATTN_BLOCK == 0
    assert s % PROJ_TILE == 0 and (b * s) % FFN_TILE == 0
    cos_a, sin_a = _rope_tables_t(s, HEAD_DIM)
    cos_b, sin_b = _rope_tables_t(s, MLA_ROPE_DIM)
    h = x.reshape(b * s, d)
    for l in range(w_in.shape[0]):
        gains = jnp.concatenate([
            mla_q_a_norm_g[l].astype(F32), mla_kv_a_norm_g[l].astype(F32),
            mla_q_norm_g[l].astype(F32) * (MLA_QK_DIM ** -0.5 * LOG2_E),
            mla_k_norm_g[l].astype(F32)]).reshape(-1, 1)
        rope = jnp.concatenate([
            _gained_rope_table(moba_q_norm_g[l].astype(F32) * (HEAD_DIM ** -0.5 * LOG2_E),
                               cos_a, sin_a),
            _gained_rope_table(moba_k_norm_g[l].astype(F32), cos_a, sin_a), cos_b, sin_b], axis=0)
        qa_t, ka, va_t, qb_t, kb, vb_t = _proj_call(
            h, attn_norm_g[l].reshape(1, d), w_in[l].T, w_q_up[l].T.astype(BF16),
            w_kv_up[l].T.astype(BF16), gains, rope, seq=s, tm=PROJ_TILE)
        (oa_t,) = _attn_call(
            _moba_kernel, qa_t, ka, va_t,
            _score_bound(moba_q_norm_g[l], moba_k_norm_g[l], HEAD_DIM),
            seq=s, qk_rows=HEAD_DIM, name="moba_attn",
            scratch_shapes=[pltpu.VMEM((s, 2 * HEADS_PER_STEP * HEAD_DIM), BF16)])
        ob_t, w_o_b, w_gate_b, w_up_b, w_down_b = _attn_call(
            _mla_kernel, qb_t, kb, vb_t,
            _score_bound(mla_q_norm_g[l], mla_k_norm_g[l], MLA_QK_DIM),
            seq=s, qk_rows=MLA_QK_PAD, name="mla_attn",
            casts=(w_o[l], w_gate[l], w_up[l], w_down[l]))
        h = _out_ffn_call(h, oa_t, ob_t, w_o_b, ffn_norm_g[l].reshape(1, d),
                          w_gate_b, w_up_b, w_down_b, tm=FFN_TILE)
    return h.reshape(b, s, d).astype(x.dtype)
```

```python
import functools
import math

import jax
import jax.numpy as jnp
from jax import lax
from jax.experimental import pallas as pl
from jax.experimental.pallas import tpu as pltpu

HEAD_DIM = 64
MOBA_HEADS = 8
MOBA_WIDTH = MOBA_HEADS * HEAD_DIM
MOBA_BLOCK = 256
MOBA_TOPK = 3
MLA_HEADS = 8
MLA_Q_RANK = 256
MLA_KV_RANK = 128
MLA_NOPE_DIM = 64
MLA_ROPE_DIM = 32
MLA_V_DIM = 64
MLA_QK_DIM = MLA_NOPE_DIM + MLA_ROPE_DIM
MLA_QK_PAD = 128
MLA_WIDTH = MLA_HEADS * MLA_V_DIM
ROPE_THETA = 10000.0
EPS = 1e-6

NEG_BIG = -1e30
ATTN_BLOCK = 256
PROJ_TILE = 512
FINISH_COLS = 256
FFN_TILE = 1024
HEADS_PER_STEP = 2
MLA_HEADS_PER_STEP = 2
ONES_ROWS = 16
LOG2_E = math.log2(math.e)
MAX_SAFE_EXPONENT = 48.0
BF16_NORM_MARGIN = 1.02
V7X_VMEM_LIMIT = 56 * 1024 * 1024

F32 = jnp.float32
BF16 = jnp.bfloat16

_NT = (((1,), (1,)), ((), ()))


def _row_slices(sizes):
    out, lo = {}, 0
    for name, n in sizes:
        out[name] = slice(lo, lo + n)
        lo += n
    return out


_GAIN_ROWS = _row_slices([("cq", MLA_Q_RANK), ("ckv", MLA_KV_RANK),
                          ("q_b", MLA_QK_DIM), ("k_b", MLA_QK_DIM)])
_ROPE_ROWS = _row_slices([("q_a", 2 * HEAD_DIM), ("k_a", 2 * HEAD_DIM),
                          ("cos_b", MLA_ROPE_DIM // 2), ("sin_b", MLA_ROPE_DIM // 2)])


def _dot(a, b):
    return jnp.dot(a, b, preferred_element_type=F32)


def _gained_rope_table(g, cos, sin):
    half = cos.shape[0]
    g1, g2 = g[:half, None], g[half:, None]
    return jnp.concatenate([g1 * cos, g2 * sin, g2 * cos, g1 * sin], axis=0)


def _gained_rope_t(n, table):
    half = n.shape[0] // 2
    n1, n2 = n[:half], n[half:]
    t = [table[i * half:(i + 1) * half] for i in range(4)]
    return jnp.concatenate([n1 * t[0] - n2 * t[1], n2 * t[2] + n1 * t[3]], axis=0)


def _rope_t(x, cos, sin):
    half = x.shape[0] // 2
    x1, x2 = x[:half], x[half:]
    return jnp.concatenate([x1 * cos - x2 * sin, x2 * cos + x1 * sin], axis=0)


def _proj_kernel(x_ref, g_attn_ref, w_in_t_ref, w_qup_t_ref, w_kvup_t_ref,
                 gains_ref, rope_ref,
                 qa_t_ref, ka_ref, va_t_ref, qb_t_ref, kb_ref, vb_t_ref,
                 proj_even_ref, proj_odd_ref):
    t = pl.program_id(0)

    @pl.when(t == 0)
    def _():
        proj_odd_ref[...] = jnp.zeros_like(proj_odd_ref)

    def step(proj_new_ref, proj_done_ref):
        g_attn = g_attn_ref[...]
        hn_rows = []
        for lo in range(0, x_ref.shape[0], FINISH_COLS):
            x = x_ref[lo:lo + FINISH_COLS, :]
            x = x * lax.rsqrt(jnp.mean(x * x, axis=-1, keepdims=True) + EPS) * g_attn
            hn_rows.append(x.astype(BF16))
        hn = jnp.concatenate(hn_rows, axis=0)

        def project(lo, hi):
            proj_new_ref[lo:hi, :] = lax.dot_general(w_in_t_ref[lo:hi, :], hn, _NT,
                                                     preferred_element_type=F32)

        tm = proj_done_ref.shape[1]
        finish = []
        for lo in range(0, tm, FINISH_COLS):
            finish += _finish_stages(proj_done_ref, w_qup_t_ref, w_kvup_t_ref,
                                     gains_ref, rope_ref,
                                     qa_t_ref, ka_ref, va_t_ref, qb_t_ref, kb_ref, vb_t_ref,
                                     slice(lo, lo + FINISH_COLS))
        n_rows = w_in_t_ref.shape[0]
        bounds = [0, MOBA_WIDTH, 2 * MOBA_WIDTH, 3 * MOBA_WIDTH, n_rows]
        chunks = [functools.partial(project, lo, hi) for lo, hi in zip(bounds[:-1], bounds[1:])]
        per_chunk = -(-len(finish) // len(chunks))
        for i, chunk in enumerate(chunks):
            for stage in finish[i * per_chunk:(i + 1) * per_chunk]:
                stage()
            chunk()

    @pl.when(t % 2 == 0)
    def _():
        step(proj_even_ref, proj_odd_ref)

    @pl.when(t % 2 == 1)
    def _():
        step(proj_odd_ref, proj_even_ref)


def _finish_stages(proj_ref, w_qup_t_ref, w_kvup_t_ref,
                   gains_ref, rope_ref,
                   qa_t_ref, ka_ref, va_t_ref, qb_t_ref, kb_ref, vb_t_ref, cols):
    tm = cols.stop - cols.start
    o_k = MOBA_WIDTH
    o_v = 2 * MOBA_WIDTH
    o_cq = 3 * MOBA_WIDTH
    o_ckv = o_cq + MLA_Q_RANK
    o_pe = o_ckv + MLA_KV_RANK
    kv_w = MLA_NOPE_DIM + MLA_V_DIM
    up = {}
    gain = lambda name: gains_ref[_GAIN_ROWS[name], :]
    rope = lambda name: rope_ref[_ROPE_ROWS[name], cols]

    def low_rank():
        cq = proj_ref[o_cq:o_ckv, cols]
        cq = cq * lax.rsqrt(jnp.mean(cq * cq, axis=0, keepdims=True) + EPS) * gain("cq")
        up["q"] = _dot(w_qup_t_ref[...], cq.astype(BF16))
        ckv = proj_ref[o_ckv:o_pe, cols]
        ckv = ckv * lax.rsqrt(jnp.mean(ckv * ckv, axis=0, keepdims=True) + EPS) * gain("ckv")
        up["kv"] = _dot(w_kvup_t_ref[...], ckv.astype(BF16))

    def moba_q():
        table = rope("q_a")
        for h in range(MOBA_HEADS):
            q = proj_ref[h * HEAD_DIM:(h + 1) * HEAD_DIM, cols]
            q = q * lax.rsqrt(jnp.mean(q * q, axis=0, keepdims=True) + EPS)
            qa_t_ref[h * HEAD_DIM:(h + 1) * HEAD_DIM, cols] = _gained_rope_t(q, table).astype(BF16)

    def moba_kv():
        table = rope("k_a")
        ka_parts = []
        for h in range(MOBA_HEADS):
            k = proj_ref[o_k + h * HEAD_DIM:o_k + (h + 1) * HEAD_DIM, cols]
            k = k * lax.rsqrt(jnp.mean(k * k, axis=0, keepdims=True) + EPS)
            ka_parts.append(_gained_rope_t(k, table))
        ka_ref[cols, :] = jnp.concatenate(ka_parts, axis=0).astype(BF16).T
        va_t_ref[:, cols] = proj_ref[o_v:o_cq, cols].astype(BF16)

    def mla_q():
        cos_b, sin_b, gq_b = rope("cos_b"), rope("sin_b"), gain("q_b")
        pad = jnp.zeros((MLA_QK_PAD - MLA_QK_DIM, tm), F32)
        for h in range(MLA_HEADS):
            q = up["q"][h * MLA_QK_DIM:(h + 1) * MLA_QK_DIM]
            q = q * lax.rsqrt(jnp.mean(q * q, axis=0, keepdims=True) + EPS) * gq_b
            q = jnp.concatenate(
                [q[:MLA_NOPE_DIM], _rope_t(q[MLA_NOPE_DIM:], cos_b, sin_b), pad], axis=0)
            qb_t_ref[h * MLA_QK_PAD:(h + 1) * MLA_QK_PAD, cols] = q.astype(BF16)

    def mla_kv():
        cos_b, sin_b, gk_b = rope("cos_b"), rope("sin_b"), gain("k_b")
        kv = up["kv"]
        k_pe = proj_ref[o_pe:o_pe + MLA_ROPE_DIM, cols]
        pe_ss = jnp.sum(k_pe * k_pe, axis=0, keepdims=True)
        pe_rot = _rope_t(k_pe * gk_b[MLA_NOPE_DIM:], cos_b, sin_b)
        pad = jnp.zeros((MLA_QK_PAD - MLA_QK_DIM, tm), F32)
        kb_parts = []
        for h in range(MLA_HEADS):
            k_nope = kv[h * kv_w:h * kv_w + MLA_NOPE_DIM]
            ss = jnp.sum(k_nope * k_nope, axis=0, keepdims=True) + pe_ss
            r = lax.rsqrt(ss / MLA_QK_DIM + EPS)
            kb_parts += [k_nope * r * gk_b[:MLA_NOPE_DIM], pe_rot * r, pad]
            vb_t_ref[h * MLA_V_DIM:(h + 1) * MLA_V_DIM, cols] = (
                kv[h * kv_w + MLA_NOPE_DIM:(h + 1) * kv_w].astype(BF16))
        kb_ref[cols, :] = jnp.concatenate(kb_parts, axis=0).astype(BF16).T

    return [low_rank, moba_q, moba_kv, mla_q, mla_kv]


def _proj_call(x2, g_attn, w_in_t, w_qup_t, w_kvup_t, gains, rope, *, seq, tm):
    t, d = x2.shape
    n_pos = seq // tm
    n_tiles = t // tm
    const = lambda i: (0, 0)
    x_row = lambda i: (jnp.minimum(i, n_tiles - 1), 0)
    row = lambda i: (jnp.maximum(i - 1, 0), 0)
    colb = lambda i: (0, jnp.maximum(i - 1, 0))
    pos = lambda i: (0, jnp.maximum(i - 1, 0) % n_pos)

    def full(a):
        return pl.BlockSpec(a.shape, const, pipeline_mode=pl.Buffered(1))

    out_shape = [
        jax.ShapeDtypeStruct((MOBA_WIDTH, t), BF16),
        jax.ShapeDtypeStruct((t, MOBA_WIDTH), BF16),
        jax.ShapeDtypeStruct((MOBA_WIDTH, t), BF16),
        jax.ShapeDtypeStruct((MLA_HEADS * MLA_QK_PAD, t), BF16),
        jax.ShapeDtypeStruct((t, MLA_HEADS * MLA_QK_PAD), BF16),
        jax.ShapeDtypeStruct((MLA_WIDTH, t), BF16),
    ]
    out_specs = [
        pl.BlockSpec((MOBA_WIDTH, tm), colb),
        pl.BlockSpec((tm, MOBA_WIDTH), row),
        pl.BlockSpec((MOBA_WIDTH, tm), colb),
        pl.BlockSpec((MLA_HEADS * MLA_QK_PAD, tm), colb),
        pl.BlockSpec((tm, MLA_HEADS * MLA_QK_PAD), row),
        pl.BlockSpec((MLA_WIDTH, tm), colb),
    ]
    in_specs = [
        pl.BlockSpec((tm, d), x_row), full(g_attn), full(w_in_t), full(w_qup_t), full(w_kvup_t),
        full(gains), pl.BlockSpec((rope.shape[0], tm), pos),
    ]
    return pl.pallas_call(
        _proj_kernel, out_shape=out_shape, grid=(n_tiles + 1,),
        in_specs=in_specs, out_specs=out_specs, name="proj_heads",
        scratch_shapes=[pltpu.VMEM((w_in_t.shape[0], tm), F32)] * 2,
        compiler_params=pltpu.CompilerParams(
            dimension_semantics=("arbitrary",), vmem_limit_bytes=V7X_VMEM_LIMIT),
    )(x2, g_attn, w_in_t, w_qup_t, w_kvup_t, gains, rope)


def _probabilities(s_blocks, bounded):
    diag = s_blocks[-1]
    key_i = lax.broadcasted_iota(jnp.int32, diag.shape, 0)
    qry_i = lax.broadcasted_iota(jnp.int32, diag.shape, 1)
    blocks = list(s_blocks[:-1]) + [jnp.where(key_i <= qry_i, diag, NEG_BIG)]
    if bounded:
        return [jnp.exp2(s).astype(BF16) for s in blocks]
    m = functools.reduce(jnp.maximum, [jnp.max(s, axis=0, keepdims=True) for s in blocks])
    return [jnp.exp2(s - m).astype(BF16) for s in blocks]


def _split3(a):
    hi = a.astype(BF16)
    r1 = a - hi.astype(F32)
    mid = r1.astype(BF16)
    lo = (r1 - mid.astype(F32)).astype(BF16)
    return hi, mid, lo


def _moba_kernel(q_t_ref, k_ref, v_t_ref, o_t_ref, k_ext_ref, *, bounded):
    seq = k_ref.shape[0]
    nb = seq // MOBA_BLOCK
    blk = MOBA_BLOCK
    n_sel = min(MOBA_TOPK, nb - 1)

    k_lanes = k_ref.shape[1]
    assert nb <= k_lanes
    k_ext_ref[:, 0:k_lanes] = k_ref[...]

    @pl.when((pl.program_id(0) == 0) & (pl.program_id(1) == 0))
    def _():
        blk_of_key = lax.broadcasted_iota(jnp.int32, (seq, k_lanes), 0) // blk
        lane_i = lax.broadcasted_iota(jnp.int32, (seq, k_lanes), 1)
        k_ext_ref[:, k_lanes:] = jnp.where(blk_of_key == lane_i, 1.0, 0.0).astype(BF16)

    ind_rows = 2 * nb
    in_block = (lax.broadcasted_iota(jnp.int32, (ind_rows, seq), 1) // blk
                == lax.broadcasted_iota(jnp.int32, (ind_rows, seq), 0))
    k_sum = _dot(jnp.where(in_block, 1.0, 0.0).astype(BF16), k_ref[...])
    k_mean = k_sum[:nb] * (1.0 / blk)
    km_stack = jnp.concatenate([p.astype(F32) for p in _split3(k_mean)]
                               + [jnp.zeros_like(k_mean)], axis=0).astype(BF16)
    row_i = lax.broadcasted_iota(jnp.int32, (nb, seq), 0)
    blk_of_qry = lax.broadcasted_iota(jnp.int32, (nb, seq), 1) // blk
    zeros = jnp.zeros((HEAD_DIM, seq), BF16)
    bias_pad = jnp.zeros((k_lanes - nb, seq), F32)
    ones = jnp.ones((ONES_ROWS, seq), BF16)

    v_aug = [jnp.concatenate([v_t_ref[hh * HEAD_DIM:(hh + 1) * HEAD_DIM, :], ones], axis=0)
             for hh in range(HEADS_PER_STEP)]

    q_w = []
    for hh in range(HEADS_PER_STEP):
        q_h = q_t_ref[hh * HEAD_DIM:(hh + 1) * HEAD_DIM, :]
        q_pad = jnp.concatenate([q_h, zeros] if hh == 0 else [zeros, q_h], axis=0)
        parts = _dot(km_stack, q_pad)
        gate = parts[:nb] + parts[nb:2 * nb] + parts[2 * nb:3 * nb]
        past = row_i < blk_of_qry
        gate = jnp.where(past, gate, -jnp.inf)
        rank = jnp.zeros((nb, seq), jnp.int32)
        for jp in range(nb - 1):
            g_jp = gate[jp:jp + 1, :]
            beats = (g_jp > gate) | ((g_jp == gate) & (jp < row_i))
            rank = rank + beats.astype(jnp.int32)
        bias = jnp.where(past & (rank >= n_sel), NEG_BIG, 0.0)
        q_w.append(jnp.concatenate(
            [q_pad, jnp.concatenate([bias, bias_pad], axis=0).astype(BF16)], axis=0))

    def score_block(hh, i, j):
        return _dot(k_ext_ref[j * blk:(j + 1) * blk, :], q_w[hh][:, i * blk:(i + 1) * blk])

    def v_block(hh, j):
        return v_aug[hh][:, j * blk:(j + 1) * blk]

    def store(hh, i, o):
        o_t_ref[hh * HEAD_DIM:(hh + 1) * HEAD_DIM, i * blk:(i + 1) * blk] = o.astype(o_t_ref.dtype)

    _pipelined_units(HEADS_PER_STEP, nb, score_block, v_block, store, bounded)


def _pipelined_units(n_heads, n_qblk, score_block, v_block, store, bounded):
    units = [(hh, i) for hh in range(n_heads)
             for i in (range(n_qblk) if hh % 2 == 0 else reversed(range(n_qblk)))]
    lookahead = n_qblk
    pending = [[score_block(hh, i, j) for j in range(i + 1)] for hh, i in units[:lookahead]]
    for u, (hh, i) in enumerate(units):
        ahead = units[u + lookahead] if u + lookahead < len(units) else None
        probs = _probabilities(pending.pop(0), bounded)
        ahead_blocks, acc = [], None
        for j in range(max(i + 1, ahead[1] + 1 if ahead else 0)):
            if ahead is not None and j <= ahead[1]:
                ahead_blocks.append(score_block(*ahead, j))
            if j <= i:
                part = _dot(v_block(hh, j), probs[j])
                acc = part if acc is None else acc + part
        if ahead is not None:
            pending.append(ahead_blocks)
        dv = acc.shape[0] - ONES_ROWS
        store(hh, i, acc[:dv] / acc[dv:dv + 1])


def _mla_kernel(q_t_ref, k_ref, v_t_ref, o_t_ref, *, bounded):
    seq = k_ref.shape[0]
    blk = ATTN_BLOCK
    n_heads = q_t_ref.shape[0] // MLA_QK_PAD
    ones = jnp.ones((ONES_ROWS, seq), BF16)
    v_aug = [jnp.concatenate([v_t_ref[hh * MLA_V_DIM:(hh + 1) * MLA_V_DIM, :], ones], axis=0)
             for hh in range(n_heads)]

    def score_block(hh, i, j):
        q_rows = slice(hh * MLA_QK_PAD, (hh + 1) * MLA_QK_PAD)
        return _dot(k_ref[j * blk:(j + 1) * blk, q_rows], q_t_ref[q_rows, i * blk:(i + 1) * blk])

    def v_block(hh, j):
        return v_aug[hh][:, j * blk:(j + 1) * blk]

    def store(hh, i, o):
        o_t_ref[hh * MLA_V_DIM:(hh + 1) * MLA_V_DIM, i * blk:(i + 1) * blk] = o.astype(o_t_ref.dtype)

    _pipelined_units(n_heads, seq // blk, score_block, v_block, store, bounded)


def _with_casts(kernel, n_casts, bounded_ref, q_t_ref, k_ref, v_t_ref, *refs):
    casts_in, o_t_ref, casts_out = refs[:n_casts], refs[n_casts], refs[n_casts + 1:2 * n_casts + 1]
    for src_ref, dst_ref in zip(casts_in, casts_out):
        dst_ref[...] = src_ref[...].astype(dst_ref.dtype)
    scratch = refs[2 * n_casts + 1:]

    @pl.when(bounded_ref[0] != 0)
    def _():
        kernel(q_t_ref, k_ref, v_t_ref, o_t_ref, *scratch, bounded=True)

    @pl.when(bounded_ref[0] == 0)
    def _():
        kernel(q_t_ref, k_ref, v_t_ref, o_t_ref, *scratch, bounded=False)


def _cast_block_rows(rows, n_steps):
    tile = 16
    per_step = -(-rows // (n_steps * tile)) * tile
    while rows % per_step:
        per_step += tile
    return per_step


def _attn_call(kernel, q_t, k, v_t, score_bound, *, seq, qk_rows, name, scratch_shapes=(),
               casts=(), heads_per_step=HEADS_PER_STEP):
    t = k.shape[0]
    n_batch = t // seq
    n_steps = q_t.shape[0] // (heads_per_step * qk_rows)
    k_lanes = k.shape[1] // n_steps
    v_rows = v_t.shape[0] // n_steps

    cast_specs = []
    for w in casts:
        rows = _cast_block_rows(w.shape[0], n_batch * n_steps)
        last = w.shape[0] // rows - 1
        cast_specs.append(pl.BlockSpec(
            (rows, w.shape[1]), lambda b, p, last=last: (jnp.minimum(b * n_steps + p, last), 0)))

    bounded = (score_bound <= MAX_SAFE_EXPONENT).astype(jnp.int32).reshape(1)
    return pl.pallas_call(
        functools.partial(_with_casts, kernel, len(casts)),
        out_shape=[jax.ShapeDtypeStruct(v_t.shape, BF16)]
        + [jax.ShapeDtypeStruct(w.shape, BF16) for w in casts],
        grid=(n_batch, n_steps),
        in_specs=[
            pl.BlockSpec(memory_space=pltpu.SMEM),
            pl.BlockSpec((heads_per_step * qk_rows, seq), lambda b, p: (p, b)),
            pl.BlockSpec((seq, k_lanes), lambda b, p: (b, p)),
            pl.BlockSpec((v_rows, seq), lambda b, p: (p, b)),
        ] + cast_specs,
        out_specs=[pl.BlockSpec((v_rows, seq), lambda b, p: (p, b))] + cast_specs,
        scratch_shapes=list(scratch_shapes),
        name=name,
        compiler_params=pltpu.CompilerParams(
            dimension_semantics=("arbitrary", "arbitrary") if scratch_shapes or casts
            else ("parallel", "parallel"), vmem_limit_bytes=V7X_VMEM_LIMIT),
    )(bounded, q_t, k, v_t, *casts)


def _score_bound(g_q, g_k, dim):
    return (jnp.max(jnp.abs(g_q.astype(F32))) * jnp.max(jnp.abs(g_k.astype(F32)))
            * (dim ** 0.5 * LOG2_E * BF16_NORM_MARGIN))


def _out_ffn_kernel(x_ref, oa_t_ref, ob_t_ref, w_o_ref, g_ffn_ref, w_gate_ref, w_up_ref,
                    w_down_ref, out_ref):
    o = jnp.concatenate([oa_t_ref[...], ob_t_ref[...]], axis=0).T
    h = x_ref[...] + _dot(o, w_o_ref[...])
    gn = h * lax.rsqrt(jnp.mean(h * h, axis=-1, keepdims=True) + EPS) * g_ffn_ref[...]
    gn = gn.astype(BF16)
    gate = _dot(gn, w_gate_ref[...])
    up = _dot(gn, w_up_ref[...])
    act = (gate * jax.nn.sigmoid(gate) * up).astype(BF16)
    out_ref[...] = h + _dot(act, w_down_ref[...])


def _out_ffn_call(x2, oa_t, ob_t, w_o, g_ffn, w_gate, w_up, w_down, *, tm):
    t, d = x2.shape
    const = lambda i: (0, 0)

    def resident(a):
        return pl.BlockSpec(a.shape, const, pipeline_mode=pl.Buffered(1))

    return pl.pallas_call(
        _out_ffn_kernel,
        out_shape=jax.ShapeDtypeStruct((t, d), x2.dtype),
        grid=(t // tm,),
        in_specs=[
            pl.BlockSpec((tm, d), lambda i: (i, 0)),
            pl.BlockSpec((oa_t.shape[0], tm), lambda i: (0, i)),
            pl.BlockSpec((ob_t.shape[0], tm), lambda i: (0, i)),
            resident(w_o), resident(g_ffn), resident(w_gate), resident(w_up), resident(w_down),
        ],
        out_specs=pl.BlockSpec((tm, d), lambda i: (i, 0)),
        name="out_ffn",
        compiler_params=pltpu.CompilerParams(
            dimension_semantics=("parallel",), vmem_limit_bytes=V7X_VMEM_LIMIT),
    )(x2, oa_t, ob_t, w_o, g_ffn, w_gate, w_up, w_down)


def _rope_tables_t(seq, dim):
    inv = ROPE_THETA ** (-jnp.arange(0, dim, 2, dtype=F32) / dim)
    ang = jnp.arange(seq, dtype=F32)[:, None] * inv[None, :]
    return jnp.cos(ang).T, jnp.sin(ang).T


def kernel(x, attn_norm_g, w_in, moba_q_norm_g, moba_k_norm_g, mla_q_a_norm_g, w_q_up,
           mla_kv_a_norm_g, w_kv_up, mla_q_norm_g, mla_k_norm_g, w_o, ffn_norm_g,
           w_gate, w_up, w_down):
    b, s, d = x.shape
    assert s % MOBA_BLOCK == 0 and s % ATTN_BLOCK == 0
    assert s % PROJ_TILE == 0 and (b * s) % FFN_TILE == 0
    cos_a, sin_a = _rope_tables_t(s, HEAD_DIM)
    cos_b, sin_b = _rope_tables_t(s, MLA_ROPE_DIM)
    h = x.reshape(b * s, d)
    for l in range(w_in.shape[0]):
        gains = jnp.concatenate([
            mla_q_a_norm_g[l].astype(F32), mla_kv_a_norm_g[l].astype(F32),
            mla_q_norm_g[l].astype(F32) * (MLA_QK_DIM ** -0.5 * LOG2_E),
            mla_k_norm_g[l].astype(F32)]).reshape(-1, 1)
        rope = jnp.concatenate([
            _gained_rope_table(moba_q_norm_g[l].astype(F32) * (HEAD_DIM ** -0.5 * LOG2_E),
                               cos_a, sin_a),
            _gained_rope_table(moba_k_norm_g[l].astype(F32), cos_a, sin_a), cos_b, sin_b], axis=0)
        qa_t, ka, va_t, qb_t, kb, vb_t = _proj_call(
            h, attn_norm_g[l].reshape(1, d), w_in[l].T, w_q_up[l].T.astype(BF16),
            w_kv_up[l].T.astype(BF16), gains, rope, seq=s, tm=PROJ_TILE)
        (oa_t,) = _attn_call(
            _moba_kernel, qa_t, ka, va_t,
            _score_bound(moba_q_norm_g[l], moba_k_norm_g[l], HEAD_DIM),
            seq=s, qk_rows=HEAD_DIM, name="moba_attn",
            scratch_shapes=[pltpu.VMEM((s, 2 * HEADS_PER_STEP * HEAD_DIM), BF16)])
        ob_t, w_o_b, w_gate_b, w_up_b, w_down_b = _attn_call(
            _mla_kernel, qb_t, kb, vb_t,
            _score_bound(mla_q_norm_g[l], mla_k_norm_g[l], MLA_QK_DIM),
            seq=s, qk_rows=MLA_QK_PAD, name="mla_attn",
            casts=(w_o[l], w_gate[l], w_up[l], w_down[l]), heads_per_step=MLA_HEADS_PER_STEP)
        h = _out_ffn_call(h, oa_t, ob_t, w_o_b, ffn_norm_g[l].reshape(1, d),
                          w_gate_b, w_up_b, w_down_b, tm=FFN_TILE)
    return h.reshape(b, s, d).astype(x.dtype)
```
